```python
import jax, jax.numpy as jnp
from jax import lax
import numpy as np

D_MODEL = 2048
BATCH = 8
SEQ = 2048
DEPTH = 1
DEC_BATCH = 2
DEC_SEQ = 4096
PAST_LEN = 128

GRID_W = 64
ATTN_WIDTH = D_MODEL // 2
GLA_WIDTH = D_MODEL // 2
HEAD_DIM = 128
N_HEADS = ATTN_WIDTH // HEAD_DIM
N_KV_HEADS = 2
GQA_GROUP = N_HEADS // N_KV_HEADS
Q_BLOCK = 128
ROPE_THETA = 10000.0
AXIS_DIM = HEAD_DIM // 2
GLA_HEADS = 4
GLA_DV = GLA_WIDTH // GLA_HEADS
GLA_DK = GLA_DV // 2
GATE_RANK = 16
GATE_NORMALIZER = 16.0
GLA_CHUNK = 64
PEER_HEADS = 8
N_KEYS = 128
N_EXPERTS = N_KEYS * N_KEYS
PEER_TOPK = 16
PEER_QDIM = D_MODEL // PEER_HEADS
PEER_HALF = PEER_QDIM // 2
PEER_TOKEN_BLOCK = 128
EPS = 1e-6
IN_SPLITS = (N_HEADS * HEAD_DIM, N_KV_HEADS * HEAD_DIM, N_KV_HEADS * HEAD_DIM,
             GLA_HEADS * GLA_DK, GLA_HEADS * GLA_DK, GLA_HEADS * GLA_DV, GLA_WIDTH,
             GATE_RANK, GATE_RANK)
IN_WIDTH = 4640

kernel_name = "hymba_gqa_gla_peer_bidir_encoder"


def rms_norm(x, g):
    xf = x.astype(jnp.float32)
    y = xf * lax.rsqrt(jnp.mean(xf * xf, axis=-1, keepdims=True) + EPS)
    return (y * g.astype(jnp.float32)).astype(x.dtype)


def axial_rope_tables(T):
    n_rows = T // GRID_W
    row = jnp.repeat(jnp.arange(n_rows, dtype=jnp.float32), GRID_W)
    col = jnp.tile(jnp.arange(GRID_W, dtype=jnp.float32), n_rows)
    freqs = ROPE_THETA ** (-jnp.arange(0, AXIS_DIM, 2, dtype=jnp.float32) / AXIS_DIM)
    ang = jnp.concatenate([row[:, None] * freqs, col[:, None] * freqs], axis=-1)
    return jnp.cos(ang), jnp.sin(ang)


def apply_rope(x, cos, sin):
    xf = x.astype(jnp.float32).reshape(*x.shape[:-1], HEAD_DIM // 2, 2)
    x0, x1 = xf[..., 0], xf[..., 1]
    c = cos[None, :, None, :]
    s = sin[None, :, None, :]
    out = jnp.stack([x0 * c - x1 * s, x0 * s + x1 * c], axis=-1).reshape(x.shape)
    return out.astype(x.dtype)


def gqa_attention(q, k, v):
    B, T = q.shape[:2]
    nb = T // Q_BLOCK
    qb = q.reshape(B, nb, Q_BLOCK, N_KV_HEADS, GQA_GROUP, HEAD_DIM).transpose(1, 0, 2, 3, 4, 5)
    kf = k.astype(jnp.float32)
    vf = v.astype(jnp.float32)
    scale = HEAD_DIM ** -0.5

    def block(qi):
        s = jnp.einsum('bqhgd,bkhd->bhgqk', qi.astype(jnp.float32), kf) * scale
        p = jax.nn.softmax(s, axis=-1)
        return jnp.einsum('bhgqk,bkhd->bqhgd', p, vf).astype(q.dtype)

    o = lax.map(block, qb)
    return o.transpose(1, 0, 2, 3, 4, 5).reshape(B, T, N_HEADS * HEAD_DIM)


def gla_direction(q, k, v, log_a):
    B, T, H, DK = q.shape
    DV = v.shape[-1]
    C = GLA_CHUNK
    n = T // C

    def chunks(z):
        return z.reshape(B, n, C, H, z.shape[-1]).transpose(1, 0, 3, 2, 4)

    lower = jnp.tril(jnp.ones((C, C), dtype=bool))[:, :, None]

    def step(S, inp):
        qi, ki, vi, gi = inp
        b = jnp.cumsum(gi, axis=-2)
        b_last = b[..., -1:, :]
        diff = b[..., :, None, :] - b[..., None, :, :]
        decay = jnp.exp(jnp.where(lower, diff, -jnp.inf))
        A = jnp.einsum('bhid,bhjd,bhijd->bhij', qi, ki, decay)
        o = (jnp.einsum('bhij,bhjv->bhiv', A, vi)
             + jnp.einsum('bhik,bhkv->bhiv', qi * jnp.exp(b), S))
        S = (S * jnp.exp(b_last[..., 0, :])[..., None]
             + jnp.einsum('bhjk,bhjv->bhkv', ki * jnp.exp(b_last - b), vi))
        return S, o

    S0 = jnp.zeros((B, H, DK, DV), dtype=jnp.float32)
    _, o = lax.scan(step, S0, (chunks(q), chunks(k), chunks(v), chunks(log_a)))
    return o.transpose(1, 0, 3, 2, 4).reshape(B, T, H, DV)


def peer_ffn(xn, w_q, sub_keys, u, v):
    B, T, D = xn.shape
    q = (xn @ w_q).astype(jnp.float32).reshape(B, T, PEER_HEADS, 2, PEER_HALF)
    s = jnp.einsum('bthpd,hpnd->bthpn', q, sub_keys.astype(jnp.float32))
    s1, i1 = lax.top_k(s[..., 0, :], PEER_TOPK)
    s2, i2 = lax.top_k(s[..., 1, :], PEER_TOPK)
    cand_s = (s1[..., :, None] + s2[..., None, :]).reshape(B, T, PEER_HEADS, PEER_TOPK * PEER_TOPK)
    cand_i = (i1[..., :, None] * N_KEYS + i2[..., None, :]).reshape(B, T, PEER_HEADS, PEER_TOPK * PEER_TOPK)
    top_s, pos = lax.top_k(cand_s, PEER_TOPK)
    idx = jnp.take_along_axis(cand_i, pos, axis=-1)
    gate = jax.nn.softmax(top_s, axis=-1)
    N = B * T
    nblk = N // PEER_TOKEN_BLOCK
    HK = PEER_HEADS * PEER_TOPK
    xb = xn.reshape(nblk, PEER_TOKEN_BLOCK, D)
    ib = idx.reshape(nblk, PEER_TOKEN_BLOCK, HK)
    gb = gate.reshape(nblk, PEER_TOKEN_BLOCK, HK)

    def block(args):
        xi, ii, gi = args
        ue = u[ii]
        h = jax.nn.gelu(jnp.einsum('nd,nkd->nk', xi.astype(jnp.float32), ue.astype(jnp.float32)), approximate=False)
        ve = v[ii]
        return jnp.einsum('nk,nkd->nd', gi * h, ve.astype(jnp.float32)).astype(xn.dtype)

    out = lax.map(block, (xb, ib, gb))
    return out.reshape(B, T, D)


def encoder_layer(x, norm_mix_g, w_in, q_norm_g, k_norm_g, w_gate_f2, b_gate_f, w_gate_b2, b_gate_b,
                  gla_norm_g, attn_out_norm_g, w_out, norm_ffn_g, peer_w_q, peer_sub_keys, peer_u, peer_v):
    B, T, _ = x.shape
    xn = rms_norm(x, norm_mix_g)
    z = xn @ w_in
    offs = []
    acc = 0
    for sz in IN_SPLITS[:-1]:
        acc += sz
        offs.append(acc)
    aq, ak, av, gq, gk, gv, gr, lr_f, lr_b = jnp.split(z, offs, axis=-1)

    cos, sin = axial_rope_tables(T)
    aq = apply_rope(rms_norm(aq.reshape(B, T, N_HEADS, HEAD_DIM), q_norm_g), cos, sin)
    ak = apply_rope(rms_norm(ak.reshape(B, T, N_KV_HEADS, HEAD_DIM), k_norm_g), cos, sin)
    av = av.reshape(B, T, N_KV_HEADS, HEAD_DIM)
    o_attn = rms_norm(gqa_attention(aq, ak, av), attn_out_norm_g)

    gq = gq.astype(jnp.float32).reshape(B, T, GLA_HEADS, GLA_DK) * (GLA_DK ** -0.5)
    gk = gk.astype(jnp.float32).reshape(B, T, GLA_HEADS, GLA_DK)
    gv = gv.astype(jnp.float32).reshape(B, T, GLA_HEADS, GLA_DV)
    log_a_f = jax.nn.log_sigmoid((lr_f @ w_gate_f2 + b_gate_f).astype(jnp.float32)) / GATE_NORMALIZER
    log_a_b = jax.nn.log_sigmoid((lr_b @ w_gate_b2 + b_gate_b).astype(jnp.float32)) / GATE_NORMALIZER
    log_a_f = log_a_f.reshape(B, T, GLA_HEADS, GLA_DK)
    log_a_b = log_a_b.reshape(B, T, GLA_HEADS, GLA_DK)
    o_f = gla_direction(gq, gk, gv, log_a_f)
    flip = lambda t: jnp.flip(t, axis=1)
    o_b = flip(gla_direction(flip(gq), flip(gk), flip(gv), flip(log_a_b)))
    o_gla = rms_norm((o_f + o_b).astype(x.dtype), gla_norm_g).reshape(B, T, GLA_WIDTH)
    o_gla = o_gla * jax.nn.silu(gr)

    h = x + jnp.concatenate([o_attn, o_gla], axis=-1) @ w_out

    h = h + peer_ffn(rms_norm(h, norm_ffn_g), peer_w_q, peer_sub_keys, peer_u, peer_v)
    return h


def setup_inputs(seed: int = 0) -> dict:
    key = jax.random.key(seed)
    ks = jax.random.split(key, 20)
    f32 = jnp.float32
    nrm = lambda k, shape, s: jax.random.normal(k, shape, f32) * s
    gain = lambda k, n: 1.0 + 0.02 * jax.random.normal(k, (n,), f32)
    return {
        "x_prompt": nrm(ks[0], (BATCH, SEQ, D_MODEL), 1.0),
        "x_sample": nrm(ks[1], (DEC_BATCH, DEC_SEQ, D_MODEL), 1.0),
        "norm_mix_g": gain(ks[2], D_MODEL),
        "w_in": nrm(ks[3], (D_MODEL, IN_WIDTH), D_MODEL ** -0.5),
        "q_norm_g": gain(ks[4], HEAD_DIM),
        "k_norm_g": gain(ks[5], HEAD_DIM),
        "w_gate_f2": nrm(ks[6], (GATE_RANK, GLA_HEADS * GLA_DK), GATE_RANK ** -0.5),
        "b_gate_f": nrm(ks[7], (GLA_HEADS * GLA_DK,), 0.1),
        "w_gate_b2": nrm(ks[8], (GATE_RANK, GLA_HEADS * GLA_DK), GATE_RANK ** -0.5),
        "b_gate_b": nrm(ks[9], (GLA_HEADS * GLA_DK,), 0.1),
        "gla_norm_g": gain(ks[10], GLA_DV),
        "attn_out_norm_g": gain(ks[11], ATTN_WIDTH),
        "w_out": nrm(ks[12], (ATTN_WIDTH + GLA_WIDTH, D_MODEL), (ATTN_WIDTH + GLA_WIDTH) ** -0.5),
        "norm_ffn_g": gain(ks[13], D_MODEL),
        "peer_w_q": nrm(ks[14], (D_MODEL, PEER_HEADS * PEER_QDIM), D_MODEL ** -0.5),
        "peer_sub_keys": nrm(ks[15], (PEER_HEADS, 2, N_KEYS, PEER_HALF), PEER_HALF ** -0.5),
        "peer_u": nrm(ks[16], (N_EXPERTS, D_MODEL), D_MODEL ** -0.5),
        "peer_v": nrm(ks[17], (N_EXPERTS, D_MODEL), 0.5),
        "final_norm_g": gain(ks[18], D_MODEL),
    }


def reference(x_prompt, x_sample, norm_mix_g, w_in, q_norm_g, k_norm_g, w_gate_f2, b_gate_f,
              w_gate_b2, b_gate_b, gla_norm_g, attn_out_norm_g, w_out, norm_ffn_g, peer_w_q,
              peer_sub_keys, peer_u, peer_v, final_norm_g):
    hp = x_prompt
    hs = x_sample
    for _ in range(DEPTH):
        hp = encoder_layer(hp, norm_mix_g, w_in, q_norm_g, k_norm_g, w_gate_f2, b_gate_f, w_gate_b2, b_gate_b,
                           gla_norm_g, attn_out_norm_g, w_out, norm_ffn_g, peer_w_q, peer_sub_keys, peer_u, peer_v)
        hs = encoder_layer(hs, norm_mix_g, w_in, q_norm_g, k_norm_g, w_gate_f2, b_gate_f, w_gate_b2, b_gate_b,
                           gla_norm_g, attn_out_norm_g, w_out, norm_ffn_g, peer_w_q, peer_sub_keys, peer_u, peer_v)
    y_prompt = rms_norm(hp, final_norm_g)
    y_sample = rms_norm(hs, final_norm_g)
    return (y_prompt, y_sample)
```

```python
import functools
import math

import jax
import jax.numpy as jnp
import numpy as np
from jax import lax
from jax.experimental import pallas as pl
from jax.experimental.pallas import tpu as pltpu

F32 = jnp.float32
BF16 = jnp.bfloat16

D_MODEL = 2048
GRID_W = 64
HEAD_DIM = 128
N_HEADS = 8
N_KV_HEADS = 2
GQA_GROUP = N_HEADS // N_KV_HEADS
ATTN_WIDTH = N_HEADS * HEAD_DIM
KV_WIDTH = N_KV_HEADS * HEAD_DIM
ROPE_THETA = 10000.0
AXIS_DIM = HEAD_DIM // 2
GLA_HEADS = 4
GLA_DK = 128
GLA_DV = 256
GLA_WIDTH = GLA_HEADS * GLA_DV
GATE_RANK = 16
GATE_NORMALIZER = 16.0
PEER_HEADS = 8
N_KEYS = 128
PEER_TOPK = 16
PEER_HALF = 128
PEER_QDIM = 2 * PEER_HALF
EPS = 1e-6

LANES = 128
SUBLANES = 8
VMEM_LIMIT = 56 * 1024 * 1024

PROJ_TM = 512
ATTN_TQ = 256
GLA_CHUNK = 64
ROUTE_TB = 256
PEER_TB = 512
PEER_TE = 1024

NT_DIMS = (((1,), (1,)), ((), ()))
TN_DIMS = (((0,), (0,)), ((), ()))


def _params(sem, vmem=VMEM_LIMIT):
    return pltpu.CompilerParams(dimension_semantics=sem, vmem_limit_bytes=vmem)


def _rms(x, g):
    return x * lax.rsqrt(jnp.mean(x * x, axis=-1, keepdims=True) + EPS) * g


def _qkv_kernel(x_ref, g_ref, w_ref, qg_ref, kg_ref, cos_ref, sin_ref, q_ref, k_ref, v_ref):
    xn = _rms(x_ref[...], g_ref[...]).astype(BF16)
    z = jnp.dot(xn, w_ref[...], preferred_element_type=F32)
    c = cos_ref[...]
    s = sin_ref[...]

    def norm_rope(hd, gain, scale):
        hn = _rms(hd, gain)
        rot = pltpu.roll(hn, HEAD_DIM // 2, axis=1)
        return ((hn * c + rot * s) * scale).astype(BF16)

    for hh in range(N_HEADS):
        sl = slice(hh * HEAD_DIM, (hh + 1) * HEAD_DIM)
        q_ref[:, sl] = norm_rope(z[:, sl], qg_ref[...], HEAD_DIM ** -0.5)
    for hh in range(N_KV_HEADS):
        sl = slice(hh * HEAD_DIM, (hh + 1) * HEAD_DIM)
        zsl = slice(ATTN_WIDTH + hh * HEAD_DIM, ATTN_WIDTH + (hh + 1) * HEAD_DIM)
        k_ref[:, sl] = norm_rope(z[:, zsl], kg_ref[...], 1.0)
    v_ref[...] = z[:, ATTN_WIDTH + KV_WIDTH:].astype(BF16)


def _qkv_proj(x2, g, w, qg, kg, cos, sin, seq):
    n, d = x2.shape
    tm = min(PROJ_TM, seq)
    nt = seq // tm
    wq = w.shape[1]
    full = lambda i: (0, 0)
    return pl.pallas_call(
        _qkv_kernel,
        grid=(n // tm,),
        in_specs=[
            pl.BlockSpec((tm, d), lambda i: (i, 0)),
            pl.BlockSpec((1, d), full),
            pl.BlockSpec((d, wq), full),
            pl.BlockSpec((1, HEAD_DIM), full),
            pl.BlockSpec((1, HEAD_DIM), full),
            pl.BlockSpec((tm, HEAD_DIM), lambda i: (i % nt, 0)),
            pl.BlockSpec((tm, HEAD_DIM), lambda i: (i % nt, 0)),
        ],
        out_specs=[
            pl.BlockSpec((tm, ATTN_WIDTH), lambda i: (i, 0)),
            pl.BlockSpec((tm, KV_WIDTH), lambda i: (i, 0)),
            pl.BlockSpec((tm, KV_WIDTH), lambda i: (i, 0)),
        ],
        out_shape=[
            jax.ShapeDtypeStruct((n, ATTN_WIDTH), BF16),
            jax.ShapeDtypeStruct((n, KV_WIDTH), BF16),
            jax.ShapeDtypeStruct((n, KV_WIDTH), BF16),
        ],
        compiler_params=_params(("parallel",)),
        name="qkv_proj",
    )(x2, g, w, qg, kg, cos, sin)


def _log_sigmoid(x):
    return jnp.minimum(x, 0.0) - jnp.log(1.0 + jnp.exp(-jnp.abs(x)))


def _gla_proj_kernel(x_ref, g_ref, w_ref, wg_ref, bg_ref,
                     gq_ref, gk_ref, gv_ref, sg_ref, laf_ref, lab_ref):
    xn = _rms(x_ref[...], g_ref[...]).astype(BF16)
    z = jnp.dot(xn, w_ref[...], preferred_element_type=F32)
    kw = GLA_HEADS * GLA_DK
    gq_ref[...] = (z[:, :kw] * (GLA_DK ** -0.5)).astype(BF16)
    gk_ref[...] = z[:, kw:2 * kw].astype(BF16)
    gv_ref[...] = z[:, 2 * kw:2 * kw + GLA_WIDTH].astype(BF16)
    gr = z[:, 2 * kw + GLA_WIDTH:2 * kw + 2 * GLA_WIDTH]
    sg_ref[...] = (gr * (1.0 / (1.0 + jnp.exp(-gr)))).astype(BF16)
    lr = z[:, 2 * kw + 2 * GLA_WIDTH:]
    pre = jnp.dot(lr, wg_ref[...], preferred_element_type=F32,
                  precision=lax.Precision.HIGHEST) + bg_ref[...]
    la = _log_sigmoid(pre) * (1.0 / GATE_NORMALIZER)
    laf_ref[...] = la[:, :kw]
    lab_ref[...] = la[:, kw:]


def _gla_proj(x2, g, w, wg, bg):
    n, d = x2.shape
    tm = PROJ_TM
    kw = GLA_HEADS * GLA_DK
    full = lambda i: (0, 0)
    row = lambda i: (i, 0)
    return pl.pallas_call(
        _gla_proj_kernel,
        grid=(n // tm,),
        in_specs=[
            pl.BlockSpec((tm, d), row),
            pl.BlockSpec((1, d), full),
            pl.BlockSpec(w.shape, full),
            pl.BlockSpec(wg.shape, full),
            pl.BlockSpec(bg.shape, full),
        ],
        out_specs=[
            pl.BlockSpec((tm, kw), row),
            pl.BlockSpec((tm, kw), row),
            pl.BlockSpec((tm, GLA_WIDTH), row),
            pl.BlockSpec((tm, GLA_WIDTH), row),
            pl.BlockSpec((tm, kw), row),
            pl.BlockSpec((tm, kw), row),
        ],
        out_shape=[
            jax.ShapeDtypeStruct((n, kw), BF16),
            jax.ShapeDtypeStruct((n, kw), BF16),
            jax.ShapeDtypeStruct((n, GLA_WIDTH), BF16),
            jax.ShapeDtypeStruct((n, GLA_WIDTH), BF16),
            jax.ShapeDtypeStruct((n, kw), F32),
            jax.ShapeDtypeStruct((n, kw), F32),
        ],
        compiler_params=_params(("parallel",)),
        name="gla_proj",
    )(x2, g, w, wg, bg)


def _attn_kernel(q_ref, k_ref, v_ref, o_ref):
    k = k_ref[0]
    v = v_ref[0]
    for hh in range(GQA_GROUP):
        sl = slice(hh * HEAD_DIM, (hh + 1) * HEAD_DIM)
        s = lax.dot_general(q_ref[0, :, sl], k, NT_DIMS, preferred_element_type=F32)
        m = jnp.max(s, axis=-1, keepdims=True)
        p = jnp.exp(s - m)
        l = jnp.sum(p, axis=-1, keepdims=True)
        o = jnp.dot(p.astype(BF16), v, preferred_element_type=F32)
        o_ref[0, :, sl] = (o / l).astype(BF16)


def _attention(q, k, v):
    b, t, _ = q.shape
    tq = min(ATTN_TQ, t)
    gw = GQA_GROUP * HEAD_DIM
    return pl.pallas_call(
        _attn_kernel,
        grid=(b, N_KV_HEADS, t // tq),
        in_specs=[
            pl.BlockSpec((1, tq, gw), lambda bi, g, qi: (bi, qi, g)),
            pl.BlockSpec((1, t, HEAD_DIM), lambda bi, g, qi: (bi, 0, g)),
            pl.BlockSpec((1, t, HEAD_DIM), lambda bi, g, qi: (bi, 0, g)),
        ],
        out_specs=pl.BlockSpec((1, tq, gw), lambda bi, g, qi: (bi, qi, g)),
        out_shape=jax.ShapeDtypeStruct((b, t, ATTN_WIDTH), BF16),
        compiler_params=_params(("parallel", "parallel", "arbitrary")),
        name="gqa_attention",
    )(q, k, v)


def _split_dot(tri, g):
    hi = g.astype(BF16)
    lo = (g - hi.astype(F32)).astype(BF16)
    return (jnp.dot(tri, hi, preferred_element_type=F32)
            + jnp.dot(tri, lo, preferred_element_type=F32))


def _gla_kernel(q_ref, k_ref, v_ref, sg_ref, gf_ref, gb_ref, gain_ref, o_ref, acc_ref,
                sf_ref, sb_ref):
    t = q_ref.shape[1]
    c = GLA_CHUNK
    n = t // c
    mid = c // 2
    ri = lax.broadcasted_iota(jnp.int32, (c, c), 0)
    ci = lax.broadcasted_iota(jnp.int32, (c, c), 1)
    lower = ri >= ci
    tri_lo = lower.astype(BF16)
    tri_up = (ri <= ci).astype(BF16)
    sf_ref[...] = jnp.zeros_like(sf_ref)
    sb_ref[...] = jnp.zeros_like(sb_ref)

    def chunk(idx, s_ref, g_ref, tri, first_row, mask):
        rows = pl.ds(pl.multiple_of(idx * c, c), c)
        q = q_ref[0, rows, :].astype(F32)
        k = k_ref[0, rows, :].astype(F32)
        v = v_ref[0, rows, :]
        b = _split_dot(tri, g_ref[0, rows, :])
        tot = b[c - 1:c, :] if first_row is False else b[0:1, :]
        ref = b[mid:mid + 1, :]
        qt = (q * jnp.exp(b - ref)).astype(BF16)
        kt = (k * jnp.exp(ref - b)).astype(BF16)
        a = lax.dot_general(qt, kt, NT_DIMS, preferred_element_type=F32)
        a = jnp.where(mask, a, 0.0).astype(BF16)
        st = s_ref[...]
        o = (jnp.dot(a, v, preferred_element_type=F32)
             + lax.dot_general((q * jnp.exp(b)).astype(BF16), st.astype(BF16), NT_DIMS,
                               preferred_element_type=F32))
        kd = (k * jnp.exp(tot - b)).astype(BF16)
        s_ref[...] = st * jnp.exp(tot) + lax.dot_general(v, kd, TN_DIMS,
                                                          preferred_element_type=F32)
        return rows, o

    def fwd(i):
        return chunk(i, sf_ref, gf_ref, tri_lo, False, lower)

    def bwd(i):
        return chunk(n - 1 - i, sb_ref, gb_ref, tri_up, True, ri <= ci)

    def finish(rows, o):
        y = _rms(o, gain_ref[...]) * sg_ref[0, rows, :].astype(F32)
        o_ref[0, rows, :] = y.astype(BF16)

    def first_half(i, carry):
        rf, of = fwd(i)
        rb, ob = bwd(i)
        acc_ref[rf, :] = of
        acc_ref[rb, :] = ob
        return carry

    def second_half(i, carry):
        rf, of = fwd(i)
        rb, ob = bwd(i)
        finish(rf, acc_ref[rf, :] + of)
        finish(rb, acc_ref[rb, :] + ob)
        return carry

    lax.fori_loop(0, n // 2, first_half, 0)
    lax.fori_loop(n // 2, n, second_half, 0)


def _gla(gq, gk, gv, sg, laf, lab, gain):
    b, t, _ = gq.shape
    kspec = pl.BlockSpec((1, t, GLA_DK), lambda bi, h: (bi, 0, h))
    vspec = pl.BlockSpec((1, t, GLA_DV), lambda bi, h: (bi, 0, h))
    return pl.pallas_call(
        _gla_kernel,
        grid=(b, GLA_HEADS),
        in_specs=[kspec, kspec, vspec, vspec, kspec, kspec,
                  pl.BlockSpec((1, GLA_DV), lambda bi, h: (0, 0))],
        out_specs=vspec,
        out_shape=jax.ShapeDtypeStruct((b, t, GLA_WIDTH), BF16),
        scratch_shapes=[
            pltpu.VMEM((t, GLA_DV), F32),
            pltpu.VMEM((GLA_DV, GLA_DK), F32),
            pltpu.VMEM((GLA_DV, GLA_DK), F32),
        ],
        compiler_params=_params(("parallel", "parallel")),
        name="gla_bidir",
    )(gq, gk, gv, sg, laf, lab, gain)


def _out_proj_kernel(x_ref, oa_ref, og_ref, ag_ref, wa_ref, wg_ref, fg_ref, h_ref, hn_ref):
    oa = _rms(oa_ref[...].astype(F32), ag_ref[...]).astype(BF16)
    h = (x_ref[...]
         + jnp.dot(oa, wa_ref[...], preferred_element_type=F32)
         + jnp.dot(og_ref[...], wg_ref[...], preferred_element_type=F32))
    h_ref[...] = h
    hn_ref[...] = _rms(h, fg_ref[...]).astype(BF16)


def _out_proj(x2, oa, og, ag, wa, wg, fg):
    n, d = x2.shape
    tm = PROJ_TM
    full = lambda i: (0, 0)
    row = lambda i: (i, 0)
    return pl.pallas_call(
        _out_proj_kernel,
        grid=(n // tm,),
        in_specs=[
            pl.BlockSpec((tm, d), row),
            pl.BlockSpec((tm, ATTN_WIDTH), row),
            pl.BlockSpec((tm, GLA_WIDTH), row),
            pl.BlockSpec((1, ATTN_WIDTH), full),
            pl.BlockSpec(wa.shape, full),
            pl.BlockSpec(wg.shape, full),
            pl.BlockSpec((1, d), full),
        ],
        out_specs=[pl.BlockSpec((tm, d), row), pl.BlockSpec((tm, d), row)],
        out_shape=[jax.ShapeDtypeStruct((n, d), F32), jax.ShapeDtypeStruct((n, d), BF16)],
        compiler_params=_params(("parallel",)),
        name="out_proj",
    )(x2, oa, og, ag, wa, wg, fg)


def _cmp_exchange(rows, i, j):
    a, b = rows[i], rows[j]
    if b is None:
        return
    if a is None:
        rows[i], rows[j] = b, None
        return
    rows[i], rows[j] = jnp.maximum(a, b), jnp.minimum(a, b)


def _bitonic_merge_desc(rows):
    n = len(rows)
    j = n // 2
    while j >= 1:
        for i in range(n):
            l = i ^ j
            if l > i:
                _cmp_exchange(rows, i, l)
        j //= 2


def _bitonic_sort_desc(rows):
    n = len(rows)
    k = 2
    while k <= n:
        j = k // 2
        while j >= 1:
            for i in range(n):
                l = i ^ j
                if l > i:
                    if (i & k) == 0:
                        _cmp_exchange(rows, i, l)
                    else:
                        _cmp_exchange(rows, l, i)
            j //= 2
        k *= 2


def _top_of_union(a, b):
    n = len(a)
    out = []
    for i in range(n):
        x, y = a[i], b[n - 1 - i]
        out.append(x if y is None else (y if x is None else jnp.maximum(x, y)))
    _bitonic_merge_desc(out)
    return out


def _top16_rows(s):
    l = s.shape[1]
    s3 = s.reshape(N_KEYS // SUBLANES, SUBLANES, l)
    rows = [s3[i] for i in range(N_KEYS // SUBLANES)]
    _bitonic_sort_desc(rows)
    for shift in (4, 2, 1):
        other = [pltpu.roll(r, shift, axis=0) for r in rows]
        rows = _top_of_union(rows, other)
    return rows


_CANDIDATES = [(i, j) for i in range(PEER_TOPK) for j in range(PEER_TOPK)
               if (i + 1) * (j + 1) <= PEER_TOPK]


def _route_kernel(hn_ref, wqt_ref, keys_ref, s2_ref, e2_ref, beta_ref, w1_ref, qt_ref):
    hh = pl.program_id(1)

    @pl.when(hh == 0)
    def _():
        qt_ref[...] = lax.dot_general(wqt_ref[...], hn_ref[...], NT_DIMS,
                                      preferred_element_type=F32).astype(BF16)

    base = pl.multiple_of(hh * PEER_QDIM, PEER_QDIM)
    q1 = qt_ref[pl.ds(base, PEER_HALF), :]
    q2 = qt_ref[pl.ds(base + PEER_HALF, PEER_HALF), :]
    s1 = jnp.dot(keys_ref[0, 0], q1, preferred_element_type=F32)
    s2 = jnp.dot(keys_ref[0, 1], q2, preferred_element_type=F32)
    a = _top16_rows(s1)
    b = _top16_rows(s2)
    cand = [a[i] + b[j] for (i, j) in _CANDIDATES]
    cand += [None] * (64 - len(cand))
    groups = []
    for gi in range(4):
        grp = cand[gi * 16:(gi + 1) * 16]
        _bitonic_sort_desc(grp)
        groups.append(grp)
    top = _top_of_union(_top_of_union(groups[0], groups[1]),
                        _top_of_union(groups[2], groups[3]))
    tau = top[PEER_TOPK - 1]
    z = jnp.zeros_like(tau)
    for tk in top:
        z = z + jnp.exp(tk - top[0])
    inv_z = 1.0 / z
    tb = s1.shape[1]
    full = lambda r: jnp.broadcast_to(r[0:1, :], (N_KEYS, tb))
    tau_f = full(tau)
    beta = jnp.full((N_KEYS, tb), jnp.inf, F32)
    for j in range(PEER_TOPK):
        bj = full(b[j])
        beta = jnp.where(s1 + bj >= tau_f, bj, beta)
    s2_ref[0] = s2
    e2_ref[0] = jnp.exp(s2 - full(b[0]))
    beta_ref[0] = beta
    w1_ref[0] = jnp.exp(s1 - full(a[0])) * full(inv_z)


def _peer_route(hn, wqt, keys):
    n, d = hn.shape
    tb = ROUTE_TB
    tab = jax.ShapeDtypeStruct((PEER_HEADS, N_KEYS, n), F32)
    tspec = pl.BlockSpec((1, N_KEYS, tb), lambda i, h: (h, 0, i))
    return pl.pallas_call(
        _route_kernel,
        grid=(n // tb, PEER_HEADS),
        in_specs=[
            pl.BlockSpec((tb, d), lambda i, h: (i, 0)),
            pl.BlockSpec(wqt.shape, lambda i, h: (0, 0)),
            pl.BlockSpec((1, 2, N_KEYS, PEER_HALF), lambda i, h: (h, 0, 0, 0)),
        ],
        out_specs=[tspec, tspec, tspec, tspec],
        out_shape=[tab, tab, tab, tab],
        scratch_shapes=[pltpu.VMEM((PEER_HEADS * PEER_QDIM, tb), BF16)],
        compiler_params=_params(("parallel", "arbitrary")),
        name="peer_route",
    )(hn, wqt, keys)


def _gelu(x):
    return 0.5 * x * (1.0 + lax.erf(x * (2.0 ** -0.5)))


def _peer_kernel(hn_ref, h_ref, u_ref, vt_ref, s2_ref, e2_ref, beta_ref, w1_ref, fg_ref,
                 y_ref, acc_ref, ht_ref, w_ref):
    j = pl.program_id(1)
    nj = pl.num_programs(1)
    te, tb = ht_ref.shape
    rows_per_step = te // N_KEYS
    lane_groups = tb // LANES

    @pl.when(j == 0)
    def _():
        acc_ref[...] = jnp.zeros_like(acc_ref)

    ht_ref[...] = lax.dot_general(u_ref[...], hn_ref[...], NT_DIMS,
                                  preferred_element_type=F32)

    row0 = pl.multiple_of(j * rows_per_step, SUBLANES)

    def lane_group(lg, carry):
        lanes = pl.ds(pl.multiple_of(lg * LANES, LANES), LANES)
        for rr in range(rows_per_step):
            rows = slice(rr * N_KEYS, (rr + 1) * N_KEYS)
            gate = jnp.zeros((N_KEYS, LANES), F32)
            for hh in range(PEER_HEADS):
                beta = beta_ref[hh, pl.ds(row0, rows_per_step), lanes][rr:rr + 1, :]
                w1 = w1_ref[hh, pl.ds(row0, rows_per_step), lanes][rr:rr + 1, :]
                gate = gate + jnp.where(s2_ref[hh, :, lanes] >= beta, e2_ref[hh, :, lanes], 0.0) * w1
            w_ref[rows, lanes] = (gate * _gelu(ht_ref[rows, lanes])).astype(BF16)
        return carry

    lax.fori_loop(0, lane_groups, lane_group, 0)
    acc_ref[...] += jnp.dot(vt_ref[...], w_ref[...], preferred_element_type=F32)

    @pl.when(j == nj - 1)
    def _():
        h = h_ref[...] + acc_ref[...].T
        y_ref[...] = _rms(h, fg_ref[...])


def _peer_dense(hn, h, u, vt, s2, e2, beta, w1, fg):
    n, d = hn.shape
    e = u.shape[0]
    tb, te = PEER_TB, PEER_TE
    once = pl.Buffered(1)
    tspec = pl.BlockSpec((PEER_HEADS, N_KEYS, tb), lambda i, j: (0, 0, i), pipeline_mode=once)
    return pl.pallas_call(
        _peer_kernel,
        grid=(n // tb, e // te),
        in_specs=[
            pl.BlockSpec((tb, d), lambda i, j: (i, 0), pipeline_mode=once),
            pl.BlockSpec((tb, d), lambda i, j: (i, 0), pipeline_mode=once),
            pl.BlockSpec((te, d), lambda i, j: (j, 0)),
            pl.BlockSpec((d, te), lambda i, j: (0, j)),
            tspec, tspec, tspec, tspec,
            pl.BlockSpec((1, d), lambda i, j: (0, 0)),
        ],
        out_specs=pl.BlockSpec((tb, d), lambda i, j: (i, 0)),
        out_shape=jax.ShapeDtypeStruct((n, d), F32),
        scratch_shapes=[
            pltpu.VMEM((d, tb), F32),
            pltpu.VMEM((te, tb), F32),
            pltpu.VMEM((te, tb), BF16),
        ],
        compiler_params=_params(("parallel", "arbitrary")),
        name="peer_dense",
    )(hn, h, u, vt, s2, e2, beta, w1, fg)


def _rope_tables(t):
    n_rows = t // GRID_W
    row = jnp.repeat(jnp.arange(n_rows, dtype=F32), GRID_W)
    col = jnp.tile(jnp.arange(GRID_W, dtype=F32), n_rows)
    freqs = ROPE_THETA ** (-jnp.arange(0, AXIS_DIM, 2, dtype=F32) / AXIS_DIM)
    ang = jnp.concatenate([row[:, None] * freqs, col[:, None] * freqs], axis=-1)
    cos, sin = jnp.cos(ang), jnp.sin(ang)
    return jnp.concatenate([cos, cos], axis=-1), jnp.concatenate([-sin, sin], axis=-1)


def _prepare(norm_mix_g, w_in, q_norm_g, k_norm_g, w_gate_f2, b_gate_f, w_gate_b2, b_gate_b,
             gla_norm_g, attn_out_norm_g, w_out, norm_ffn_g, peer_w_q, peer_sub_keys, peer_u,
             peer_v, final_norm_g):
    perm = np.concatenate([np.arange(0, HEAD_DIM, 2), np.arange(1, HEAD_DIM, 2)])
    kw = GLA_HEADS * GLA_DK
    o_k = ATTN_WIDTH
    o_v = o_k + KV_WIDTH
    o_g = o_v + KV_WIDTH
    o_lr = o_g + 2 * kw + 2 * GLA_WIDTH
    wq = w_in[:, :o_k].reshape(D_MODEL, N_HEADS, HEAD_DIM)[:, :, perm].reshape(D_MODEL, o_k)
    wk = w_in[:, o_k:o_v].reshape(D_MODEL, N_KV_HEADS, HEAD_DIM)[:, :, perm].reshape(D_MODEL, KV_WIDTH)
    w_qkv = jnp.concatenate([wq, wk, w_in[:, o_v:o_g]], axis=1).astype(BF16)
    w_lr = jnp.pad(w_in[:, o_lr:], ((0, 0), (0, LANES - 2 * GATE_RANK)))
    w_gla = jnp.concatenate([w_in[:, o_g:o_lr], w_lr], axis=1).astype(BF16)
    wg = jnp.zeros((LANES, 2 * kw), F32)
    wg = wg.at[:GATE_RANK, :kw].set(w_gate_f2).at[GATE_RANK:2 * GATE_RANK, kw:].set(w_gate_b2)
    bg = jnp.concatenate([b_gate_f, b_gate_b])[None, :]
    return dict(
        norm_mix_g=norm_mix_g[None, :], w_qkv=w_qkv, w_gla=w_gla, wg=wg, bg=bg,
        qg=q_norm_g[perm][None, :], kg=k_norm_g[perm][None, :],
        gla_gain=gla_norm_g[None, :], attn_gain=attn_out_norm_g[None, :],
        wo_a=w_out[:ATTN_WIDTH].astype(BF16), wo_g=w_out[ATTN_WIDTH:].astype(BF16),
        ffn_g=norm_ffn_g[None, :], wqt=peer_w_q.T.astype(BF16),
        keys=peer_sub_keys.astype(BF16), u=peer_u.astype(BF16), vt=peer_v.T.astype(BF16),
        final_g=final_norm_g[None, :],
    )


def _layer(x, p):
    b, t, d = x.shape
    n = b * t
    x2 = x.reshape(n, d)
    cos, sin = _rope_tables(t)
    q, k, v = _qkv_proj(x2, p["norm_mix_g"], p["w_qkv"], p["qg"], p["kg"], cos, sin, t)
    gq, gk, gv, sg, laf, lab = _gla_proj(x2, p["norm_mix_g"], p["w_gla"], p["wg"], p["bg"])
    oa = _attention(q.reshape(b, t, -1), k.reshape(b, t, -1), v.reshape(b, t, -1))
    r3 = lambda a: a.reshape(b, t, -1)
    og = _gla(r3(gq), r3(gk), r3(gv), r3(sg), r3(laf), r3(lab), p["gla_gain"])
    h, hn = _out_proj(x2, oa.reshape(n, -1), og.reshape(n, -1), p["attn_gain"],
                      p["wo_a"], p["wo_g"], p["ffn_g"])
    s2, e2, beta, w1 = _peer_route(hn, p["wqt"], p["keys"])
    y = _peer_dense(hn, h, p["u"], p["vt"], s2, e2, beta, w1, p["final_g"])
    return y.reshape(b, t, d)


def kernel(x_prompt, x_sample, norm_mix_g, w_in, q_norm_g, k_norm_g, w_gate_f2, b_gate_f,
           w_gate_b2, b_gate_b, gla_norm_g, attn_out_norm_g, w_out, norm_ffn_g, peer_w_q,
           peer_sub_keys, peer_u, peer_v, final_norm_g):
    p = _prepare(norm_mix_g, w_in, q_norm_g, k_norm_g, w_gate_f2, b_gate_f, w_gate_b2,
                 b_gate_b, gla_norm_g, attn_out_norm_g, w_out, norm_ffn_g, peer_w_q,
                 peer_sub_keys, peer_u, peer_v, final_norm_g)
    return (_layer(x_prompt, p), _layer(x_sample, p))
```

```python
import functools
import math

import jax
import jax.numpy as jnp
import numpy as np
from jax import lax
from jax.experimental import pallas as pl
from jax.experimental.pallas import tpu as pltpu

F32 = jnp.float32
BF16 = jnp.bfloat16

D_MODEL = 2048
GRID_W = 64
HEAD_DIM = 128
N_HEADS = 8
N_KV_HEADS = 2
GQA_GROUP = N_HEADS // N_KV_HEADS
ATTN_WIDTH = N_HEADS * HEAD_DIM
KV_WIDTH = N_KV_HEADS * HEAD_DIM
ROPE_THETA = 10000.0
AXIS_DIM = HEAD_DIM // 2
GLA_HEADS = 4
GLA_DK = 128
GLA_DV = 256
GLA_WIDTH = GLA_HEADS * GLA_DV
GATE_RANK = 16
GATE_NORMALIZER = 16.0
PEER_HEADS = 8
N_KEYS = 128
PEER_TOPK = 16
PEER_HALF = 128
PEER_QDIM = 2 * PEER_HALF
EPS = 1e-6

LANES = 128
SUBLANES = 8
VMEM_LIMIT = 56 * 1024 * 1024

PROJ_TM = 512
ATTN_TQ = 256
GLA_CHUNK = 64
ROUTE_TB = 256
PEER_TB = 512
PEER_TE = 1024

NT_DIMS = (((1,), (1,)), ((), ()))
TN_DIMS = (((0,), (0,)), ((), ()))


def _params(sem, vmem=VMEM_LIMIT):
    return pltpu.CompilerParams(dimension_semantics=sem, vmem_limit_bytes=vmem)


def _rms(x, g):
    return x * lax.rsqrt(jnp.mean(x * x, axis=-1, keepdims=True) + EPS) * g


def _qkv_kernel(x_ref, g_ref, w_ref, qg_ref, kg_ref, cos_ref, sin_ref, q_ref, k_ref, v_ref):
    xn = _rms(x_ref[...], g_ref[...]).astype(BF16)
    z = jnp.dot(xn, w_ref[...], preferred_element_type=F32)
    c = cos_ref[...]
    s = sin_ref[...]

    def norm_rope(hd, gain, scale):
        hn = _rms(hd, gain)
        rot = pltpu.roll(hn, HEAD_DIM // 2, axis=1)
        return ((hn * c + rot * s) * scale).astype(BF16)

    for hh in range(N_HEADS):
        sl = slice(hh * HEAD_DIM, (hh + 1) * HEAD_DIM)
        q_ref[:, sl] = norm_rope(z[:, sl], qg_ref[...], HEAD_DIM ** -0.5)
    for hh in range(N_KV_HEADS):
        sl = slice(hh * HEAD_DIM, (hh + 1) * HEAD_DIM)
        zsl = slice(ATTN_WIDTH + hh * HEAD_DIM, ATTN_WIDTH + (hh + 1) * HEAD_DIM)
        k_ref[:, sl] = norm_rope(z[:, zsl], kg_ref[...], 1.0)
    v_ref[...] = z[:, ATTN_WIDTH + KV_WIDTH:].astype(BF16)


def _qkv_proj(x2, g, w, qg, kg, cos, sin, seq):
    n, d = x2.shape
    tm = min(PROJ_TM, seq)
    nt = seq // tm
    wq = w.shape[1]
    full = lambda i: (0, 0)
    return pl.pallas_call(
        _qkv_kernel,
        grid=(n // tm,),
        in_specs=[
            pl.BlockSpec((tm, d), lambda i: (i, 0)),
            pl.BlockSpec((1, d), full),
            pl.BlockSpec((d, wq), full),
            pl.BlockSpec((1, HEAD_DIM), full),
            pl.BlockSpec((1, HEAD_DIM), full),
            pl.BlockSpec((tm, HEAD_DIM), lambda i: (i % nt, 0)),
            pl.BlockSpec((tm, HEAD_DIM), lambda i: (i % nt, 0)),
        ],
        out_specs=[
            pl.BlockSpec((tm, ATTN_WIDTH), lambda i: (i, 0)),
            pl.BlockSpec((tm, KV_WIDTH), lambda i: (i, 0)),
            pl.BlockSpec((tm, KV_WIDTH), lambda i: (i, 0)),
        ],
        out_shape=[
            jax.ShapeDtypeStruct((n, ATTN_WIDTH), BF16),
            jax.ShapeDtypeStruct((n, KV_WIDTH), BF16),
            jax.ShapeDtypeStruct((n, KV_WIDTH), BF16),
        ],
        compiler_params=_params(("parallel",)),
        name="qkv_proj",
    )(x2, g, w, qg, kg, cos, sin)


def _log_sigmoid(x):
    return jnp.minimum(x, 0.0) - jnp.log(1.0 + jnp.exp(-jnp.abs(x)))


def _gla_proj_kernel(x_ref, g_ref, w_ref, wg_ref, bg_ref,
                     gq_ref, gk_ref, gv_ref, sg_ref, laf_ref, lab_ref):
    xn = _rms(x_ref[...], g_ref[...]).astype(BF16)
    z = jnp.dot(xn, w_ref[...], preferred_element_type=F32)
    kw = GLA_HEADS * GLA_DK
    gq_ref[...] = (z[:, :kw] * (GLA_DK ** -0.5)).astype(BF16)
    gk_ref[...] = z[:, kw:2 * kw].astype(BF16)
    gv_ref[...] = z[:, 2 * kw:2 * kw + GLA_WIDTH].astype(BF16)
    gr = z[:, 2 * kw + GLA_WIDTH:2 * kw + 2 * GLA_WIDTH]
    sg_ref[...] = (gr * (1.0 / (1.0 + jnp.exp(-gr)))).astype(BF16)
    lr = z[:, 2 * kw + 2 * GLA_WIDTH:]
    pre = jnp.dot(lr, wg_ref[...], preferred_element_type=F32,
                  precision=lax.Precision.HIGHEST) + bg_ref[...]
    la = _log_sigmoid(pre) * (1.0 / GATE_NORMALIZER)
    laf_ref[...] = la[:, :kw]
    lab_ref[...] = la[:, kw:]


def _gla_proj(x2, g, w, wg, bg):
    n, d = x2.shape
    tm = PROJ_TM
    kw = GLA_HEADS * GLA_DK
    full = lambda i: (0, 0)
    row = lambda i: (i, 0)
    return pl.pallas_call(
        _gla_proj_kernel,
        grid=(n // tm,),
        in_specs=[
            pl.BlockSpec((tm, d), row),
            pl.BlockSpec((1, d), full),
            pl.BlockSpec(w.shape, full),
            pl.BlockSpec(wg.shape, full),
            pl.BlockSpec(bg.shape, full),
        ],
        out_specs=[
            pl.BlockSpec((tm, kw), row),
            pl.BlockSpec((tm, kw), row),
            pl.BlockSpec((tm, GLA_WIDTH), row),
            pl.BlockSpec((tm, GLA_WIDTH), row),
            pl.BlockSpec((tm, kw), row),
            pl.BlockSpec((tm, kw), row),
        ],
        out_shape=[
            jax.ShapeDtypeStruct((n, kw), BF16),
            jax.ShapeDtypeStruct((n, kw), BF16),
            jax.ShapeDtypeStruct((n, GLA_WIDTH), BF16),
            jax.ShapeDtypeStruct((n, GLA_WIDTH), BF16),
            jax.ShapeDtypeStruct((n, kw), F32),
            jax.ShapeDtypeStruct((n, kw), F32),
        ],
        compiler_params=_params(("parallel",)),
        name="gla_proj",
    )(x2, g, w, wg, bg)


def _attn_kernel(q_ref, k_ref, v_ref, o_ref):
    k = k_ref[0]
    v = v_ref[0]
    for hh in range(GQA_GROUP):
        sl = slice(hh * HEAD_DIM, (hh + 1) * HEAD_DIM)
        s = lax.dot_general(q_ref[0, :, sl], k, NT_DIMS, preferred_element_type=F32)
        m = jnp.max(s, axis=-1, keepdims=True)
        p = jnp.exp(s - m)
        l = jnp.sum(p, axis=-1, keepdims=True)
        o = jnp.dot(p.astype(BF16), v, preferred_element_type=F32)
        o_ref[0, :, sl] = (o / l).astype(BF16)


def _attention(q, k, v):
    b, t, _ = q.shape
    tq = min(ATTN_TQ, t)
    gw = GQA_GROUP * HEAD_DIM
    return pl.pallas_call(
        _attn_kernel,
        grid=(b, N_KV_HEADS, t // tq),
        in_specs=[
            pl.BlockSpec((1, tq, gw), lambda bi, g, qi: (bi, qi, g)),
            pl.BlockSpec((1, t, HEAD_DIM), lambda bi, g, qi: (bi, 0, g)),
            pl.BlockSpec((1, t, HEAD_DIM), lambda bi, g, qi: (bi, 0, g)),
        ],
        out_specs=pl.BlockSpec((1, tq, gw), lambda bi, g, qi: (bi, qi, g)),
        out_shape=jax.ShapeDtypeStruct((b, t, ATTN_WIDTH), BF16),
        compiler_params=_params(("parallel", "parallel", "arbitrary")),
        name="gqa_attention",
    )(q, k, v)


def _split_dot(tri, g):
    hi = g.astype(BF16)
    lo = (g - hi.astype(F32)).astype(BF16)
    return (jnp.dot(tri, hi, preferred_element_type=F32)
            + jnp.dot(tri, lo, preferred_element_type=F32))


def _gla_kernel(q_ref, k_ref, v_ref, sg_ref, gf_ref, gb_ref, gain_ref, o_ref, acc_ref,
                sf_ref, sb_ref):
    t = q_ref.shape[1]
    c = GLA_CHUNK
    n = t // c
    mid = c // 2
    ri = lax.broadcasted_iota(jnp.int32, (c, c), 0)
    ci = lax.broadcasted_iota(jnp.int32, (c, c), 1)
    lower = ri >= ci
    tri_lo = lower.astype(BF16)
    tri_up = (ri <= ci).astype(BF16)
    sf_ref[...] = jnp.zeros_like(sf_ref)
    sb_ref[...] = jnp.zeros_like(sb_ref)

    def chunk(idx, s_ref, g_ref, tri, first_row, mask):
        rows = pl.ds(pl.multiple_of(idx * c, c), c)
        q = q_ref[0, rows, :].astype(F32)
        k = k_ref[0, rows, :].astype(F32)
        v = v_ref[0, rows, :]
        b = _split_dot(tri, g_ref[0, rows, :])
        tot = b[c - 1:c, :] if first_row is False else b[0:1, :]
        ref = b[mid:mid + 1, :]
        qt = (q * jnp.exp(b - ref)).astype(BF16)
        kt = (k * jnp.exp(ref - b)).astype(BF16)
        a = lax.dot_general(qt, kt, NT_DIMS, preferred_element_type=F32)
        a = jnp.where(mask, a, 0.0).astype(BF16)
        st = s_ref[...]
        o = (jnp.dot(a, v, preferred_element_type=F32)
             + lax.dot_general((q * jnp.exp(b)).astype(BF16), st.astype(BF16), NT_DIMS,
                               preferred_element_type=F32))
        kd = (k * jnp.exp(tot - b)).astype(BF16)
        s_ref[...] = st * jnp.exp(tot) + lax.dot_general(v, kd, TN_DIMS,
                                                          preferred_element_type=F32)
        return rows, o

    def fwd(i):
        return chunk(i, sf_ref, gf_ref, tri_lo, False, lower)

    def bwd(i):
        return chunk(n - 1 - i, sb_ref, gb_ref, tri_up, True, ri <= ci)

    def finish(rows, o):
        y = _rms(o, gain_ref[...]) * sg_ref[0, rows, :].astype(F32)
        o_ref[0, rows, :] = y.astype(BF16)

    def first_half(i, carry):
        rf, of = fwd(i)
        rb, ob = bwd(i)
        acc_ref[rf, :] = of
        acc_ref[rb, :] = ob
        return carry

    def second_half(i, carry):
        rf, of = fwd(i)
        rb, ob = bwd(i)
        finish(rf, acc_ref[rf, :] + of)
        finish(rb, acc_ref[rb, :] + ob)
        return carry

    lax.fori_loop(0, n // 2, first_half, 0)
    lax.fori_loop(n // 2, n, second_half, 0)


def _gla(gq, gk, gv, sg, laf, lab, gain):
    b, t, _ = gq.shape
    kspec = pl.BlockSpec((1, t, GLA_DK), lambda bi, h: (bi, 0, h))
    vspec = pl.BlockSpec((1, t, GLA_DV), lambda bi, h: (bi, 0, h))
    return pl.pallas_call(
        _gla_kernel,
        grid=(b, GLA_HEADS),
        in_specs=[kspec, kspec, vspec, vspec, kspec, kspec,
                  pl.BlockSpec((1, GLA_DV), lambda bi, h: (0, 0))],
        out_specs=vspec,
        out_shape=jax.ShapeDtypeStruct((b, t, GLA_WIDTH), BF16),
        scratch_shapes=[
            pltpu.VMEM((t, GLA_DV), F32),
            pltpu.VMEM((GLA_DV, GLA_DK), F32),
            pltpu.VMEM((GLA_DV, GLA_DK), F32),
        ],
        compiler_params=_params(("parallel", "parallel")),
        name="gla_bidir",
    )(gq, gk, gv, sg, laf, lab, gain)


def _out_proj_kernel(x_ref, oa_ref, og_ref, ag_ref, wa_ref, wg_ref, fg_ref, h_ref, hn_ref):
    oa = _rms(oa_ref[...].astype(F32), ag_ref[...]).astype(BF16)
    h = (x_ref[...]
         + jnp.dot(oa, wa_ref[...], preferred_element_type=F32)
         + jnp.dot(og_ref[...], wg_ref[...], preferred_element_type=F32))
    h_ref[...] = h
    hn_ref[...] = _rms(h, fg_ref[...]).astype(BF16)


def _out_proj(x2, oa, og, ag, wa, wg, fg):
    n, d = x2.shape
    tm = PROJ_TM
    full = lambda i: (0, 0)
    row = lambda i: (i, 0)
    return pl.pallas_call(
        _out_proj_kernel,
        grid=(n // tm,),
        in_specs=[
            pl.BlockSpec((tm, d), row),
            pl.BlockSpec((tm, ATTN_WIDTH), row),
            pl.BlockSpec((tm, GLA_WIDTH), row),
            pl.BlockSpec((1, ATTN_WIDTH), full),
            pl.BlockSpec(wa.shape, full),
            pl.BlockSpec(wg.shape, full),
            pl.BlockSpec((1, d), full),
        ],
        out_specs=[pl.BlockSpec((tm, d), row), pl.BlockSpec((tm, d), row)],
        out_shape=[jax.ShapeDtypeStruct((n, d), F32), jax.ShapeDtypeStruct((n, d), BF16)],
        compiler_params=_params(("parallel",)),
        name="out_proj",
    )(x2, oa, og, ag, wa, wg, fg)


def _cmp_exchange(rows, i, j):
    a, b = rows[i], rows[j]
    if b is None:
        return
    if a is None:
        rows[i], rows[j] = b, None
        return
    rows[i], rows[j] = jnp.maximum(a, b), jnp.minimum(a, b)


def _bitonic_merge_desc(rows):
    n = len(rows)
    j = n // 2
    while j >= 1:
        for i in range(n):
            l = i ^ j
            if l > i:
                _cmp_exchange(rows, i, l)
        j //= 2


def _bitonic_sort_desc(rows):
    n = len(rows)
    k = 2
    while k <= n:
        j = k // 2
        while j >= 1:
            for i in range(n):
                l = i ^ j
                if l > i:
                    if (i & k) == 0:
                        _cmp_exchange(rows, i, l)
                    else:
                        _cmp_exchange(rows, l, i)
            j //= 2
        k *= 2


def _top_of_union(a, b):
    n = len(a)
    out = []
    for i in range(n):
        x, y = a[i], b[n - 1 - i]
        out.append(x if y is None else (y if x is None else jnp.maximum(x, y)))
    _bitonic_merge_desc(out)
    return out


def _top16_rows(s):
    l = s.shape[1]
    s3 = s.reshape(N_KEYS // SUBLANES, SUBLANES, l)
    rows = [s3[i] for i in range(N_KEYS // SUBLANES)]
    _bitonic_sort_desc(rows)
    for shift in (4, 2, 1):
        other = [pltpu.roll(r, shift, axis=0) for r in rows]
        rows = _top_of_union(rows, other)
    return rows


_CANDIDATES = [(i, j) for i in range(PEER_TOPK) for j in range(PEER_TOPK)
               if (i + 1) * (j + 1) <= PEER_TOPK]


def _route_kernel(hn_ref, wqt_ref, keys_ref, s2_ref, e2_ref, beta_ref, w1_ref, qt_ref):
    hh = pl.program_id(1)

    @pl.when(hh == 0)
    def _():
        qt_ref[...] = lax.dot_general(wqt_ref[...], hn_ref[...], NT_DIMS,
                                      preferred_element_type=F32).astype(BF16)

    base = pl.multiple_of(hh * PEER_QDIM, PEER_QDIM)
    q1 = qt_ref[pl.ds(base, PEER_HALF), :]
    q2 = qt_ref[pl.ds(base + PEER_HALF, PEER_HALF), :]
    s1 = jnp.dot(keys_ref[0, 0], q1, preferred_element_type=F32)
    s2 = jnp.dot(keys_ref[0, 1], q2, preferred_element_type=F32)
    a = _top16_rows(s1)
    b = _top16_rows(s2)
    cand = [a[i] + b[j] for (i, j) in _CANDIDATES]
    cand += [None] * (64 - len(cand))
    groups = []
    for gi in range(4):
        grp = cand[gi * 16:(gi + 1) * 16]
        _bitonic_sort_desc(grp)
        groups.append(grp)
    top = _top_of_union(_top_of_union(groups[0], groups[1]),
                        _top_of_union(groups[2], groups[3]))
    tau = top[PEER_TOPK - 1]
    z = jnp.zeros_like(tau)
    for tk in top:
        z = z + jnp.exp(tk - top[0])
    inv_z = 1.0 / z
    tb = s1.shape[1]
    full = lambda r: jnp.broadcast_to(r[0:1, :], (N_KEYS, tb))
    tau_f = full(tau)
    beta = jnp.full((N_KEYS, tb), jnp.inf, F32)
    for j in range(PEER_TOPK):
        bj = full(b[j])
        beta = jnp.where(s1 + bj >= tau_f, bj, beta)
    e2 = jnp.exp(s2 - full(b[0]))
    w1 = jnp.exp(s1 - full(a[0])) * full(inv_z)
    for lg in range(tb // LANES):
        lanes = slice(lg * LANES, (lg + 1) * LANES)
        s2_ref[lg, 0] = s2[:, lanes]
        e2_ref[lg, 0] = e2[:, lanes]
        beta_ref[lg, 0] = beta[:, lanes]
        w1_ref[lg, 0] = w1[:, lanes]


def _peer_route(hn, wqt, keys):
    n, d = hn.shape
    tb = ROUTE_TB
    tab = jax.ShapeDtypeStruct((n // LANES, PEER_HEADS, N_KEYS, LANES), F32)
    tspec = pl.BlockSpec((tb // LANES, 1, N_KEYS, LANES), lambda i, h: (i, h, 0, 0))
    return pl.pallas_call(
        _route_kernel,
        grid=(n // tb, PEER_HEADS),
        in_specs=[
            pl.BlockSpec((tb, d), lambda i, h: (i, 0)),
            pl.BlockSpec(wqt.shape, lambda i, h: (0, 0)),
            pl.BlockSpec((1, 2, N_KEYS, PEER_HALF), lambda i, h: (h, 0, 0, 0)),
        ],
        out_specs=[tspec, tspec, tspec, tspec],
        out_shape=[tab, tab, tab, tab],
        scratch_shapes=[pltpu.VMEM((PEER_HEADS * PEER_QDIM, tb), BF16)],
        compiler_params=_params(("parallel", "arbitrary")),
        name="peer_route",
    )(hn, wqt, keys)


def _gelu(x):
    return 0.5 * x * (1.0 + lax.erf(x * (2.0 ** -0.5)))


def _peer_kernel(hn_ref, h_ref, u0_ref, u_ref, vt_ref, s2_ref, e2_ref, beta_ref, w1_ref, fg_ref,
                 y_ref, acc_ref, hta_ref, htb_ref, w_ref):
    j = pl.program_id(1)
    nj = pl.num_programs(1)
    lane_groups, te, _ = hta_ref.shape
    rows_per_step = te // N_KEYS

    def score(u_tile_ref, dst_ref):
        ht = lax.dot_general(u_tile_ref[...], hn_ref[...], NT_DIMS,
                             preferred_element_type=F32)
        for lg in range(lane_groups):
            dst_ref[lg] = ht[:, lg * LANES:(lg + 1) * LANES]

    @pl.when(j == 0)
    def _():
        acc_ref[...] = jnp.zeros_like(acc_ref)
        score(u0_ref, hta_ref)

    def step(cur_ref, nxt_ref):
        for lg in range(lane_groups):
            for rr in range(rows_per_step):
                rows = slice(rr * N_KEYS, (rr + 1) * N_KEYS)
                gate = jnp.zeros((N_KEYS, LANES), F32)
                for hh in range(PEER_HEADS):
                    beta = beta_ref[lg, hh, rr:rr + 1, :]
                    w1 = w1_ref[lg, hh, rr:rr + 1, :]
                    gate = gate + jnp.where(s2_ref[lg, hh] >= beta, e2_ref[lg, hh], 0.0) * w1
                w_ref[rows, lg * LANES:(lg + 1) * LANES] = (
                    gate * _gelu(cur_ref[lg, rows, :])).astype(BF16)
        score(u_ref, nxt_ref)
        acc_ref[...] += jnp.dot(vt_ref[...], w_ref[...], preferred_element_type=F32)

    @pl.when(j % 2 == 0)
    def _():
        step(hta_ref, htb_ref)

    @pl.when(j % 2 == 1)
    def _():
        step(htb_ref, hta_ref)

    @pl.when(j == nj - 1)
    def _():
        h = h_ref[...] + acc_ref[...].T
        y_ref[...] = _rms(h, fg_ref[...])


def _peer_dense(hn, h, u, vt, s2, e2, beta, w1, fg):
    n, d = hn.shape
    e = u.shape[0]
    tb, te = PEER_TB, PEER_TE
    nj = e // te
    lgs = tb // LANES
    rows = te // N_KEYS
    once = pl.Buffered(1)
    tspec = pl.BlockSpec((lgs, PEER_HEADS, N_KEYS, LANES), lambda i, j: (i, 0, 0, 0),
                         pipeline_mode=once)
    rspec = pl.BlockSpec((lgs, PEER_HEADS, rows, LANES), lambda i, j: (i, 0, j, 0))
    return pl.pallas_call(
        _peer_kernel,
        grid=(n // tb, nj),
        in_specs=[
            pl.BlockSpec((tb, d), lambda i, j: (i, 0), pipeline_mode=once),
            pl.BlockSpec((tb, d), lambda i, j: (i, 0), pipeline_mode=once),
            pl.BlockSpec((te, d), lambda i, j: (0, 0), pipeline_mode=once),
            pl.BlockSpec((te, d), lambda i, j: (jnp.minimum(j + 1, nj - 1), 0)),
            pl.BlockSpec((d, te), lambda i, j: (0, j)),
            tspec, tspec, rspec, rspec,
            pl.BlockSpec((1, d), lambda i, j: (0, 0)),
        ],
        out_specs=pl.BlockSpec((tb, d), lambda i, j: (i, 0)),
        out_shape=jax.ShapeDtypeStruct((n, d), F32),
        scratch_shapes=[
            pltpu.VMEM((d, tb), F32),
            pltpu.VMEM((lgs, te, LANES), F32),
            pltpu.VMEM((lgs, te, LANES), F32),
            pltpu.VMEM((te, tb), BF16),
        ],
        compiler_params=_params(("parallel", "arbitrary")),
        name="peer_dense",
    )(hn, h, u, u, vt, s2, e2, beta, w1, fg)


def _rope_tables(t):
    n_rows = t // GRID_W
    row = jnp.repeat(jnp.arange(n_rows, dtype=F32), GRID_W)
    col = jnp.tile(jnp.arange(GRID_W, dtype=F32), n_rows)
    freqs = ROPE_THETA ** (-jnp.arange(0, AXIS_DIM, 2, dtype=F32) / AXIS_DIM)
    ang = jnp.concatenate([row[:, None] * freqs, col[:, None] * freqs], axis=-1)
    cos, sin = jnp.cos(ang), jnp.sin(ang)
    return jnp.concatenate([cos, cos], axis=-1), jnp.concatenate([-sin, sin], axis=-1)


def _prepare(norm_mix_g, w_in, q_norm_g, k_norm_g, w_gate_f2, b_gate_f, w_gate_b2, b_gate_b,
             gla_norm_g, attn_out_norm_g, w_out, norm_ffn_g, peer_w_q, peer_sub_keys, peer_u,
             peer_v, final_norm_g):
    perm = np.concatenate([np.arange(0, HEAD_DIM, 2), np.arange(1, HEAD_DIM, 2)])
    kw = GLA_HEADS * GLA_DK
    o_k = ATTN_WIDTH
    o_v = o_k + KV_WIDTH
    o_g = o_v + KV_WIDTH
    o_lr = o_g + 2 * kw + 2 * GLA_WIDTH
    wq = w_in[:, :o_k].reshape(D_MODEL, N_HEADS, HEAD_DIM)[:, :, perm].reshape(D_MODEL, o_k)
    wk = w_in[:, o_k:o_v].reshape(D_MODEL, N_KV_HEADS, HEAD_DIM)[:, :, perm].reshape(D_MODEL, KV_WIDTH)
    w_qkv = jnp.concatenate([wq, wk, w_in[:, o_v:o_g]], axis=1).astype(BF16)
    w_lr = jnp.pad(w_in[:, o_lr:], ((0, 0), (0, LANES - 2 * GATE_RANK)))
    w_gla = jnp.concatenate([w_in[:, o_g:o_lr], w_lr], axis=1).astype(BF16)
    wg = jnp.zeros((LANES, 2 * kw), F32)
    wg = wg.at[:GATE_RANK, :kw].set(w_gate_f2).at[GATE_RANK:2 * GATE_RANK, kw:].set(w_gate_b2)
    bg = jnp.concatenate([b_gate_f, b_gate_b])[None, :]
    return dict(
        norm_mix_g=norm_mix_g[None, :], w_qkv=w_qkv, w_gla=w_gla, wg=wg, bg=bg,
        qg=q_norm_g[perm][None, :], kg=k_norm_g[perm][None, :],
        gla_gain=gla_norm_g[None, :], attn_gain=attn_out_norm_g[None, :],
        wo_a=w_out[:ATTN_WIDTH].astype(BF16), wo_g=w_out[ATTN_WIDTH:].astype(BF16),
        ffn_g=norm_ffn_g[None, :], wqt=peer_w_q.T.astype(BF16),
        keys=peer_sub_keys.astype(BF16), u=peer_u.astype(BF16), vt=peer_v.T.astype(BF16),
        final_g=final_norm_g[None, :],
    )


def _layer(x, p):
    b, t, d = x.shape
    n = b * t
    x2 = x.reshape(n, d)
    cos, sin = _rope_tables(t)
    q, k, v = _qkv_proj(x2, p["norm_mix_g"], p["w_qkv"], p["qg"], p["kg"], cos, sin, t)
    gq, gk, gv, sg, laf, lab = _gla_proj(x2, p["norm_mix_g"], p["w_gla"], p["wg"], p["bg"])
    oa = _attention(q.reshape(b, t, -1), k.reshape(b, t, -1), v.reshape(b, t, -1))
    r3 = lambda a: a.reshape(b, t, -1)
    og = _gla(r3(gq), r3(gk), r3(gv), r3(sg), r3(laf), r3(lab), p["gla_gain"])
    h, hn = _out_proj(x2, oa.reshape(n, -1), og.reshape(n, -1), p["attn_gain"],
                      p["wo_a"], p["wo_g"], p["ffn_g"])
    s2, e2, beta, w1 = _peer_route(hn, p["wqt"], p["keys"])
    y = _peer_dense(hn, h, p["u"], p["vt"], s2, e2, beta, w1, p["final_g"])
    return y.reshape(b, t, d)


def kernel(x_prompt, x_sample, norm_mix_g, w_in, q_norm_g, k_norm_g, w_gate_f2, b_gate_f,
           w_gate_b2, b_gate_b, gla_norm_g, attn_out_norm_g, w_out, norm_ffn_g, peer_w_q,
           peer_sub_keys, peer_u, peer_v, final_norm_g):
    p = _prepare(norm_mix_g, w_in, q_norm_g, k_norm_g, w_gate_f2, b_gate_f, w_gate_b2,
                 b_gate_b, gla_norm_g, attn_out_norm_g, w_out, norm_ffn_g, peer_w_q,
                 peer_sub_keys, peer_u, peer_v, final_norm_g)
    return (_layer(x_prompt, p), _layer(x_sample, p))
```

```python
import functools
import math

import jax
import jax.numpy as jnp
import numpy as np
from jax import lax
from jax.experimental import pallas as pl
from jax.experimental.pallas import tpu as pltpu

F32 = jnp.float32
BF16 = jnp.bfloat16

D_MODEL = 2048
GRID_W = 64
HEAD_DIM = 128
N_HEADS = 8
N_KV_HEADS = 2
GQA_GROUP = N_HEADS // N_KV_HEADS
ATTN_WIDTH = N_HEADS * HEAD_DIM
KV_WIDTH = N_KV_HEADS * HEAD_DIM
ROPE_THETA = 10000.0
AXIS_DIM = HEAD_DIM // 2
GLA_HEADS = 4
GLA_DK = 128
GLA_DV = 256
GLA_WIDTH = GLA_HEADS * GLA_DV
GATE_RANK = 16
GATE_NORMALIZER = 16.0
PEER_HEADS = 8
N_KEYS = 128
PEER_TOPK = 16
PEER_HALF = 128
PEER_QDIM = 2 * PEER_HALF
EPS = 1e-6

LANES = 128
SUBLANES = 8
BF16_ROWS = 16
VMEM_LIMIT = 56 * 1024 * 1024

PROJ_TM = 512
ATTN_TQ = 256
GLA_CHUNK = 64
GLA_UNROLL = 4
ROUTE_TB = 256
PEER_TB = 512
PEER_TE = 1024

PEER_KEY_ORDER = np.arange(N_KEYS).reshape(2, N_KEYS // 2).T.reshape(-1)

NT_DIMS =(((1,), (1,)), ((), ()))
TN_DIMS = (((0,), (0,)), ((), ()))


def _params(sem, vmem=VMEM_LIMIT, flags=None):
    return pltpu.CompilerParams(dimension_semantics=sem, vmem_limit_bytes=vmem, flags=flags)


def _rms(x, g):
    return x * lax.rsqrt(jnp.mean(x * x, axis=-1, keepdims=True) + EPS) * g


def _qkv_kernel(x_ref, g_ref, w_ref, qg_ref, kg_ref, cos_ref, sin_ref, q_ref, k_ref, v_ref):
    xn = _rms(x_ref[...], g_ref[...]).astype(BF16)
    z = jnp.dot(xn, w_ref[...], preferred_element_type=F32)
    c = cos_ref[...]
    s = sin_ref[...]

    def norm_rope(hd, gain, scale):
        hn = _rms(hd, gain)
        rot = pltpu.roll(hn, HEAD_DIM // 2, axis=1)
        return ((hn * c + rot * s) * scale).astype(BF16)

    for hh in range(N_HEADS):
        sl = slice(hh * HEAD_DIM, (hh + 1) * HEAD_DIM)
        q_ref[:, sl] = norm_rope(z[:, sl], qg_ref[...], HEAD_DIM ** -0.5)
    for hh in range(N_KV_HEADS):
        sl = slice(hh * HEAD_DIM, (hh + 1) * HEAD_DIM)
        zsl = slice(ATTN_WIDTH + hh * HEAD_DIM, ATTN_WIDTH + (hh + 1) * HEAD_DIM)
        k_ref[:, sl] = norm_rope(z[:, zsl], kg_ref[...], 1.0)
    v_ref[...] = z[:, ATTN_WIDTH + KV_WIDTH:].astype(BF16)


def _qkv_proj(x2, g, w, qg, kg, cos, sin, seq):
    n, d = x2.shape
    tm = min(PROJ_TM, seq)
    nt = seq // tm
    wq = w.shape[1]
    full = lambda i: (0, 0)
    return pl.pallas_call(
        _qkv_kernel,
        grid=(n // tm,),
        in_specs=[
            pl.BlockSpec((tm, d), lambda i: (i, 0)),
            pl.BlockSpec((1, d), full),
            pl.BlockSpec((d, wq), full),
            pl.BlockSpec((1, HEAD_DIM), full),
            pl.BlockSpec((1, HEAD_DIM), full),
            pl.BlockSpec((tm, HEAD_DIM), lambda i: (i % nt, 0)),
            pl.BlockSpec((tm, HEAD_DIM), lambda i: (i % nt, 0)),
        ],
        out_specs=[
            pl.BlockSpec((tm, ATTN_WIDTH), lambda i: (i, 0)),
            pl.BlockSpec((tm, KV_WIDTH), lambda i: (i, 0)),
            pl.BlockSpec((tm, KV_WIDTH), lambda i: (i, 0)),
        ],
        out_shape=[
            jax.ShapeDtypeStruct((n, ATTN_WIDTH), BF16),
            jax.ShapeDtypeStruct((n, KV_WIDTH), BF16),
            jax.ShapeDtypeStruct((n, KV_WIDTH), BF16),
        ],
        compiler_params=_params(("parallel",)),
        name="qkv_proj",
    )(x2, g, w, qg, kg, cos, sin)


def _log_sigmoid(x):
    return jnp.minimum(x, 0.0) - jnp.log(1.0 + jnp.exp(-jnp.abs(x)))


def _gla_proj_kernel(x_ref, g_ref, w_ref, wg_ref, bg_ref,
                     gq_ref, gk_ref, gv_ref, sg_ref, laf_ref, lab_ref):
    xn = _rms(x_ref[...], g_ref[...]).astype(BF16)
    z = jnp.dot(xn, w_ref[...], preferred_element_type=F32)
    kw = GLA_HEADS * GLA_DK
    gq_ref[...] = (z[:, :kw] * (GLA_DK ** -0.5)).astype(BF16)
    gk_ref[...] = z[:, kw:2 * kw].astype(BF16)
    gv_ref[...] = z[:, 2 * kw:2 * kw + GLA_WIDTH].astype(BF16)
    gr = z[:, 2 * kw + GLA_WIDTH:2 * kw + 2 * GLA_WIDTH]
    sg_ref[...] = (gr * (1.0 / (1.0 + jnp.exp(-gr)))).astype(BF16)
    lr = z[:, 2 * kw + 2 * GLA_WIDTH:]
    pre = jnp.dot(lr, wg_ref[...], preferred_element_type=F32,
                  precision=lax.Precision.HIGHEST) + bg_ref[...]
    la = _log_sigmoid(pre) * (1.0 / GATE_NORMALIZER)
    laf_ref[...] = la[:, :kw]
    lab_ref[...] = la[:, kw:]


def _gla_proj(x2, g, w, wg, bg):
    n, d = x2.shape
    tm = PROJ_TM
    kw = GLA_HEADS * GLA_DK
    full = lambda i: (0, 0)
    row = lambda i: (i, 0)
    return pl.pallas_call(
        _gla_proj_kernel,
        grid=(n // tm,),
        in_specs=[
            pl.BlockSpec((tm, d), row),
            pl.BlockSpec((1, d), full),
            pl.BlockSpec(w.shape, full),
            pl.BlockSpec(wg.shape, full),
            pl.BlockSpec(bg.shape, full),
        ],
        out_specs=[
            pl.BlockSpec((tm, kw), row),
            pl.BlockSpec((tm, kw), row),
            pl.BlockSpec((tm, GLA_WIDTH), row),
            pl.BlockSpec((tm, GLA_WIDTH), row),
            pl.BlockSpec((tm, kw), row),
            pl.BlockSpec((tm, kw), row),
        ],
        out_shape=[
            jax.ShapeDtypeStruct((n, kw), BF16),
            jax.ShapeDtypeStruct((n, kw), BF16),
            jax.ShapeDtypeStruct((n, GLA_WIDTH), BF16),
            jax.ShapeDtypeStruct((n, GLA_WIDTH), BF16),
            jax.ShapeDtypeStruct((n, kw), F32),
            jax.ShapeDtypeStruct((n, kw), F32),
        ],
        compiler_params=_params(("parallel",)),
        name="gla_proj",
    )(x2, g, w, wg, bg)


def _attn_kernel(q_ref, k_ref, v_ref, o_ref):
    k = k_ref[0]
    v = v_ref[0]
    for hh in range(GQA_GROUP):
        sl = slice(hh * HEAD_DIM, (hh + 1) * HEAD_DIM)
        s = lax.dot_general(q_ref[0, :, sl], k, NT_DIMS, preferred_element_type=F32)
        m = jnp.max(s, axis=-1, keepdims=True)
        p = jnp.exp(s - m)
        l = jnp.sum(p, axis=-1, keepdims=True)
        o = jnp.dot(p.astype(BF16), v, preferred_element_type=F32)
        o_ref[0, :, sl] = (o / l).astype(BF16)


def _attention(q, k, v):
    b, t, _ = q.shape
    tq = min(ATTN_TQ, t)
    gw = GQA_GROUP * HEAD_DIM
    return pl.pallas_call(
        _attn_kernel,
        grid=(b, N_KV_HEADS, t // tq),
        in_specs=[
            pl.BlockSpec((1, tq, gw), lambda bi, g, qi: (bi, qi, g)),
            pl.BlockSpec((1, t, HEAD_DIM), lambda bi, g, qi: (bi, 0, g)),
            pl.BlockSpec((1, t, HEAD_DIM), lambda bi, g, qi: (bi, 0, g)),
        ],
        out_specs=pl.BlockSpec((1, tq, gw), lambda bi, g, qi: (bi, qi, g)),
        out_shape=jax.ShapeDtypeStruct((b, t, ATTN_WIDTH), BF16),
        compiler_params=_params(("parallel", "parallel", "arbitrary")),
        name="gqa_attention",
    )(q, k, v)


def _split_dot(tri, g):
    hi = g.astype(BF16)
    lo = (g - hi.astype(F32)).astype(BF16)
    return (jnp.dot(tri, hi, preferred_element_type=F32)
            + jnp.dot(tri, lo, preferred_element_type=F32))


def _gla_kernel(q_ref, k_ref, v_ref, sg_ref, gf_ref, gb_ref, gain_ref, o_ref, acc_ref,
                sf_ref, sb_ref):
    t = q_ref.shape[1]
    c = GLA_CHUNK
    n = t // c
    mid = c // 2
    ri = lax.broadcasted_iota(jnp.int32, (c, c), 0)
    ci = lax.broadcasted_iota(jnp.int32, (c, c), 1)
    lower = ri >= ci
    tri_lo = lower.astype(BF16)
    tri_up = (ri <= ci).astype(BF16)
    sf_ref[...] = jnp.zeros_like(sf_ref)
    sb_ref[...] = jnp.zeros_like(sb_ref)

    def chunk(idx, s_ref, g_ref, tri, first_row, mask):
        rows = pl.ds(pl.multiple_of(idx * c, c), c)
        q = q_ref[0, rows, :].astype(F32)
        k = k_ref[0, rows, :].astype(F32)
        v = v_ref[0, rows, :]
        b = _split_dot(tri, g_ref[0, rows, :])
        tot = b[c - 1:c, :] if first_row is False else b[0:1, :]
        ref = b[mid:mid + 1, :]
        qt = (q * jnp.exp(b - ref)).astype(BF16)
        kt = (k * jnp.exp(ref - b)).astype(BF16)
        a = lax.dot_general(qt, kt, NT_DIMS, preferred_element_type=F32)
        a = jnp.where(mask, a, 0.0).astype(BF16)
        st = s_ref[...]
        o = (jnp.dot(a, v, preferred_element_type=F32)
             + lax.dot_general((q * jnp.exp(b)).astype(BF16), st.astype(BF16), NT_DIMS,
                               preferred_element_type=F32))
        kd = (k * jnp.exp(tot - b)).astype(BF16)
        s_ref[...] = st * jnp.exp(tot) + lax.dot_general(v, kd, TN_DIMS,
                                                          preferred_element_type=F32)
        return rows, o

    def fwd(i):
        return chunk(i, sf_ref, gf_ref, tri_lo, False, lower)

    def bwd(i):
        return chunk(n - 1 - i, sb_ref, gb_ref, tri_up, True, ri <= ci)

    def finish(rows, o):
        y = _rms(o, gain_ref[...]) * sg_ref[0, rows, :].astype(F32)
        o_ref[0, rows, :] = y.astype(BF16)

    def first_half(i, carry):
        rf, of = fwd(i)
        rb, ob = bwd(i)
        acc_ref[rf, :] = of
        acc_ref[rb, :] = ob
        return carry

    def second_half(i, carry):
        rf, of = fwd(i)
        rb, ob = bwd(i)
        finish(rf, acc_ref[rf, :] + of)
        finish(rb, acc_ref[rb, :] + ob)
        return carry

    lax.fori_loop(0, n // 2, first_half, 0, unroll=GLA_UNROLL)
    lax.fori_loop(n // 2, n, second_half, 0, unroll=GLA_UNROLL)


def _gla(gq, gk, gv, sg, laf, lab, gain):
    b, t, _ = gq.shape
    kspec = pl.BlockSpec((1, t, GLA_DK), lambda bi, h: (bi, 0, h))
    vspec = pl.BlockSpec((1, t, GLA_DV), lambda bi, h: (bi, 0, h))
    return pl.pallas_call(
        _gla_kernel,
        grid=(b, GLA_HEADS),
        in_specs=[kspec, kspec, vspec, vspec, kspec, kspec,
                  pl.BlockSpec((1, GLA_DV), lambda bi, h: (0, 0))],
        out_specs=vspec,
        out_shape=jax.ShapeDtypeStruct((b, t, GLA_WIDTH), BF16),
        scratch_shapes=[
            pltpu.VMEM((t, GLA_DV), F32),
            pltpu.VMEM((GLA_DV, GLA_DK), F32),
            pltpu.VMEM((GLA_DV, GLA_DK), F32),
        ],
        compiler_params=_params(("parallel", "parallel")),
        name="gla_bidir",
    )(gq, gk, gv, sg, laf, lab, gain)


def _out_proj_kernel(x_ref, oa_ref, og_ref, ag_ref, wa_ref, wg_ref, fg_ref, h_ref, hn_ref):
    oa = _rms(oa_ref[...].astype(F32), ag_ref[...]).astype(BF16)
    h = (x_ref[...]
         + jnp.dot(oa, wa_ref[...], preferred_element_type=F32)
         + jnp.dot(og_ref[...], wg_ref[...], preferred_element_type=F32))
    h_ref[...] = h
    hn_ref[...] = _rms(h, fg_ref[...]).astype(BF16)


def _out_proj(x2, oa, og, ag, wa, wg, fg):
    n, d = x2.shape
    tm = PROJ_TM
    full = lambda i: (0, 0)
    row = lambda i: (i, 0)
    return pl.pallas_call(
        _out_proj_kernel,
        grid=(n // tm,),
        in_specs=[
            pl.BlockSpec((tm, d), row),
            pl.BlockSpec((tm, ATTN_WIDTH), row),
            pl.BlockSpec((tm, GLA_WIDTH), row),
            pl.BlockSpec((1, ATTN_WIDTH), full),
            pl.BlockSpec(wa.shape, full),
            pl.BlockSpec(wg.shape, full),
            pl.BlockSpec((1, d), full),
        ],
        out_specs=[pl.BlockSpec((tm, d), row), pl.BlockSpec((tm, d), row)],
        out_shape=[jax.ShapeDtypeStruct((n, d), F32), jax.ShapeDtypeStruct((n, d), BF16)],
        compiler_params=_params(("parallel",)),
        name="out_proj",
    )(x2, oa, og, ag, wa, wg, fg)


def _cmp_exchange(rows, i, j):
    a, b = rows[i], rows[j]
    if b is None:
        return
    if a is None:
        rows[i], rows[j] = b, None
        return
    rows[i], rows[j] = jnp.maximum(a, b), jnp.minimum(a, b)


def _bitonic_merge_desc(rows):
    n = len(rows)
    j = n // 2
    while j >= 1:
        for i in range(n):
            l = i ^ j
            if l > i:
                _cmp_exchange(rows, i, l)
        j //= 2


def _bitonic_sort_desc(rows):
    n = len(rows)
    k = 2
    while k <= n:
        j = k // 2
        while j >= 1:
            for i in range(n):
                l = i ^ j
                if l > i:
                    if (i & k) == 0:
                        _cmp_exchange(rows, i, l)
                    else:
                        _cmp_exchange(rows, l, i)
            j //= 2
        k *= 2


def _top_of_union(a, b):
    n = len(a)
    out = []
    for i in range(n):
        x, y = a[i], b[n - 1 - i]
        out.append(x if y is None else (y if x is None else jnp.maximum(x, y)))
    _bitonic_merge_desc(out)
    return out


def _top16_rows(s):
    l = s.shape[1]
    s3 = s.reshape(N_KEYS // SUBLANES, SUBLANES, l)
    rows = [s3[i] for i in range(N_KEYS // SUBLANES)]
    _bitonic_sort_desc(rows)
    for shift in (4, 2, 1):
        other = [pltpu.roll(r, shift, axis=0) for r in rows]
        rows = _top_of_union(rows, other)
    return rows


_CANDIDATES = [(i, j) for i in range(PEER_TOPK) for j in range(PEER_TOPK)
               if (i + 1) * (j + 1) <= PEER_TOPK]


def _bf16_bits(x):
    bits = lax.bitcast_convert_type(x, jnp.uint32)
    return (bits + jnp.uint32(0x7FFF) + ((bits >> 16) & jnp.uint32(1))) >> 16


def _bf16_pair_words(x):
    hi = _bf16_bits(x)
    return lax.bitcast_convert_type((hi << 16) | hi, jnp.int32)


def _bf16_packed_rows(x):
    hi = _bf16_bits(x)
    half = x.shape[0] // 2
    return lax.bitcast_convert_type(hi[:half] | (hi[half:] << 16), jnp.int32)


def _route_kernel(hn_ref, wqt_ref, keys_ref, rank_ref, e2_ref, cnt_ref, w1_ref, qt_ref):
    hh = pl.program_id(1)

    @pl.when(hh == 0)
    def _():
        qt_ref[...] = lax.dot_general(wqt_ref[...], hn_ref[...], NT_DIMS,
                                      preferred_element_type=F32).astype(BF16)

    base = pl.multiple_of(hh * PEER_QDIM, PEER_QDIM)
    q1 = qt_ref[pl.ds(base, PEER_HALF), :]
    q2 = qt_ref[pl.ds(base + PEER_HALF, PEER_HALF), :]
    s1 = jnp.dot(keys_ref[0, 0], q1, preferred_element_type=F32)
    s2 = jnp.dot(keys_ref[0, 1], q2, preferred_element_type=F32)
    a = _top16_rows(s1)
    b = _top16_rows(s2)
    cand = [a[i] + b[j] for (i, j) in _CANDIDATES]
    cand += [None] * (64 - len(cand))
    groups = []
    for gi in range(4):
        grp = cand[gi * 16:(gi + 1) * 16]
        _bitonic_sort_desc(grp)
        groups.append(grp)
    top = _top_of_union(_top_of_union(groups[0], groups[1]),
                        _top_of_union(groups[2], groups[3]))
    tau = top[PEER_TOPK - 1]
    z = jnp.zeros_like(tau)
    for tk in top:
        z = z + jnp.exp(tk - top[0])
    inv_z = 1.0 / z
    tb = s1.shape[1]
    full = lambda r: jnp.broadcast_to(r[0:1, :], (N_KEYS, tb))
    tau_f = full(tau)
    cnt = jnp.zeros((N_KEYS, tb), F32)
    rank = jnp.zeros((N_KEYS, tb), F32)
    for j in range(PEER_TOPK):
        bj = full(b[j])
        cnt = cnt + jnp.where(s1 + bj >= tau_f, 1.0, 0.0)
        rank = rank + jnp.where(bj > s2, 1.0, 0.0)
    e2 = jnp.exp(s2 - full(b[0]))
    w1 = jnp.exp(s1 - full(a[0])) * full(inv_z)
    rank_w = _bf16_packed_rows(rank)
    e2_w = _bf16_packed_rows(e2)
    cnt_w = _bf16_pair_words(cnt)
    w1_w = _bf16_pair_words(w1)
    for lg in range(tb // LANES):
        lanes = slice(lg * LANES, (lg + 1) * LANES)
        rank_ref[lg, 0] = rank_w[:, lanes]
        e2_ref[lg, 0] = e2_w[:, lanes]
        cnt_ref[lg, 0] = cnt_w[:, lanes]
        w1_ref[lg, 0] = w1_w[:, lanes]


def _peer_route(hn, wqt, keys):
    n, d = hn.shape
    tb = ROUTE_TB
    lgs = tb // LANES
    tab = jax.ShapeDtypeStruct((n // LANES, PEER_HEADS, N_KEYS, LANES), jnp.int32)
    tab16 = jax.ShapeDtypeStruct((n // LANES, PEER_HEADS, N_KEYS // 2, LANES), jnp.int32)
    tspec = pl.BlockSpec((lgs, 1, N_KEYS, LANES), lambda i, h: (i, h, 0, 0))
    tspec16 = pl.BlockSpec((lgs, 1, N_KEYS // 2, LANES), lambda i, h: (i, h, 0, 0))
    return pl.pallas_call(
        _route_kernel,
        grid=(n // tb, PEER_HEADS),
        in_specs=[
            pl.BlockSpec((tb, d), lambda i, h: (i, 0)),
            pl.BlockSpec(wqt.shape, lambda i, h: (0, 0)),
            pl.BlockSpec((1, 2, N_KEYS, PEER_HALF), lambda i, h: (h, 0, 0, 0)),
        ],
        out_specs=[tspec16, tspec16, tspec, tspec],
        out_shape=[tab16, tab16, tab, tab],
        scratch_shapes=[pltpu.VMEM((PEER_HEADS * PEER_QDIM, tb), BF16)],
        compiler_params=_params(("parallel", "arbitrary")),
        name="peer_route",
    )(hn, wqt, keys)


def _gelu(x):
    return 0.5 * x * (1.0 + lax.erf(x * (2.0 ** -0.5)))


def _peer_kernel(hn_ref, h_ref, u0_ref, u_ref, vt_ref, rank_ref, e2_ref, cnt_ref, w1_ref, fg_ref,
                 y_ref, acc_ref, hta_ref, htb_ref, w_ref):
    j = pl.program_id(1)
    nj = pl.num_programs(1)
    lane_groups, te, _ = hta_ref.shape
    rows_per_step = te // N_KEYS

    def score(u_tile_ref, dst_ref):
        ht = lax.dot_general(u_tile_ref[...], hn_ref[...], NT_DIMS,
                             preferred_element_type=F32)
        for lg in range(lane_groups):
            dst_ref[lg] = ht[:, lg * LANES:(lg + 1) * LANES]

    @pl.when(j == 0)
    def _():
        acc_ref[...] = jnp.zeros_like(acc_ref)
        score(u0_ref, hta_ref)

    def step(cur_ref, nxt_ref):
        zero = jnp.zeros((BF16_ROWS, LANES), BF16)

        def row_tile(ref, lg, hh, rr):
            words = jnp.broadcast_to(ref[lg, hh, rr:rr + 1, :], (SUBLANES, LANES))
            return pltpu.bitcast(words, BF16)

        for lg in range(lane_groups):
            lanes = slice(lg * LANES, (lg + 1) * LANES)
            for rr in range(rows_per_step):
                cnt = [row_tile(cnt_ref, lg, hh, rr) for hh in range(PEER_HEADS)]
                w1 = [row_tile(w1_ref, lg, hh, rr) for hh in range(PEER_HEADS)]
                for kk in range(N_KEYS // BF16_ROWS):
                    words = slice(kk * SUBLANES, (kk + 1) * SUBLANES)
                    rows = slice(rr * N_KEYS + kk * BF16_ROWS, rr * N_KEYS + (kk + 1) * BF16_ROWS)
                    gate = zero
                    for hh in range(PEER_HEADS):
                        rank = pltpu.bitcast(rank_ref[lg, hh, words, :], BF16)
                        e2 = pltpu.bitcast(e2_ref[lg, hh, words, :], BF16)
                        gate = gate + jnp.where(rank < cnt[hh], e2, zero) * w1[hh]
                    w_ref[rows, lanes] = gate * _gelu(cur_ref[lg, rows, :]).astype(BF16)
        score(u_ref, nxt_ref)
        acc_ref[...] += jnp.dot(vt_ref[...], w_ref[...], preferred_element_type=F32)

    @pl.when(j % 2 == 0)
    def _():
        step(hta_ref, htb_ref)

    @pl.when(j % 2 == 1)
    def _():
        step(htb_ref, hta_ref)

    @pl.when(j == nj - 1)
    def _():
        h = h_ref[...] + acc_ref[...].T
        y_ref[...] = _rms(h, fg_ref[...])


def _peer_dense(hn, h, u, vt, rank, e2, cnt, w1, fg):
    n, d = hn.shape
    e = u.shape[0]
    tb, te = PEER_TB, PEER_TE
    nj = e // te
    lgs = tb // LANES
    rows = te // N_KEYS
    once = pl.Buffered(1)
    tspec = pl.BlockSpec((lgs, PEER_HEADS, N_KEYS // 2, LANES), lambda i, j: (i, 0, 0, 0),
                         pipeline_mode=once)
    rspec = pl.BlockSpec((lgs, PEER_HEADS, rows, LANES), lambda i, j: (i, 0, j, 0))
    return pl.pallas_call(
        _peer_kernel,
        grid=(n // tb, nj),
        in_specs=[
            pl.BlockSpec((tb, d), lambda i, j: (i, 0), pipeline_mode=once),
            pl.BlockSpec((tb, d), lambda i, j: (i, 0), pipeline_mode=once),
            pl.BlockSpec((te, d), lambda i, j: (0, 0), pipeline_mode=once),
            pl.BlockSpec((te, d), lambda i, j: (jnp.minimum(j + 1, nj - 1), 0)),
            pl.BlockSpec((None, d, te), lambda i, j: (j, 0, 0)),
            tspec, tspec, rspec, rspec,
            pl.BlockSpec((1, d), lambda i, j: (0, 0)),
        ],
        out_specs=pl.BlockSpec((tb, d), lambda i, j: (i, 0)),
        out_shape=jax.ShapeDtypeStruct((n, d), F32),
        scratch_shapes=[
            pltpu.VMEM((d, tb), F32),
            pltpu.VMEM((lgs, te, LANES), F32),
            pltpu.VMEM((lgs, te, LANES), F32),
            pltpu.VMEM((te, tb), BF16),
        ],
        compiler_params=_params(("parallel", "arbitrary")),
        name="peer_dense",
    )(hn, h, u, u, vt, rank, e2, cnt, w1, fg)


def _rope_tables(t):
    n_rows = t // GRID_W
    row = jnp.repeat(jnp.arange(n_rows, dtype=F32), GRID_W)
    col = jnp.tile(jnp.arange(GRID_W, dtype=F32), n_rows)
    freqs = ROPE_THETA ** (-jnp.arange(0, AXIS_DIM, 2, dtype=F32) / AXIS_DIM)
    ang = jnp.concatenate([row[:, None] * freqs, col[:, None] * freqs], axis=-1)
    cos, sin = jnp.cos(ang), jnp.sin(ang)
    return jnp.concatenate([cos, cos], axis=-1), jnp.concatenate([-sin, sin], axis=-1)


def _prepare(norm_mix_g, w_in, q_norm_g, k_norm_g, w_gate_f2, b_gate_f, w_gate_b2, b_gate_b,
             gla_norm_g, attn_out_norm_g, w_out, norm_ffn_g, peer_w_q, peer_sub_keys, peer_u,
             peer_v, final_norm_g):
    perm = np.concatenate([np.arange(0, HEAD_DIM, 2), np.arange(1, HEAD_DIM, 2)])
    kw = GLA_HEADS * GLA_DK
    o_k = ATTN_WIDTH
    o_v = o_k + KV_WIDTH
    o_g = o_v + KV_WIDTH
    o_lr = o_g + 2 * kw + 2 * GLA_WIDTH
    wq = w_in[:, :o_k].reshape(D_MODEL, N_HEADS, HEAD_DIM)[:, :, perm].reshape(D_MODEL, o_k)
    wk = w_in[:, o_k:o_v].reshape(D_MODEL, N_KV_HEADS, HEAD_DIM)[:, :, perm].reshape(D_MODEL, KV_WIDTH)
    w_qkv = jnp.concatenate([wq, wk, w_in[:, o_v:o_g]], axis=1).astype(BF16)
    w_lr = jnp.pad(w_in[:, o_lr:], ((0, 0), (0, LANES - 2 * GATE_RANK)))
    w_gla = jnp.concatenate([w_in[:, o_g:o_lr], w_lr], axis=1).astype(BF16)
    wg = jnp.zeros((LANES, 2 * kw), F32)
    wg = wg.at[:GATE_RANK, :kw].set(w_gate_f2).at[GATE_RANK:2 * GATE_RANK, kw:].set(w_gate_b2)
    bg = jnp.concatenate([b_gate_f, b_gate_b])[None, :]

    def by_key2(table):
        return table.reshape(N_KEYS, N_KEYS, D_MODEL)[:, PEER_KEY_ORDER, :].reshape(-1, D_MODEL)

    return dict(
        norm_mix_g=norm_mix_g[None, :], w_qkv=w_qkv, w_gla=w_gla, wg=wg, bg=bg,
        qg=q_norm_g[perm][None, :], kg=k_norm_g[perm][None, :],
        gla_gain=gla_norm_g[None, :], attn_gain=attn_out_norm_g[None, :],
        wo_a=w_out[:ATTN_WIDTH].astype(BF16), wo_g=w_out[ATTN_WIDTH:].astype(BF16),
        ffn_g=norm_ffn_g[None, :], wqt=peer_w_q.T.astype(BF16),
        keys=peer_sub_keys.astype(BF16), u=by_key2(peer_u).astype(BF16),
        vt=by_key2(peer_v).astype(BF16).reshape(-1, PEER_TE, D_MODEL).transpose(0, 2, 1),
        final_g=final_norm_g[None, :],
    )


def _layer(x, p):
    b, t, d = x.shape
    n = b * t
    x2 = x.reshape(n, d)
    cos, sin = _rope_tables(t)
    q, k, v = _qkv_proj(x2, p["norm_mix_g"], p["w_qkv"], p["qg"], p["kg"], cos, sin, t)
    gq, gk, gv, sg, laf, lab = _gla_proj(x2, p["norm_mix_g"], p["w_gla"], p["wg"], p["bg"])
    oa = _attention(q.reshape(b, t, -1), k.reshape(b, t, -1), v.reshape(b, t, -1))
    r3 = lambda a: a.reshape(b, t, -1)
    og = _gla(r3(gq), r3(gk), r3(gv), r3(sg), r3(laf), r3(lab), p["gla_gain"])
    h, hn = _out_proj(x2, oa.reshape(n, -1), og.reshape(n, -1), p["attn_gain"],
                      p["wo_a"], p["wo_g"], p["ffn_g"])
    rank, e2, cnt, w1 = _peer_route(hn, p["wqt"], p["keys"])
    y = _peer_dense(hn, h, p["u"], p["vt"], rank, e2, cnt, w1, p["final_g"])
    return y.reshape(b, t, d)


def kernel(x_prompt, x_sample, norm_mix_g, w_in, q_norm_g, k_norm_g, w_gate_f2, b_gate_f,
           w_gate_b2, b_gate_b, gla_norm_g, attn_out_norm_g, w_out, norm_ffn_g, peer_w_q,
           peer_sub_keys, peer_u, peer_v, final_norm_g):
    p = _prepare(norm_mix_g, w_in, q_norm_g, k_norm_g, w_gate_f2, b_gate_f, w_gate_b2,
                 b_gate_b, gla_norm_g, attn_out_norm_g, w_out, norm_ffn_g, peer_w_q,
                 peer_sub_keys, peer_u, peer_v, final_norm_g)
    return (_layer(x_prompt, p), _layer(x_sample, p))
```

```python
import functools
import math

import jax
import jax.numpy as jnp
import numpy as np
from jax import lax
from jax.experimental import pallas as pl
from jax.experimental.pallas import tpu as pltpu

F32 = jnp.float32
BF16 = jnp.bfloat16

D_MODEL = 2048
GRID_W = 64
HEAD_DIM = 128
N_HEADS = 8
N_KV_HEADS = 2
GQA_GROUP = N_HEADS // N_KV_HEADS
ATTN_WIDTH = N_HEADS * HEAD_DIM
KV_WIDTH = N_KV_HEADS * HEAD_DIM
ROPE_THETA = 10000.0
AXIS_DIM = HEAD_DIM // 2
GLA_HEADS = 4
GLA_DK = 128
GLA_DV = 256
GLA_WIDTH = GLA_HEADS * GLA_DV
GATE_RANK = 16
GATE_NORMALIZER = 16.0
PEER_HEADS = 8
N_KEYS = 128
PEER_TOPK = 16
PEER_HALF = 128
PEER_QDIM = 2 * PEER_HALF
EPS = 1e-6

LANES = 128
SUBLANES = 8
BF16_ROWS = 16
VMEM_LIMIT = 56 * 1024 * 1024

PROJ_TM = 512
ATTN_TQ = 256
GLA_CHUNK = 64
GLA_UNROLL = 8
ROUTE_TB = 512
PEER_TB = 512
PEER_TE = 1024

PEER_KEY2_ROWS = np.concatenate([np.arange(0, N_KEYS, 2), np.arange(1, N_KEYS, 2)])

NT_DIMS = (((1,), (1,)), ((), ()))
TN_DIMS = (((0,), (0,)), ((), ()))


def _params(sem, vmem=VMEM_LIMIT, flags=None):
    return pltpu.CompilerParams(dimension_semantics=sem, vmem_limit_bytes=vmem, flags=flags)


def _rms(x, g):
    return x * lax.rsqrt(jnp.mean(x * x, axis=-1, keepdims=True) + EPS) * g


def _qkv_kernel(x_ref, g_ref, w_ref, qg_ref, kg_ref, cos_ref, sin_ref, q_ref, k_ref, v_ref):
    xn = _rms(x_ref[...], g_ref[...]).astype(BF16)
    z = jnp.dot(xn, w_ref[...], preferred_element_type=F32)
    c = cos_ref[...]
    s = sin_ref[...]

    def norm_rope(hd, gain, scale):
        hn = _rms(hd, gain)
        rot = pltpu.roll(hn, HEAD_DIM // 2, axis=1)
        return ((hn * c + rot * s) * scale).astype(BF16)

    for hh in range(N_HEADS):
        sl = slice(hh * HEAD_DIM, (hh + 1) * HEAD_DIM)
        q_ref[:, sl] = norm_rope(z[:, sl], qg_ref[...], HEAD_DIM ** -0.5)
    for hh in range(N_KV_HEADS):
        sl = slice(hh * HEAD_DIM, (hh + 1) * HEAD_DIM)
        zsl = slice(ATTN_WIDTH + hh * HEAD_DIM, ATTN_WIDTH + (hh + 1) * HEAD_DIM)
        k_ref[:, sl] = norm_rope(z[:, zsl], kg_ref[...], 1.0)
    v_ref[...] = z[:, ATTN_WIDTH + KV_WIDTH:].astype(BF16)


def _qkv_proj(x2, g, w, qg, kg, cos, sin, seq):
    n, d = x2.shape
    tm = min(PROJ_TM, seq)
    nt = seq // tm
    wq = w.shape[1]
    full = lambda i: (0, 0)
    return pl.pallas_call(
        _qkv_kernel,
        grid=(n // tm,),
        in_specs=[
            pl.BlockSpec((tm, d), lambda i: (i, 0)),
            pl.BlockSpec((1, d), full),
            pl.BlockSpec((d, wq), full),
            pl.BlockSpec((1, HEAD_DIM), full),
            pl.BlockSpec((1, HEAD_DIM), full),
            pl.BlockSpec((tm, HEAD_DIM), lambda i: (i % nt, 0)),
            pl.BlockSpec((tm, HEAD_DIM), lambda i: (i % nt, 0)),
        ],
        out_specs=[
            pl.BlockSpec((tm, ATTN_WIDTH), lambda i: (i, 0)),
            pl.BlockSpec((tm, KV_WIDTH), lambda i: (i, 0)),
            pl.BlockSpec((tm, KV_WIDTH), lambda i: (i, 0)),
        ],
        out_shape=[
            jax.ShapeDtypeStruct((n, ATTN_WIDTH), BF16),
            jax.ShapeDtypeStruct((n, KV_WIDTH), BF16),
            jax.ShapeDtypeStruct((n, KV_WIDTH), BF16),
        ],
        compiler_params=_params(("parallel",)),
        name="qkv_proj",
    )(x2, g, w, qg, kg, cos, sin)


def _log_sigmoid(x):
    return jnp.minimum(x, 0.0) - jnp.log(1.0 + jnp.exp(-jnp.abs(x)))


def _gla_proj_kernel(x_ref, g_ref, w_ref, wg_ref, bg_ref,
                     gq_ref, gk_ref, gv_ref, sg_ref, laf_ref, lab_ref):
    xn = _rms(x_ref[...], g_ref[...]).astype(BF16)
    z = jnp.dot(xn, w_ref[...], preferred_element_type=F32)
    kw = GLA_HEADS * GLA_DK
    gq_ref[...] = (z[:, :kw] * (GLA_DK ** -0.5)).astype(BF16)
    gk_ref[...] = z[:, kw:2 * kw].astype(BF16)
    gv_ref[...] = z[:, 2 * kw:2 * kw + GLA_WIDTH].astype(BF16)
    gr = z[:, 2 * kw + GLA_WIDTH:2 * kw + 2 * GLA_WIDTH]
    sg_ref[...] = (gr * (1.0 / (1.0 + jnp.exp(-gr)))).astype(BF16)
    lr = z[:, 2 * kw + 2 * GLA_WIDTH:]
    pre = jnp.dot(lr, wg_ref[...], preferred_element_type=F32,
                  precision=lax.Precision.HIGHEST) + bg_ref[...]
    la = _log_sigmoid(pre) * (1.0 / GATE_NORMALIZER)
    laf_ref[...] = la[:, :kw]
    lab_ref[...] = la[:, kw:]


def _gla_proj(x2, g, w, wg, bg):
    n, d = x2.shape
    tm = PROJ_TM
    kw = GLA_HEADS * GLA_DK
    full = lambda i: (0, 0)
    row = lambda i: (i, 0)
    return pl.pallas_call(
        _gla_proj_kernel,
        grid=(n // tm,),
        in_specs=[
            pl.BlockSpec((tm, d), row),
            pl.BlockSpec((1, d), full),
            pl.BlockSpec(w.shape, full),
            pl.BlockSpec(wg.shape, full),
            pl.BlockSpec(bg.shape, full),
        ],
        out_specs=[
            pl.BlockSpec((tm, kw), row),
            pl.BlockSpec((tm, kw), row),
            pl.BlockSpec((tm, GLA_WIDTH), row),
            pl.BlockSpec((tm, GLA_WIDTH), row),
            pl.BlockSpec((tm, kw), row),
            pl.BlockSpec((tm, kw), row),
        ],
        out_shape=[
            jax.ShapeDtypeStruct((n, kw), BF16),
            jax.ShapeDtypeStruct((n, kw), BF16),
            jax.ShapeDtypeStruct((n, GLA_WIDTH), BF16),
            jax.ShapeDtypeStruct((n, GLA_WIDTH), BF16),
            jax.ShapeDtypeStruct((n, kw), F32),
            jax.ShapeDtypeStruct((n, kw), F32),
        ],
        compiler_params=_params(("parallel",)),
        name="gla_proj",
    )(x2, g, w, wg, bg)


def _attn_kernel(q_ref, k_ref, v_ref, o_ref):
    k = k_ref[0]
    v = v_ref[0]
    for hh in range(GQA_GROUP):
        sl = slice(hh * HEAD_DIM, (hh + 1) * HEAD_DIM)
        s = lax.dot_general(q_ref[0, :, sl], k, NT_DIMS, preferred_element_type=F32)
        m = jnp.max(s, axis=-1, keepdims=True)
        p = jnp.exp(s - m)
        l = jnp.sum(p, axis=-1, keepdims=True)
        o = jnp.dot(p.astype(BF16), v, preferred_element_type=F32)
        o_ref[0, :, sl] = (o / l).astype(BF16)


def _attention(q, k, v):
    b, t, _ = q.shape
    tq = min(ATTN_TQ, t)
    gw = GQA_GROUP * HEAD_DIM
    return pl.pallas_call(
        _attn_kernel,
        grid=(b, N_KV_HEADS, t // tq),
        in_specs=[
            pl.BlockSpec((1, tq, gw), lambda bi, g, qi: (bi, qi, g)),
            pl.BlockSpec((1, t, HEAD_DIM), lambda bi, g, qi: (bi, 0, g)),
            pl.BlockSpec((1, t, HEAD_DIM), lambda bi, g, qi: (bi, 0, g)),
        ],
        out_specs=pl.BlockSpec((1, tq, gw), lambda bi, g, qi: (bi, qi, g)),
        out_shape=jax.ShapeDtypeStruct((b, t, ATTN_WIDTH), BF16),
        compiler_params=_params(("parallel", "parallel", "arbitrary")),
        name="gqa_attention",
    )(q, k, v)


def _split_dot(tri, g):
    hi = g.astype(BF16)
    lo = (g - hi.astype(F32)).astype(BF16)
    return (jnp.dot(tri, hi, preferred_element_type=F32)
            + jnp.dot(tri, lo, preferred_element_type=F32))


def _gla_kernel(q_ref, k_ref, v_ref, sg_ref, gf_ref, gb_ref, gain_ref, o_ref, acc_ref,
                sf_ref, sb_ref):
    t = q_ref.shape[1]
    c = GLA_CHUNK
    n = t // c
    mid = c // 2
    ri = lax.broadcasted_iota(jnp.int32, (c, c), 0)
    ci = lax.broadcasted_iota(jnp.int32, (c, c), 1)
    lower = ri >= ci
    tri_lo = lower.astype(BF16)
    tri_up = (ri <= ci).astype(BF16)
    sf_ref[...] = jnp.zeros_like(sf_ref)
    sb_ref[...] = jnp.zeros_like(sb_ref)

    def chunk(idx, s_ref, g_ref, tri, first_row, mask):
        rows = pl.ds(pl.multiple_of(idx * c, c), c)
        q = q_ref[0, rows, :].astype(F32)
        k = k_ref[0, rows, :].astype(F32)
        v = v_ref[0, rows, :]
        b = _split_dot(tri, g_ref[0, rows, :])
        tot = b[c - 1:c, :] if first_row is False else b[0:1, :]
        ref = b[mid:mid + 1, :]
        qt = (q * jnp.exp(b - ref)).astype(BF16)
        kt = (k * jnp.exp(ref - b)).astype(BF16)
        a = lax.dot_general(qt, kt, NT_DIMS, preferred_element_type=F32)
        a = jnp.where(mask, a, 0.0).astype(BF16)
        st = s_ref[...]
        o = (jnp.dot(a, v, preferred_element_type=F32)
             + lax.dot_general((q * jnp.exp(b)).astype(BF16), st.astype(BF16), NT_DIMS,
                               preferred_element_type=F32))
        kd = (k * jnp.exp(tot - b)).astype(BF16)
        s_ref[...] = st * jnp.exp(tot) + lax.dot_general(v, kd, TN_DIMS,
                                                          preferred_element_type=F32)
        return rows, o

    def fwd(i):
        return chunk(i, sf_ref, gf_ref, tri_lo, False, lower)

    def bwd(i):
        return chunk(n - 1 - i, sb_ref, gb_ref, tri_up, True, ri <= ci)

    def finish(rows, o):
        y = _rms(o, gain_ref[...]) * sg_ref[0, rows, :].astype(F32)
        o_ref[0, rows, :] = y.astype(BF16)

    def first_half(i, carry):
        rf, of = fwd(i)
        rb, ob = bwd(i)
        acc_ref[rf, :] = of
        acc_ref[rb, :] = ob
        return carry

    def second_half(i, carry):
        rf, of = fwd(i)
        rb, ob = bwd(i)
        finish(rf, acc_ref[rf, :] + of)
        finish(rb, acc_ref[rb, :] + ob)
        return carry

    lax.fori_loop(0, n // 2, first_half, 0, unroll=GLA_UNROLL)
    lax.fori_loop(n // 2, n, second_half, 0, unroll=GLA_UNROLL)


def _gla(gq, gk, gv, sg, laf, lab, gain):
    b, t, _ = gq.shape
    kspec = pl.BlockSpec((1, t, GLA_DK), lambda bi, h: (bi, 0, h))
    vspec = pl.BlockSpec((1, t, GLA_DV), lambda bi, h: (bi, 0, h))
    return pl.pallas_call(
        _gla_kernel,
        grid=(b, GLA_HEADS),
        in_specs=[kspec, kspec, vspec, vspec, kspec, kspec,
                  pl.BlockSpec((1, GLA_DV), lambda bi, h: (0, 0))],
        out_specs=vspec,
        out_shape=jax.ShapeDtypeStruct((b, t, GLA_WIDTH), BF16),
        scratch_shapes=[
            pltpu.VMEM((t, GLA_DV), F32),
            pltpu.VMEM((GLA_DV, GLA_DK), F32),
            pltpu.VMEM((GLA_DV, GLA_DK), F32),
        ],
        compiler_params=_params(("parallel", "parallel")),
        name="gla_bidir",
    )(gq, gk, gv, sg, laf, lab, gain)


def _out_proj_kernel(x_ref, oa_ref, og_ref, ag_ref, wa_ref, wg_ref, fg_ref, h_ref, hn_ref):
    oa = _rms(oa_ref[...].astype(F32), ag_ref[...]).astype(BF16)
    h = (x_ref[...]
         + jnp.dot(oa, wa_ref[...], preferred_element_type=F32)
         + jnp.dot(og_ref[...], wg_ref[...], preferred_element_type=F32))
    h_ref[...] = h
    hn_ref[...] = _rms(h, fg_ref[...]).astype(BF16)


def _out_proj(x2, oa, og, ag, wa, wg, fg):
    n, d = x2.shape
    tm = PROJ_TM
    full = lambda i: (0, 0)
    row = lambda i: (i, 0)
    return pl.pallas_call(
        _out_proj_kernel,
        grid=(n // tm,),
        in_specs=[
            pl.BlockSpec((tm, d), row),
            pl.BlockSpec((tm, ATTN_WIDTH), row),
            pl.BlockSpec((tm, GLA_WIDTH), row),
            pl.BlockSpec((1, ATTN_WIDTH), full),
            pl.BlockSpec(wa.shape, full),
            pl.BlockSpec(wg.shape, full),
            pl.BlockSpec((1, d), full),
        ],
        out_specs=[pl.BlockSpec((tm, d), row), pl.BlockSpec((tm, d), row)],
        out_shape=[jax.ShapeDtypeStruct((n, d), F32), jax.ShapeDtypeStruct((n, d), BF16)],
        compiler_params=_params(("parallel",)),
        name="out_proj",
    )(x2, oa, og, ag, wa, wg, fg)


def _cmp_exchange(rows, i, j):
    a, b = rows[i], rows[j]
    if b is None:
        return
    if a is None:
        rows[i], rows[j] = b, None
        return
    rows[i], rows[j] = jnp.maximum(a, b), jnp.minimum(a, b)


def _bitonic_merge_desc(rows):
    n = len(rows)
    j = n // 2
    while j >= 1:
        for i in range(n):
            l = i ^ j
            if l > i:
                _cmp_exchange(rows, i, l)
        j //= 2


def _bitonic_sort_desc(rows):
    n = len(rows)
    k = 2
    while k <= n:
        j = k // 2
        while j >= 1:
            for i in range(n):
                l = i ^ j
                if l > i:
                    if (i & k) == 0:
                        _cmp_exchange(rows, i, l)
                    else:
                        _cmp_exchange(rows, l, i)
            j //= 2
        k *= 2


def _top_of_union(a, b):
    n = len(a)
    out = []
    for i in range(n):
        x, y = a[i], b[n - 1 - i]
        out.append(x if y is None else (y if x is None else jnp.maximum(x, y)))
    _bitonic_merge_desc(out)
    return out


def _top16_rows(s):
    l = s.shape[1]
    s3 = s.reshape(N_KEYS // SUBLANES, SUBLANES, l)
    rows = [s3[i] for i in range(N_KEYS // SUBLANES)]
    _bitonic_sort_desc(rows)
    for shift in (4, 2, 1):
        other = [pltpu.roll(r, shift, axis=0) for r in rows]
        rows = _top_of_union(rows, other)
    return rows


_CANDIDATES = [(i, j) for i in range(PEER_TOPK) for j in range(PEER_TOPK)
               if (i + 1) * (j + 1) <= PEER_TOPK]


def _bf16_bits(x):
    bits = lax.bitcast_convert_type(x, jnp.uint32)
    return (bits + jnp.uint32(0x7FFF) + ((bits >> 16) & jnp.uint32(1))) >> 16


def _bf16_pair_words(x):
    hi = _bf16_bits(x)
    return lax.bitcast_convert_type((hi << 16) | hi, jnp.int32)


def _bf16_packed_rows(x):
    hi = _bf16_bits(x)
    half = x.shape[0] // 2
    return lax.bitcast_convert_type(hi[:half] | (hi[half:] << 16), jnp.int32)


def _route_kernel(hn_ref, wqt_ref, keys_ref, rank_ref, e2_ref, cnt_ref, w1_ref, qt_ref):
    hh = pl.program_id(1)

    @pl.when(hh == 0)
    def _():
        qt_ref[...] = lax.dot_general(wqt_ref[...], hn_ref[...], NT_DIMS,
                                      preferred_element_type=F32).astype(BF16)

    base = pl.multiple_of(hh * PEER_QDIM, PEER_QDIM)
    q1 = qt_ref[pl.ds(base, PEER_HALF), :]
    q2 = qt_ref[pl.ds(base + PEER_HALF, PEER_HALF), :]
    s1 = jnp.dot(keys_ref[0, 0], q1, preferred_element_type=F32)
    s2 = jnp.dot(keys_ref[0, 1], q2, preferred_element_type=F32)
    tb = s1.shape[1]
    lgs = tb // LANES
    a = _top16_rows(s1)
    b = _top16_rows(s2)

    sub = lax.broadcasted_iota(jnp.int32, (SUBLANES, LANES), 0)

    def compact(r):
        out = r[:, :LANES]
        for g in range(1, lgs):
            out = jnp.where(sub == g, r[:, g * LANES:(g + 1) * LANES], out)
        return out

    def expand(c):
        return jnp.concatenate(
            [jnp.broadcast_to(c[g:g + 1, :], (SUBLANES, LANES)) for g in range(lgs)], axis=1)

    ac = [compact(r) for r in a]
    bc = [compact(r) for r in b]
    cand = [ac[i] + bc[j] for (i, j) in _CANDIDATES]
    cand += [None] * (64 - len(cand))
    groups = []
    for gi in range(4):
        grp = cand[gi * 16:(gi + 1) * 16]
        _bitonic_sort_desc(grp)
        groups.append(grp)
    top = _top_of_union(_top_of_union(groups[0], groups[1]),
                        _top_of_union(groups[2], groups[3]))
    z = jnp.zeros_like(top[0])
    for tk in top:
        z = z + jnp.exp(tk - top[0])
    tau = expand(top[PEER_TOPK - 1])
    inv_z = expand(1.0 / z)
    full = lambda r: jnp.broadcast_to(r[0:1, :], (N_KEYS, tb))
    tau_f = full(tau)
    cnt_bits = jnp.zeros((N_KEYS, tb), jnp.int32)
    rank_bits = jnp.zeros((N_KEYS, tb), jnp.int32)
    for j in range(PEER_TOPK):
        bj = full(b[j])
        bits = int(np.asarray(j + 1, dtype=np.float32).view(np.uint32)) >> 16
        cnt_bits = jnp.where(s1 + bj >= tau_f, bits, cnt_bits)
        rank_bits = jnp.where(bj > s2, bits, rank_bits)
    e2 = jnp.exp(s2 - full(b[0]))
    w1 = jnp.exp(s1 - full(a[0])) * full(inv_z)
    half = N_KEYS // 2
    rank_w = rank_bits[:half] | (rank_bits[half:] << 16)
    e2_w = _bf16_packed_rows(e2)
    cnt_w = cnt_bits | (cnt_bits << 16)
    w1_w = _bf16_pair_words(w1)
    for lg in range(tb // LANES):
        lanes = slice(lg * LANES, (lg + 1) * LANES)
        rank_ref[lg, 0] = rank_w[:, lanes]
        e2_ref[lg, 0] = e2_w[:, lanes]
        cnt_ref[lg, 0] = cnt_w[:, lanes]
        w1_ref[lg, 0] = w1_w[:, lanes]


def _peer_route(hn, wqt, keys):
    n, d = hn.shape
    tb = min(ROUTE_TB, n)
    lgs = tb // LANES
    tab = jax.ShapeDtypeStruct((n // LANES, PEER_HEADS, N_KEYS, LANES), jnp.int32)
    tab16 = jax.ShapeDtypeStruct((n // LANES, PEER_HEADS, N_KEYS // 2, LANES), jnp.int32)
    tspec = pl.BlockSpec((lgs, 1, N_KEYS, LANES), lambda i, h: (i, h, 0, 0))
    tspec16 = pl.BlockSpec((lgs, 1, N_KEYS // 2, LANES), lambda i, h: (i, h, 0, 0))
    return pl.pallas_call(
        _route_kernel,
        grid=(n // tb, PEER_HEADS),
        in_specs=[
            pl.BlockSpec((tb, d), lambda i, h: (i, 0)),
            pl.BlockSpec(wqt.shape, lambda i, h: (0, 0), pipeline_mode=pl.Buffered(1)),
            pl.BlockSpec((1, 2, N_KEYS, PEER_HALF), lambda i, h: (h, 0, 0, 0)),
        ],
        out_specs=[tspec16, tspec16, tspec, tspec],
        out_shape=[tab16, tab16, tab, tab],
        scratch_shapes=[pltpu.VMEM((PEER_HEADS * PEER_QDIM, tb), BF16)],
        compiler_params=_params(("parallel", "arbitrary")),
        name="peer_route",
    )(hn, wqt, keys)


def _gelu(x):
    return 0.5 * x * (1.0 + lax.erf(x * (2.0 ** -0.5)))


def _peer_kernel(hn_ref, h_ref, u0_ref, u_ref, vt_ref, rank_ref, e2_ref, cnt_ref, w1_ref, fg_ref,
                 y_ref, acc_ref, hta_ref, htb_ref, w_ref):
    j = pl.program_id(1)
    nj = pl.num_programs(1)
    lane_groups, te, _ = hta_ref.shape
    rows_per_step = te // N_KEYS

    def score(u_tile_ref, dst_ref):
        ht = lax.dot_general(u_tile_ref[...], hn_ref[...], NT_DIMS,
                             preferred_element_type=F32)
        for lg in range(lane_groups):
            dst_ref[lg] = ht[:, lg * LANES:(lg + 1) * LANES]

    @pl.when(j == 0)
    def _():
        acc_ref[...] = jnp.zeros_like(acc_ref)
        score(u0_ref, hta_ref)

    def step(cur_ref, nxt_ref):
        zero = jnp.zeros((BF16_ROWS, LANES), BF16)

        def row_tile(ref, lg, hh, rr):
            words = jnp.broadcast_to(ref[lg, hh, rr:rr + 1, :], (SUBLANES, LANES))
            return pltpu.bitcast(words, BF16)

        for lg in range(lane_groups):
            lanes = slice(lg * LANES, (lg + 1) * LANES)
            for rr in range(rows_per_step):
                cnt = [row_tile(cnt_ref, lg, hh, rr) for hh in range(PEER_HEADS)]
                w1 = [row_tile(w1_ref, lg, hh, rr) for hh in range(PEER_HEADS)]
                for kk in range(N_KEYS // BF16_ROWS):
                    words = slice(kk * SUBLANES, (kk + 1) * SUBLANES)
                    rows = slice(rr * N_KEYS + kk * BF16_ROWS, rr * N_KEYS + (kk + 1) * BF16_ROWS)
                    gate = zero
                    for hh in range(PEER_HEADS):
                        rank = pltpu.bitcast(rank_ref[lg, hh, words, :], BF16)
                        e2 = pltpu.bitcast(e2_ref[lg, hh, words, :], BF16)
                        gate = gate + jnp.where(rank < cnt[hh], e2, zero) * w1[hh]
                    w_ref[rows, lanes] = gate * _gelu(cur_ref[lg, rows, :]).astype(BF16)
        score(u_ref, nxt_ref)
        acc_ref[...] += jnp.dot(vt_ref[...], w_ref[...], preferred_element_type=F32)

    @pl.when(j % 2 == 0)
    def _():
        step(hta_ref, htb_ref)

    @pl.when(j % 2 == 1)
    def _():
        step(htb_ref, hta_ref)

    @pl.when(j == nj - 1)
    def _():
        h = h_ref[...] + acc_ref[...].T
        y_ref[...] = _rms(h, fg_ref[...])


def _peer_dense(hn, h, u, vt, rank, e2, cnt, w1, fg):
    n, d = hn.shape
    e = u.shape[0]
    tb, te = PEER_TB, PEER_TE
    nj = e // te
    lgs = tb // LANES
    rows = te // N_KEYS
    once = pl.Buffered(1)
    tspec = pl.BlockSpec((lgs, PEER_HEADS, N_KEYS // 2, LANES), lambda i, j: (i, 0, 0, 0),
                         pipeline_mode=once)
    rspec = pl.BlockSpec((lgs, PEER_HEADS, rows, LANES), lambda i, j: (i, 0, j, 0))
    return pl.pallas_call(
        _peer_kernel,
        grid=(n // tb, nj),
        in_specs=[
            pl.BlockSpec((tb, d), lambda i, j: (i, 0), pipeline_mode=once),
            pl.BlockSpec((tb, d), lambda i, j: (i, 0), pipeline_mode=once),
            pl.BlockSpec((te, d), lambda i, j: (0, 0), pipeline_mode=once),
            pl.BlockSpec((te, d), lambda i, j: (jnp.minimum(j + 1, nj - 1), 0)),
            pl.BlockSpec((None, d, te), lambda i, j: (j, 0, 0)),
            tspec, tspec, rspec, rspec,
            pl.BlockSpec((1, d), lambda i, j: (0, 0)),
        ],
        out_specs=pl.BlockSpec((tb, d), lambda i, j: (i, 0)),
        out_shape=jax.ShapeDtypeStruct((n, d), F32),
        scratch_shapes=[
            pltpu.VMEM((d, tb), F32),
            pltpu.VMEM((lgs, te, LANES), F32),
            pltpu.VMEM((lgs, te, LANES), F32),
            pltpu.VMEM((te, tb), BF16),
        ],
        compiler_params=_params(("parallel", "arbitrary")),
        name="peer_dense",
    )(hn, h, u, u, vt, rank, e2, cnt, w1, fg)


def _rope_tables(t):
    n_rows = t // GRID_W
    row = jnp.repeat(jnp.arange(n_rows, dtype=F32), GRID_W)
    col = jnp.tile(jnp.arange(GRID_W, dtype=F32), n_rows)
    freqs = ROPE_THETA ** (-jnp.arange(0, AXIS_DIM, 2, dtype=F32) / AXIS_DIM)
    ang = jnp.concatenate([row[:, None] * freqs, col[:, None] * freqs], axis=-1)
    cos, sin = jnp.cos(ang), jnp.sin(ang)
    return jnp.concatenate([cos, cos], axis=-1), jnp.concatenate([-sin, sin], axis=-1)


def _prepare(norm_mix_g, w_in, q_norm_g, k_norm_g, w_gate_f2, b_gate_f, w_gate_b2, b_gate_b,
             gla_norm_g, attn_out_norm_g, w_out, norm_ffn_g, peer_w_q, peer_sub_keys, peer_u,
             peer_v, final_norm_g):
    perm = np.concatenate([np.arange(0, HEAD_DIM, 2), np.arange(1, HEAD_DIM, 2)])
    kw = GLA_HEADS * GLA_DK
    o_k = ATTN_WIDTH
    o_v = o_k + KV_WIDTH
    o_g = o_v + KV_WIDTH
    o_lr = o_g + 2 * kw + 2 * GLA_WIDTH
    wq = w_in[:, :o_k].reshape(D_MODEL, N_HEADS, HEAD_DIM)[:, :, perm].reshape(D_MODEL, o_k)
    wk = w_in[:, o_k:o_v].reshape(D_MODEL, N_KV_HEADS, HEAD_DIM)[:, :, perm].reshape(D_MODEL, KV_WIDTH)
    w_qkv = jnp.concatenate([wq, wk, w_in[:, o_v:o_g]], axis=1).astype(BF16)
    w_lr = jnp.pad(w_in[:, o_lr:], ((0, 0), (0, LANES - 2 * GATE_RANK)))
    w_gla = jnp.concatenate([w_in[:, o_g:o_lr], w_lr], axis=1).astype(BF16)
    wg = jnp.zeros((LANES, 2 * kw), F32)
    wg = wg.at[:GATE_RANK, :kw].set(w_gate_f2).at[GATE_RANK:2 * GATE_RANK, kw:].set(w_gate_b2)
    bg = jnp.concatenate([b_gate_f, b_gate_b])[None, :]

    keys = jnp.stack([peer_sub_keys[:, 0], peer_sub_keys[:, 1][:, PEER_KEY2_ROWS]], axis=1)
    return dict(
        norm_mix_g=norm_mix_g[None, :], w_qkv=w_qkv, w_gla=w_gla, wg=wg, bg=bg,
        qg=q_norm_g[perm][None, :], kg=k_norm_g[perm][None, :],
        gla_gain=gla_norm_g[None, :], attn_gain=attn_out_norm_g[None, :],
        wo_a=w_out[:ATTN_WIDTH].astype(BF16), wo_g=w_out[ATTN_WIDTH:].astype(BF16),
        ffn_g=norm_ffn_g[None, :], wqt=peer_w_q.T.astype(BF16),
        keys=keys.astype(BF16), u=peer_u.astype(BF16),
        vt=peer_v.astype(BF16).reshape(-1, PEER_TE, D_MODEL).transpose(0, 2, 1),
        final_g=final_norm_g[None, :],
    )


def _layer(x, p):
    b, t, d = x.shape
    n = b * t
    x2 = x.reshape(n, d)
    cos, sin = _rope_tables(t)
    q, k, v = _qkv_proj(x2, p["norm_mix_g"], p["w_qkv"], p["qg"], p["kg"], cos, sin, t)
    gq, gk, gv, sg, laf, lab = _gla_proj(x2, p["norm_mix_g"], p["w_gla"], p["wg"], p["bg"])
    oa = _attention(q.reshape(b, t, -1), k.reshape(b, t, -1), v.reshape(b, t, -1))
    r3 = lambda a: a.reshape(b, t, -1)
    og = _gla(r3(gq), r3(gk), r3(gv), r3(sg), r3(laf), r3(lab), p["gla_gain"])
    h, hn = _out_proj(x2, oa.reshape(n, -1), og.reshape(n, -1), p["attn_gain"],
                      p["wo_a"], p["wo_g"], p["ffn_g"])
    rank, e2, cnt, w1 = _peer_route(hn, p["wqt"], p["keys"])
    y = _peer_dense(hn, h, p["u"], p["vt"], rank, e2, cnt, w1, p["final_g"])
    return y.reshape(b, t, d)


def kernel(x_prompt, x_sample, norm_mix_g, w_in, q_norm_g, k_norm_g, w_gate_f2, b_gate_f,
           w_gate_b2, b_gate_b, gla_norm_g, attn_out_norm_g, w_out, norm_ffn_g, peer_w_q,
           peer_sub_keys, peer_u, peer_v, final_norm_g):
    p = _prepare(norm_mix_g, w_in, q_norm_g, k_norm_g, w_gate_f2, b_gate_f, w_gate_b2,
                 b_gate_b, gla_norm_g, attn_out_norm_g, w_out, norm_ffn_g, peer_w_q,
                 peer_sub_keys, peer_u, peer_v, final_norm_g)
    return (_layer(x_prompt, p), _layer(x_sample, p))
```

```python
import functools
import math

import jax
import jax.numpy as jnp
import numpy as np
from jax import lax
from jax.experimental import pallas as pl
from jax.experimental.pallas import tpu as pltpu

F32 = jnp.float32
BF16 = jnp.bfloat16

D_MODEL = 2048
GRID_W = 64
HEAD_DIM = 128
N_HEADS = 8
N_KV_HEADS = 2
GQA_GROUP = N_HEADS // N_KV_HEADS
ATTN_WIDTH = N_HEADS * HEAD_DIM
KV_WIDTH = N_KV_HEADS * HEAD_DIM
ROPE_THETA = 10000.0
AXIS_DIM = HEAD_DIM // 2
GLA_HEADS = 4
GLA_DK = 128
GLA_DV = 256
GLA_WIDTH = GLA_HEADS * GLA_DV
GATE_RANK = 16
GATE_NORMALIZER = 16.0
PEER_HEADS = 8
N_KEYS = 128
PEER_TOPK = 16
PEER_HALF = 128
PEER_QDIM = 2 * PEER_HALF
EPS = 1e-6

LANES = 128
SUBLANES = 8
BF16_ROWS = 16
MXU_DIM = 256
VMEM_LIMIT = 56 * 1024 * 1024

PROJ_TM = 512
ATTN_TQ = 256
GLA_CHUNK = 64
GLA_UNROLL = 8
ROUTE_TB = 512
PEER_TB = 512
PEER_TE = 1024

PEER_KEY2_ROWS = np.concatenate([np.arange(0, N_KEYS, 2), np.arange(1, N_KEYS, 2)])

NT_DIMS = (((1,), (1,)), ((), ()))
TN_DIMS = (((0,), (0,)), ((), ()))


def _params(sem, vmem=VMEM_LIMIT, flags=None):
    return pltpu.CompilerParams(dimension_semantics=sem, vmem_limit_bytes=vmem, flags=flags)


def _rms(x, g):
    return x * lax.rsqrt(jnp.mean(x * x, axis=-1, keepdims=True) + EPS) * g


def _qkv_kernel(x_ref, g_ref, w_ref, qg_ref, kg_ref, cos_ref, sin_ref, q_ref, k_ref, v_ref):
    xn = _rms(x_ref[...], g_ref[...]).astype(BF16)
    z = jnp.dot(xn, w_ref[...], preferred_element_type=F32)
    c = cos_ref[...]
    s = sin_ref[...]

    def norm_rope(hd, gain, scale):
        hn = _rms(hd, gain)
        rot = pltpu.roll(hn, HEAD_DIM // 2, axis=1)
        return ((hn * c + rot * s) * scale).astype(BF16)

    for hh in range(N_HEADS):
        sl = slice(hh * HEAD_DIM, (hh + 1) * HEAD_DIM)
        q_ref[:, sl] = norm_rope(z[:, sl], qg_ref[...], HEAD_DIM ** -0.5)
    for hh in range(N_KV_HEADS):
        sl = slice(hh * HEAD_DIM, (hh + 1) * HEAD_DIM)
        zsl = slice(ATTN_WIDTH + hh * HEAD_DIM, ATTN_WIDTH + (hh + 1) * HEAD_DIM)
        k_ref[:, sl] = norm_rope(z[:, zsl], kg_ref[...], 1.0)
    v_ref[...] = z[:, ATTN_WIDTH + KV_WIDTH:].astype(BF16)


def _qkv_proj(x2, g, w, qg, kg, cos, sin, seq):
    n, d = x2.shape
    tm = min(PROJ_TM, seq)
    nt = seq // tm
    wq = w.shape[1]
    full = lambda i: (0, 0)
    return pl.pallas_call(
        _qkv_kernel,
        grid=(n // tm,),
        in_specs=[
            pl.BlockSpec((tm, d), lambda i: (i, 0)),
            pl.BlockSpec((1, d), full),
            pl.BlockSpec((d, wq), full),
            pl.BlockSpec((1, HEAD_DIM), full),
            pl.BlockSpec((1, HEAD_DIM), full),
            pl.BlockSpec((tm, HEAD_DIM), lambda i: (i % nt, 0)),
            pl.BlockSpec((tm, HEAD_DIM), lambda i: (i % nt, 0)),
        ],
        out_specs=[
            pl.BlockSpec((tm, ATTN_WIDTH), lambda i: (i, 0)),
            pl.BlockSpec((tm, KV_WIDTH), lambda i: (i, 0)),
            pl.BlockSpec((tm, KV_WIDTH), lambda i: (i, 0)),
        ],
        out_shape=[
            jax.ShapeDtypeStruct((n, ATTN_WIDTH), BF16),
            jax.ShapeDtypeStruct((n, KV_WIDTH), BF16),
            jax.ShapeDtypeStruct((n, KV_WIDTH), BF16),
        ],
        compiler_params=_params(("parallel",)),
        name="qkv_proj",
    )(x2, g, w, qg, kg, cos, sin)


def _log_sigmoid(x):
    return jnp.minimum(x, 0.0) - jnp.log(1.0 + jnp.exp(-jnp.abs(x)))


def _gla_proj_kernel(x_ref, g_ref, w_ref, wg_ref, bg_ref,
                     gq_ref, gk_ref, gv_ref, sg_ref, laf_ref, lab_ref):
    xn = _rms(x_ref[...], g_ref[...]).astype(BF16)
    z = jnp.dot(xn, w_ref[...], preferred_element_type=F32)
    kw = GLA_HEADS * GLA_DK
    gq_ref[...] = (z[:, :kw] * (GLA_DK ** -0.5)).astype(BF16)
    gk_ref[...] = z[:, kw:2 * kw].astype(BF16)
    gv_ref[...] = z[:, 2 * kw:2 * kw + GLA_WIDTH].astype(BF16)
    gr = z[:, 2 * kw + GLA_WIDTH:2 * kw + 2 * GLA_WIDTH]
    sg_ref[...] = (gr * (1.0 / (1.0 + jnp.exp(-gr)))).astype(BF16)
    lr = z[:, 2 * kw + 2 * GLA_WIDTH:]
    pre = jnp.dot(lr, wg_ref[...], preferred_element_type=F32,
                  precision=lax.Precision.HIGHEST) + bg_ref[...]
    la = _log_sigmoid(pre) * (1.0 / GATE_NORMALIZER)
    laf_ref[...] = la[:, :kw]
    lab_ref[...] = la[:, kw:]


def _gla_proj(x2, g, w, wg, bg):
    n, d = x2.shape
    tm = PROJ_TM
    kw = GLA_HEADS * GLA_DK
    full = lambda i: (0, 0)
    row = lambda i: (i, 0)
    return pl.pallas_call(
        _gla_proj_kernel,
        grid=(n // tm,),
        in_specs=[
            pl.BlockSpec((tm, d), row),
            pl.BlockSpec((1, d), full),
            pl.BlockSpec(w.shape, full),
            pl.BlockSpec(wg.shape, full),
            pl.BlockSpec(bg.shape, full),
        ],
        out_specs=[
            pl.BlockSpec((tm, kw), row),
            pl.BlockSpec((tm, kw), row),
            pl.BlockSpec((tm, GLA_WIDTH), row),
            pl.BlockSpec((tm, GLA_WIDTH), row),
            pl.BlockSpec((tm, kw), row),
            pl.BlockSpec((tm, kw), row),
        ],
        out_shape=[
            jax.ShapeDtypeStruct((n, kw), BF16),
            jax.ShapeDtypeStruct((n, kw), BF16),
            jax.ShapeDtypeStruct((n, GLA_WIDTH), BF16),
            jax.ShapeDtypeStruct((n, GLA_WIDTH), BF16),
            jax.ShapeDtypeStruct((n, kw), F32),
            jax.ShapeDtypeStruct((n, kw), F32),
        ],
        compiler_params=_params(("parallel",)),
        name="gla_proj",
    )(x2, g, w, wg, bg)


def _attn_kernel(q_ref, k_ref, v_ref, o_ref):
    k = k_ref[0]
    v = v_ref[0]
    for hh in range(GQA_GROUP):
        sl = slice(hh * HEAD_DIM, (hh + 1) * HEAD_DIM)
        s = lax.dot_general(q_ref[0, :, sl], k, NT_DIMS, preferred_element_type=F32)
        m = jnp.max(s, axis=-1, keepdims=True)
        p = jnp.exp(s - m)
        l = jnp.sum(p, axis=-1, keepdims=True)
        o = jnp.dot(p.astype(BF16), v, preferred_element_type=F32)
        o_ref[0, :, sl] = (o / l).astype(BF16)


def _attention(q, k, v):
    b, t, _ = q.shape
    tq = min(ATTN_TQ, t)
    gw = GQA_GROUP * HEAD_DIM
    return pl.pallas_call(
        _attn_kernel,
        grid=(b, N_KV_HEADS, t // tq),
        in_specs=[
            pl.BlockSpec((1, tq, gw), lambda bi, g, qi: (bi, qi, g)),
            pl.BlockSpec((1, t, HEAD_DIM), lambda bi, g, qi: (bi, 0, g)),
            pl.BlockSpec((1, t, HEAD_DIM), lambda bi, g, qi: (bi, 0, g)),
        ],
        out_specs=pl.BlockSpec((1, tq, gw), lambda bi, g, qi: (bi, qi, g)),
        out_shape=jax.ShapeDtypeStruct((b, t, ATTN_WIDTH), BF16),
        compiler_params=_params(("parallel", "parallel", "arbitrary")),
        name="gqa_attention",
    )(q, k, v)


def _split_dot(tri, g):
    hi = g.astype(BF16)
    lo = (g - hi.astype(F32)).astype(BF16)
    return (jnp.dot(tri, hi, preferred_element_type=F32)
            + jnp.dot(tri, lo, preferred_element_type=F32))


def _gla_kernel(q_ref, k_ref, v_ref, sg_ref, gf_ref, gb_ref, gain_ref, o_ref, acc_ref,
                sf_ref, sb_ref):
    t = q_ref.shape[1]
    c = GLA_CHUNK
    n = t // c
    mid = c // 2
    ri = lax.broadcasted_iota(jnp.int32, (c, c), 0)
    ci = lax.broadcasted_iota(jnp.int32, (c, c), 1)
    lower = ri >= ci
    tri_lo = lower.astype(BF16)
    tri_up = (ri <= ci).astype(BF16)
    sf_ref[...] = jnp.zeros_like(sf_ref)
    sb_ref[...] = jnp.zeros_like(sb_ref)

    def chunk(idx, s_ref, g_ref, tri, first_row, mask):
        rows = pl.ds(pl.multiple_of(idx * c, c), c)
        q = q_ref[0, rows, :].astype(F32)
        k = k_ref[0, rows, :].astype(F32)
        v = v_ref[0, rows, :]
        b = _split_dot(tri, g_ref[0, rows, :])
        tot = b[c - 1:c, :] if first_row is False else b[0:1, :]
        ref = b[mid:mid + 1, :]
        qt = (q * jnp.exp(b - ref)).astype(BF16)
        kt = (k * jnp.exp(ref - b)).astype(BF16)
        a = lax.dot_general(qt, kt, NT_DIMS, preferred_element_type=F32)
        a = jnp.where(mask, a, 0.0).astype(BF16)
        st = s_ref[...]
        o = (jnp.dot(a, v, preferred_element_type=F32)
             + lax.dot_general((q * jnp.exp(b)).astype(BF16), st.astype(BF16), NT_DIMS,
                               preferred_element_type=F32))
        kd = (k * jnp.exp(tot - b)).astype(BF16)
        s_ref[...] = st * jnp.exp(tot) + lax.dot_general(v, kd, TN_DIMS,
                                                          preferred_element_type=F32)
        return rows, o

    def fwd(i):
        return chunk(i, sf_ref, gf_ref, tri_lo, False, lower)

    def bwd(i):
        return chunk(n - 1 - i, sb_ref, gb_ref, tri_up, True, ri <= ci)

    def finish(rows, o):
        y = _rms(o, gain_ref[...]) * sg_ref[0, rows, :].astype(F32)
        o_ref[0, rows, :] = y.astype(BF16)

    def first_half(i, carry):
        rf, of = fwd(i)
        rb, ob = bwd(i)
        acc_ref[rf, :] = of
        acc_ref[rb, :] = ob
        return carry

    def second_half(i, carry):
        rf, of = fwd(i)
        rb, ob = bwd(i)
        finish(rf, acc_ref[rf, :] + of)
        finish(rb, acc_ref[rb, :] + ob)
        return carry

    lax.fori_loop(0, n // 2, first_half, 0, unroll=GLA_UNROLL)
    lax.fori_loop(n // 2, n, second_half, 0, unroll=GLA_UNROLL)


def _gla(gq, gk, gv, sg, laf, lab, gain):
    b, t, _ = gq.shape
    kspec = pl.BlockSpec((1, t, GLA_DK), lambda bi, h: (bi, 0, h))
    vspec = pl.BlockSpec((1, t, GLA_DV), lambda bi, h: (bi, 0, h))
    return pl.pallas_call(
        _gla_kernel,
        grid=(b, GLA_HEADS),
        in_specs=[kspec, kspec, vspec, vspec, kspec, kspec,
                  pl.BlockSpec((1, GLA_DV), lambda bi, h: (0, 0))],
        out_specs=vspec,
        out_shape=jax.ShapeDtypeStruct((b, t, GLA_WIDTH), BF16),
        scratch_shapes=[
            pltpu.VMEM((t, GLA_DV), F32),
            pltpu.VMEM((GLA_DV, GLA_DK), F32),
            pltpu.VMEM((GLA_DV, GLA_DK), F32),
        ],
        compiler_params=_params(("parallel", "parallel")),
        name="gla_bidir",
    )(gq, gk, gv, sg, laf, lab, gain)


def _out_proj_kernel(x_ref, oa_ref, og_ref, ag_ref, wa_ref, wg_ref, fg_ref, h_ref, hn_ref):
    oa = _rms(oa_ref[...].astype(F32), ag_ref[...]).astype(BF16)
    h = (x_ref[...]
         + jnp.dot(oa, wa_ref[...], preferred_element_type=F32)
         + jnp.dot(og_ref[...], wg_ref[...], preferred_element_type=F32))
    h_ref[...] = h
    hn_ref[...] = _rms(h, fg_ref[...]).astype(BF16)


def _out_proj(x2, oa, og, ag, wa, wg, fg):
    n, d = x2.shape
    tm = PROJ_TM
    full = lambda i: (0, 0)
    row = lambda i: (i, 0)
    return pl.pallas_call(
        _out_proj_kernel,
        grid=(n // tm,),
        in_specs=[
            pl.BlockSpec((tm, d), row),
            pl.BlockSpec((tm, ATTN_WIDTH), row),
            pl.BlockSpec((tm, GLA_WIDTH), row),
            pl.BlockSpec((1, ATTN_WIDTH), full),
            pl.BlockSpec(wa.shape, full),
            pl.BlockSpec(wg.shape, full),
            pl.BlockSpec((1, d), full),
        ],
        out_specs=[pl.BlockSpec((tm, d), row), pl.BlockSpec((tm, d), row)],
        out_shape=[jax.ShapeDtypeStruct((n, d), F32), jax.ShapeDtypeStruct((n, d), BF16)],
        compiler_params=_params(("parallel",)),
        name="out_proj",
    )(x2, oa, og, ag, wa, wg, fg)


def _cmp_exchange(rows, i, j):
    a, b = rows[i], rows[j]
    if b is None:
        return
    if a is None:
        rows[i], rows[j] = b, None
        return
    rows[i], rows[j] = jnp.maximum(a, b), jnp.minimum(a, b)


def _bitonic_merge_desc(rows):
    n = len(rows)
    j = n // 2
    while j >= 1:
        for i in range(n):
            l = i ^ j
            if l > i:
                _cmp_exchange(rows, i, l)
        j //= 2


def _bitonic_sort_desc(rows):
    n = len(rows)
    k = 2
    while k <= n:
        j = k // 2
        while j >= 1:
            for i in range(n):
                l = i ^ j
                if l > i:
                    if (i & k) == 0:
                        _cmp_exchange(rows, i, l)
                    else:
                        _cmp_exchange(rows, l, i)
            j //= 2
        k *= 2


def _top_of_union(a, b):
    n = len(a)
    out = []
    for i in range(n):
        x, y = a[i], b[n - 1 - i]
        out.append(x if y is None else (y if x is None else jnp.maximum(x, y)))
    _bitonic_merge_desc(out)
    return out


def _top16_rows(s):
    l = s.shape[1]
    s3 = s.reshape(N_KEYS // SUBLANES, SUBLANES, l)
    rows = [s3[i] for i in range(N_KEYS // SUBLANES)]
    _bitonic_sort_desc(rows)
    for shift in (4, 2, 1):
        other = [pltpu.roll(r, shift, axis=0) for r in rows]
        rows = _top_of_union(rows, other)
    return rows


_CANDIDATES = [(i, j) for i in range(PEER_TOPK) for j in range(PEER_TOPK)
               if (i + 1) * (j + 1) <= PEER_TOPK]


def _bf16_bits(x):
    bits = lax.bitcast_convert_type(x, jnp.uint32)
    return (bits + jnp.uint32(0x7FFF) + ((bits >> 16) & jnp.uint32(1))) >> 16


def _bf16_pair_words(x):
    hi = _bf16_bits(x)
    return lax.bitcast_convert_type((hi << 16) | hi, jnp.int32)


def _bf16_packed_rows(x):
    hi = _bf16_bits(x)
    half = x.shape[0] // 2
    return lax.bitcast_convert_type(hi[:half] | (hi[half:] << 16), jnp.int32)


def _route_kernel(hn_ref, wqt_ref, keys_ref, rank_ref, e2_ref, cnt_ref, w1_ref, qt_ref):
    hh = pl.program_id(1)

    @pl.when(hh == 0)
    def _():
        qt_ref[...] = lax.dot_general(wqt_ref[...], hn_ref[...], NT_DIMS,
                                      preferred_element_type=F32).astype(BF16)

    base = pl.multiple_of(hh * PEER_QDIM, PEER_QDIM)
    q1 = qt_ref[pl.ds(base, PEER_HALF), :]
    q2 = qt_ref[pl.ds(base + PEER_HALF, PEER_HALF), :]
    s1 = jnp.dot(keys_ref[0, 0], q1, preferred_element_type=F32)
    s2 = jnp.dot(keys_ref[0, 1], q2, preferred_element_type=F32)
    tb = s1.shape[1]
    lgs = tb // LANES
    a = _top16_rows(s1)
    b = _top16_rows(s2)

    sub = lax.broadcasted_iota(jnp.int32, (SUBLANES, LANES), 0)

    def compact(r):
        out = r[:, :LANES]
        for g in range(1, lgs):
            out = jnp.where(sub == g, r[:, g * LANES:(g + 1) * LANES], out)
        return out

    def expand(c):
        return jnp.concatenate(
            [jnp.broadcast_to(c[g:g + 1, :], (SUBLANES, LANES)) for g in range(lgs)], axis=1)

    ac = [compact(r) for r in a]
    bc = [compact(r) for r in b]
    cand = [ac[i] + bc[j] for (i, j) in _CANDIDATES]
    cand += [None] * (64 - len(cand))
    groups = []
    for gi in range(4):
        grp = cand[gi * 16:(gi + 1) * 16]
        _bitonic_sort_desc(grp)
        groups.append(grp)
    top = _top_of_union(_top_of_union(groups[0], groups[1]),
                        _top_of_union(groups[2], groups[3]))
    z = jnp.zeros_like(top[0])
    for tk in top:
        z = z + jnp.exp(tk - top[0])
    tau = expand(top[PEER_TOPK - 1])
    inv_z = expand(1.0 / z)
    full = lambda r: jnp.broadcast_to(r[0:1, :], (N_KEYS, tb))
    tau_f = full(tau)
    cnt_bits = jnp.zeros((N_KEYS, tb), jnp.int32)
    rank_bits = jnp.zeros((N_KEYS, tb), jnp.int32)
    for j in range(PEER_TOPK):
        bj = full(b[j])
        bits = int(np.asarray(j + 1, dtype=np.float32).view(np.uint32)) >> 16
        cnt_bits = jnp.where(s1 + bj >= tau_f, bits, cnt_bits)
        rank_bits = jnp.where(bj > s2, bits, rank_bits)
    e2 = jnp.exp(s2 - full(b[0]))
    w1 = jnp.exp(s1 - full(a[0])) * full(inv_z)
    half = N_KEYS // 2
    rank_w = rank_bits[:half] | (rank_bits[half:] << 16)
    e2_w = _bf16_packed_rows(e2)
    cnt_w = cnt_bits | (cnt_bits << 16)
    w1_w = _bf16_pair_words(w1)
    for lg in range(tb // LANES):
        lanes = slice(lg * LANES, (lg + 1) * LANES)
        rank_ref[lg, 0] = rank_w[:, lanes]
        e2_ref[lg, 0] = e2_w[:, lanes]
        cnt_ref[lg, 0] = cnt_w[:, lanes]
        w1_ref[lg, 0] = w1_w[:, lanes]


def _peer_route(hn, wqt, keys):
    n, d = hn.shape
    tb = min(ROUTE_TB, n)
    lgs = tb // LANES
    tab = jax.ShapeDtypeStruct((n // LANES, PEER_HEADS, N_KEYS, LANES), jnp.int32)
    tab16 = jax.ShapeDtypeStruct((n // LANES, PEER_HEADS, N_KEYS // 2, LANES), jnp.int32)
    tspec = pl.BlockSpec((lgs, 1, N_KEYS, LANES), lambda i, h: (i, h, 0, 0))
    tspec16 = pl.BlockSpec((lgs, 1, N_KEYS // 2, LANES), lambda i, h: (i, h, 0, 0))
    return pl.pallas_call(
        _route_kernel,
        grid=(n // tb, PEER_HEADS),
        in_specs=[
            pl.BlockSpec((tb, d), lambda i, h: (i, 0)),
            pl.BlockSpec(wqt.shape, lambda i, h: (0, 0), pipeline_mode=pl.Buffered(1)),
            pl.BlockSpec((1, 2, N_KEYS, PEER_HALF), lambda i, h: (h, 0, 0, 0)),
        ],
        out_specs=[tspec16, tspec16, tspec, tspec],
        out_shape=[tab16, tab16, tab, tab],
        scratch_shapes=[pltpu.VMEM((PEER_HEADS * PEER_QDIM, tb), BF16)],
        compiler_params=_params(("parallel", "arbitrary")),
        name="peer_route",
    )(hn, wqt, keys)


def _gelu(x):
    return 0.5 * x * (1.0 + lax.erf(x * (2.0 ** -0.5)))


def _peer_kernel(hn_ref, h_ref, u0_ref, u_ref, vt_ref, rank_ref, e2_ref, cnt_ref, w1_ref, fg_ref,
                 y_ref, acc_ref, hta_ref, htb_ref, w_ref):
    j = pl.program_id(1)
    nj = pl.num_programs(1)
    lane_groups, te, _ = hta_ref.shape
    rows_per_step = te // N_KEYS

    def score(u_tile_ref, dst_ref, hn=None):
        hn = hn_ref[...] if hn is None else hn
        ht = lax.dot_general(u_tile_ref[...], hn, NT_DIMS,
                             preferred_element_type=F32)
        for lg in range(lane_groups):
            dst_ref[lg] = ht[:, lg * LANES:(lg + 1) * LANES]

    def released_by(gates):
        hn = hn_ref[...]
        tb, d = hn.shape
        n_tiles, k_tiles = tb // MXU_DIM, d // MXU_DIM
        groups = MXU_DIM // BF16_ROWS
        assert len(gates) == n_tiles * k_tiles * groups
        rows = []
        for nt in range(n_tiles):
            for rg in range(groups):
                r0 = nt * MXU_DIM + rg * BF16_ROWS
                cols = []
                for kt in range(k_tiles):
                    g = gates[(nt * k_tiles + kt) * groups + rg]
                    z = g * jnp.zeros_like(g)
                    z = jnp.concatenate([z] * (MXU_DIM // LANES), axis=1)
                    cols.append(hn[r0:r0 + BF16_ROWS, kt * MXU_DIM:(kt + 1) * MXU_DIM] + z)
                rows.append(jnp.concatenate(cols, axis=1))
        return jnp.concatenate(rows, axis=0)

    @pl.when(j == 0)
    def _():
        acc_ref[...] = jnp.zeros_like(acc_ref)
        score(u0_ref, hta_ref)

    def step(cur_ref, nxt_ref):
        zero = jnp.zeros((BF16_ROWS, LANES), BF16)
        gates = []

        def row_tile(ref, lg, hh, rr):
            words = jnp.broadcast_to(ref[lg, hh, rr:rr + 1, :], (SUBLANES, LANES))
            return pltpu.bitcast(words, BF16)

        for lg in range(lane_groups):
            lanes = slice(lg * LANES, (lg + 1) * LANES)
            for rr in range(rows_per_step):
                cnt = [row_tile(cnt_ref, lg, hh, rr) for hh in range(PEER_HEADS)]
                w1 = [row_tile(w1_ref, lg, hh, rr) for hh in range(PEER_HEADS)]
                for kk in range(N_KEYS // BF16_ROWS):
                    words = slice(kk * SUBLANES, (kk + 1) * SUBLANES)
                    rows = slice(rr * N_KEYS + kk * BF16_ROWS, rr * N_KEYS + (kk + 1) * BF16_ROWS)
                    gate = zero
                    for hh in range(PEER_HEADS):
                        rank = pltpu.bitcast(rank_ref[lg, hh, words, :], BF16)
                        e2 = pltpu.bitcast(e2_ref[lg, hh, words, :], BF16)
                        gate = gate + jnp.where(rank < cnt[hh], e2, zero) * w1[hh]
                    gates.append(gate)
                    w_ref[rows, lanes] = gate * _gelu(cur_ref[lg, rows, :]).astype(BF16)
        score(u_ref, nxt_ref, released_by(gates))
        acc_ref[...] += jnp.dot(vt_ref[...], w_ref[...], preferred_element_type=F32)

    @pl.when(j % 2 == 0)
    def _():
        step(hta_ref, htb_ref)

    @pl.when(j % 2 == 1)
    def _():
        step(htb_ref, hta_ref)

    @pl.when(j == nj - 1)
    def _():
        h = h_ref[...] + acc_ref[...].T
        y_ref[...] = _rms(h, fg_ref[...])


def _peer_dense(hn, h, u, vt, rank, e2, cnt, w1, fg):
    n, d = hn.shape
    e = u.shape[0]
    tb, te = PEER_TB, PEER_TE
    nj = e // te
    lgs = tb // LANES
    rows = te // N_KEYS
    once = pl.Buffered(1)
    tspec = pl.BlockSpec((lgs, PEER_HEADS, N_KEYS // 2, LANES), lambda i, j: (i, 0, 0, 0),
                         pipeline_mode=once)
    rspec = pl.BlockSpec((lgs, PEER_HEADS, rows, LANES), lambda i, j: (i, 0, j, 0))
    return pl.pallas_call(
        _peer_kernel,
        grid=(n // tb, nj),
        in_specs=[
            pl.BlockSpec((tb, d), lambda i, j: (i, 0), pipeline_mode=once),
            pl.BlockSpec((tb, d), lambda i, j: (i, 0), pipeline_mode=once),
            pl.BlockSpec((te, d), lambda i, j: (0, 0), pipeline_mode=once),
            pl.BlockSpec((te, d), lambda i, j: (jnp.minimum(j + 1, nj - 1), 0)),
            pl.BlockSpec((None, d, te), lambda i, j: (j, 0, 0)),
            tspec, tspec, rspec, rspec,
            pl.BlockSpec((1, d), lambda i, j: (0, 0)),
        ],
        out_specs=pl.BlockSpec((tb, d), lambda i, j: (i, 0)),
        out_shape=jax.ShapeDtypeStruct((n, d), F32),
        scratch_shapes=[
            pltpu.VMEM((d, tb), F32),
            pltpu.VMEM((lgs, te, LANES), F32),
            pltpu.VMEM((lgs, te, LANES), F32),
            pltpu.VMEM((te, tb), BF16),
        ],
        compiler_params=_params(("parallel", "arbitrary")),
        name="peer_dense",
    )(hn, h, u, u, vt, rank, e2, cnt, w1, fg)


def _rope_tables(t):
    n_rows = t // GRID_W
    row = jnp.repeat(jnp.arange(n_rows, dtype=F32), GRID_W)
    col = jnp.tile(jnp.arange(GRID_W, dtype=F32), n_rows)
    freqs = ROPE_THETA ** (-jnp.arange(0, AXIS_DIM, 2, dtype=F32) / AXIS_DIM)
    ang = jnp.concatenate([row[:, None] * freqs, col[:, None] * freqs], axis=-1)
    cos, sin = jnp.cos(ang), jnp.sin(ang)
    return jnp.concatenate([cos, cos], axis=-1), jnp.concatenate([-sin, sin], axis=-1)


def _prepare(norm_mix_g, w_in, q_norm_g, k_norm_g, w_gate_f2, b_gate_f, w_gate_b2, b_gate_b,
             gla_norm_g, attn_out_norm_g, w_out, norm_ffn_g, peer_w_q, peer_sub_keys, peer_u,
             peer_v, final_norm_g):
    perm = np.concatenate([np.arange(0, HEAD_DIM, 2), np.arange(1, HEAD_DIM, 2)])
    kw = GLA_HEADS * GLA_DK
    o_k = ATTN_WIDTH
    o_v = o_k + KV_WIDTH
    o_g = o_v + KV_WIDTH
    o_lr = o_g + 2 * kw + 2 * GLA_WIDTH
    wq = w_in[:, :o_k].reshape(D_MODEL, N_HEADS, HEAD_DIM)[:, :, perm].reshape(D_MODEL, o_k)
    wk = w_in[:, o_k:o_v].reshape(D_MODEL, N_KV_HEADS, HEAD_DIM)[:, :, perm].reshape(D_MODEL, KV_WIDTH)
    w_qkv = jnp.concatenate([wq, wk, w_in[:, o_v:o_g]], axis=1).astype(BF16)
    w_lr = jnp.pad(w_in[:, o_lr:], ((0, 0), (0, LANES - 2 * GATE_RANK)))
    w_gla = jnp.concatenate([w_in[:, o_g:o_lr], w_lr], axis=1).astype(BF16)
    wg = jnp.zeros((LANES, 2 * kw), F32)
    wg = wg.at[:GATE_RANK, :kw].set(w_gate_f2).at[GATE_RANK:2 * GATE_RANK, kw:].set(w_gate_b2)
    bg = jnp.concatenate([b_gate_f, b_gate_b])[None, :]

    keys = jnp.stack([peer_sub_keys[:, 0], peer_sub_keys[:, 1][:, PEER_KEY2_ROWS]], axis=1)
    return dict(
        norm_mix_g=norm_mix_g[None, :], w_qkv=w_qkv, w_gla=w_gla, wg=wg, bg=bg,
        qg=q_norm_g[perm][None, :], kg=k_norm_g[perm][None, :],
        gla_gain=gla_norm_g[None, :], attn_gain=attn_out_norm_g[None, :],
        wo_a=w_out[:ATTN_WIDTH].astype(BF16), wo_g=w_out[ATTN_WIDTH:].astype(BF16),
        ffn_g=norm_ffn_g[None, :], wqt=peer_w_q.T.astype(BF16),
        keys=keys.astype(BF16), u=peer_u.astype(BF16),
        vt=peer_v.astype(BF16).reshape(-1, PEER_TE, D_MODEL).transpose(0, 2, 1),
        final_g=final_norm_g[None, :],
    )


def _layer(x, p):
    b, t, d = x.shape
    n = b * t
    x2 = x.reshape(n, d)
    cos, sin = _rope_tables(t)
    q, k, v = _qkv_proj(x2, p["norm_mix_g"], p["w_qkv"], p["qg"], p["kg"], cos, sin, t)
    gq, gk, gv, sg, laf, lab = _gla_proj(x2, p["norm_mix_g"], p["w_gla"], p["wg"], p["bg"])
    oa = _attention(q.reshape(b, t, -1), k.reshape(b, t, -1), v.reshape(b, t, -1))
    r3 = lambda a: a.reshape(b, t, -1)
    og = _gla(r3(gq), r3(gk), r3(gv), r3(sg), r3(laf), r3(lab), p["gla_gain"])
    h, hn = _out_proj(x2, oa.reshape(n, -1), og.reshape(n, -1), p["attn_gain"],
                      p["wo_a"], p["wo_g"], p["ffn_g"])
    rank, e2, cnt, w1 = _peer_route(hn, p["wqt"], p["keys"])
    y = _peer_dense(hn, h, p["u"], p["vt"], rank, e2, cnt, w1, p["final_g"])
    return y.reshape(b, t, d)


def kernel(x_prompt, x_sample, norm_mix_g, w_in, q_norm_g, k_norm_g, w_gate_f2, b_gate_f,
           w_gate_b2, b_gate_b, gla_norm_g, attn_out_norm_g, w_out, norm_ffn_g, peer_w_q,
           peer_sub_keys, peer_u, peer_v, final_norm_g):
    p = _prepare(norm_mix_g, w_in, q_norm_g, k_norm_g, w_gate_f2, b_gate_f, w_gate_b2,
                 b_gate_b, gla_norm_g, attn_out_norm_g, w_out, norm_ffn_g, peer_w_q,
                 peer_sub_keys, peer_u, peer_v, final_norm_g)
    return (_layer(x_prompt, p), _layer(x_sample, p))
```

```python
import functools
import math

import jax
import jax.numpy as jnp
import numpy as np
from jax import lax
from jax.experimental import pallas as pl
from jax.experimental.pallas import tpu as pltpu

F32 = jnp.float32
BF16 = jnp.bfloat16

D_MODEL = 2048
GRID_W = 64
HEAD_DIM = 128
N_HEADS = 8
N_KV_HEADS = 2
GQA_GROUP = N_HEADS // N_KV_HEADS
ATTN_WIDTH = N_HEADS * HEAD_DIM
KV_WIDTH = N_KV_HEADS * HEAD_DIM
ROPE_THETA = 10000.0
AXIS_DIM = HEAD_DIM // 2
GLA_HEADS = 4
GLA_DK = 128
GLA_DV = 256
GLA_WIDTH = GLA_HEADS * GLA_DV
GATE_RANK = 16
GATE_NORMALIZER = 16.0
PEER_HEADS = 8
N_KEYS = 128
PEER_TOPK = 16
PEER_HALF = 128
PEER_QDIM = 2 * PEER_HALF
EPS = 1e-6

LANES = 128
SUBLANES = 8
BF16_ROWS = 16
MXU_DIM = 256
VMEM_LIMIT = 56 * 1024 * 1024

PROJ_TM = 512
ATTN_TQ = 256
GLA_CHUNK = 64
GLA_UNROLL = 8
ROUTE_TB = 512
PEER_TB = 512
PEER_TE = 1024

PEER_KEY2_ROWS = np.concatenate([np.arange(0, N_KEYS, 2), np.arange(1, N_KEYS, 2)])

NT_DIMS = (((1,), (1,)), ((), ()))
TN_DIMS = (((0,), (0,)), ((), ()))


def _params(sem, vmem=VMEM_LIMIT, flags=None):
    return pltpu.CompilerParams(dimension_semantics=sem, vmem_limit_bytes=vmem, flags=flags)


def _rms(x, g):
    return x * lax.rsqrt(jnp.mean(x * x, axis=-1, keepdims=True) + EPS) * g


def _qkv_kernel(x_ref, g_ref, w_ref, qg_ref, kg_ref, cos_ref, sin_ref, q_ref, k_ref, v_ref):
    xn = _rms(x_ref[...], g_ref[...]).astype(BF16)
    z = jnp.dot(xn, w_ref[...], preferred_element_type=F32)
    c = cos_ref[...]
    s = sin_ref[...]

    def norm_rope(hd, gain, scale):
        hn = _rms(hd, gain)
        rot = pltpu.roll(hn, HEAD_DIM // 2, axis=1)
        return ((hn * c + rot * s) * scale).astype(BF16)

    for hh in range(N_HEADS):
        sl = slice(hh * HEAD_DIM, (hh + 1) * HEAD_DIM)
        q_ref[:, sl] = norm_rope(z[:, sl], qg_ref[...], HEAD_DIM ** -0.5)
    for hh in range(N_KV_HEADS):
        sl = slice(hh * HEAD_DIM, (hh + 1) * HEAD_DIM)
        zsl = slice(ATTN_WIDTH + hh * HEAD_DIM, ATTN_WIDTH + (hh + 1) * HEAD_DIM)
        k_ref[:, sl] = norm_rope(z[:, zsl], kg_ref[...], 1.0)
    v_ref[...] = z[:, ATTN_WIDTH + KV_WIDTH:].astype(BF16)


def _qkv_proj(x2, g, w, qg, kg, cos, sin, seq):
    n, d = x2.shape
    tm = min(PROJ_TM, seq)
    nt = seq // tm
    wq = w.shape[1]
    full = lambda i: (0, 0)
    return pl.pallas_call(
        _qkv_kernel,
        grid=(n // tm,),
        in_specs=[
            pl.BlockSpec((tm, d), lambda i: (i, 0)),
            pl.BlockSpec((1, d), full),
            pl.BlockSpec((d, wq), full),
            pl.BlockSpec((1, HEAD_DIM), full),
            pl.BlockSpec((1, HEAD_DIM), full),
            pl.BlockSpec((tm, HEAD_DIM), lambda i: (i % nt, 0)),
            pl.BlockSpec((tm, HEAD_DIM), lambda i: (i % nt, 0)),
        ],
        out_specs=[
            pl.BlockSpec((tm, ATTN_WIDTH), lambda i: (i, 0)),
            pl.BlockSpec((tm, KV_WIDTH), lambda i: (i, 0)),
            pl.BlockSpec((tm, KV_WIDTH), lambda i: (i, 0)),
        ],
        out_shape=[
            jax.ShapeDtypeStruct((n, ATTN_WIDTH), BF16),
            jax.ShapeDtypeStruct((n, KV_WIDTH), BF16),
            jax.ShapeDtypeStruct((n, KV_WIDTH), BF16),
        ],
        compiler_params=_params(("parallel",)),
        name="qkv_proj",
    )(x2, g, w, qg, kg, cos, sin)


def _log_sigmoid(x):
    return jnp.minimum(x, 0.0) - jnp.log(1.0 + jnp.exp(-jnp.abs(x)))


def _gla_proj_kernel(x_ref, g_ref, w_ref, wg_ref, bg_ref,
                     gq_ref, gk_ref, gv_ref, sg_ref, laf_ref, lab_ref):
    xn = _rms(x_ref[...], g_ref[...]).astype(BF16)
    z = jnp.dot(xn, w_ref[...], preferred_element_type=F32)
    kw = GLA_HEADS * GLA_DK
    gq_ref[...] = (z[:, :kw] * (GLA_DK ** -0.5)).astype(BF16)
    gk_ref[...] = z[:, kw:2 * kw].astype(BF16)
    gv_ref[...] = z[:, 2 * kw:2 * kw + GLA_WIDTH].astype(BF16)
    gr = z[:, 2 * kw + GLA_WIDTH:2 * kw + 2 * GLA_WIDTH]
    sg_ref[...] = (gr * (1.0 / (1.0 + jnp.exp(-gr)))).astype(BF16)
    lr = z[:, 2 * kw + 2 * GLA_WIDTH:]
    pre = jnp.dot(lr, wg_ref[...], preferred_element_type=F32,
                  precision=lax.Precision.HIGHEST) + bg_ref[...]
    la = _log_sigmoid(pre) * (1.0 / GATE_NORMALIZER)
    laf_ref[...] = la[:, :kw]
    lab_ref[...] = la[:, kw:]


def _gla_proj(x2, g, w, wg, bg):
    n, d = x2.shape
    tm = PROJ_TM
    kw = GLA_HEADS * GLA_DK
    full = lambda i: (0, 0)
    row = lambda i: (i, 0)
    return pl.pallas_call(
        _gla_proj_kernel,
        grid=(n // tm,),
        in_specs=[
            pl.BlockSpec((tm, d), row),
            pl.BlockSpec((1, d), full),
            pl.BlockSpec(w.shape, full),
            pl.BlockSpec(wg.shape, full),
            pl.BlockSpec(bg.shape, full),
        ],
        out_specs=[
            pl.BlockSpec((tm, kw), row),
            pl.BlockSpec((tm, kw), row),
            pl.BlockSpec((tm, GLA_WIDTH), row),
            pl.BlockSpec((tm, GLA_WIDTH), row),
            pl.BlockSpec((tm, kw), row),
            pl.BlockSpec((tm, kw), row),
        ],
        out_shape=[
            jax.ShapeDtypeStruct((n, kw), BF16),
            jax.ShapeDtypeStruct((n, kw), BF16),
            jax.ShapeDtypeStruct((n, GLA_WIDTH), BF16),
            jax.ShapeDtypeStruct((n, GLA_WIDTH), BF16),
            jax.ShapeDtypeStruct((n, kw), F32),
            jax.ShapeDtypeStruct((n, kw), F32),
        ],
        compiler_params=_params(("parallel",)),
        name="gla_proj",
    )(x2, g, w, wg, bg)


def _attn_kernel(q_ref, k_ref, v_ref, o_ref):
    k = k_ref[0]
    v = v_ref[0]

    def scores(hh):
        sl = slice(hh * HEAD_DIM, (hh + 1) * HEAD_DIM)
        return lax.dot_general(q_ref[0, :, sl], k, NT_DIMS, preferred_element_type=F32)

    s = scores(0)
    for hh in range(GQA_GROUP):
        s_next = scores(hh + 1) if hh + 1 < GQA_GROUP else None
        m = jnp.max(s, axis=-1, keepdims=True)
        p = jnp.exp(s - m)
        l = jnp.sum(p, axis=-1, keepdims=True)
        o = jnp.dot(p.astype(BF16), v, preferred_element_type=F32)
        o_ref[0, :, hh * HEAD_DIM:(hh + 1) * HEAD_DIM] = (o / l).astype(BF16)
        s = s_next


def _attention(q, k, v):
    b, t, _ = q.shape
    tq = min(ATTN_TQ, t)
    gw = GQA_GROUP * HEAD_DIM
    return pl.pallas_call(
        _attn_kernel,
        grid=(b, N_KV_HEADS, t // tq),
        in_specs=[
            pl.BlockSpec((1, tq, gw), lambda bi, g, qi: (bi, qi, g)),
            pl.BlockSpec((1, t, HEAD_DIM), lambda bi, g, qi: (bi, 0, g)),
            pl.BlockSpec((1, t, HEAD_DIM), lambda bi, g, qi: (bi, 0, g)),
        ],
        out_specs=pl.BlockSpec((1, tq, gw), lambda bi, g, qi: (bi, qi, g)),
        out_shape=jax.ShapeDtypeStruct((b, t, ATTN_WIDTH), BF16),
        compiler_params=_params(("parallel", "parallel", "arbitrary")),
        name="gqa_attention",
    )(q, k, v)


def _split_dot(tri, g):
    hi = g.astype(BF16)
    lo = (g - hi.astype(F32)).astype(BF16)
    return (jnp.dot(tri, hi, preferred_element_type=F32)
            + jnp.dot(tri, lo, preferred_element_type=F32))


def _gla_kernel(q_ref, k_ref, v_ref, sg_ref, gf_ref, gb_ref, gain_ref, o_ref, acc_ref,
                sf_ref, sb_ref):
    t = q_ref.shape[1]
    c = GLA_CHUNK
    n = t // c
    mid = c // 2
    ri = lax.broadcasted_iota(jnp.int32, (c, c), 0)
    ci = lax.broadcasted_iota(jnp.int32, (c, c), 1)
    lower = ri >= ci
    tri_lo = lower.astype(BF16)
    tri_up = (ri <= ci).astype(BF16)
    sf_ref[...] = jnp.zeros_like(sf_ref)
    sb_ref[...] = jnp.zeros_like(sb_ref)

    upper = ri <= ci
    un = min(GLA_UNROLL, n // 2)

    def finish(rows, o):
        y = _rms(o, gain_ref[...]) * sg_ref[0, rows, :].astype(F32)
        o_ref[0, rows, :] = y.astype(BF16)

    def trip(i, second_half):
        items = []
        for u in range(un):
            idx = i * un + u
            items.append((idx, sf_ref, gf_ref, tri_lo, c - 1, lower))
            items.append((n - 1 - idx, sb_ref, gb_ref, tri_up, 0, upper))
        rows = [pl.ds(pl.multiple_of(it[0] * c, c), c) for it in items]
        bs = [_split_dot(it[3], it[2][0, r, :]) for it, r in zip(items, rows)]
        qs, ks, scores = [], [], []
        for it, r, b in zip(items, rows, bs):
            q = q_ref[0, r, :].astype(F32)
            k = k_ref[0, r, :].astype(F32)
            ref = b[mid:mid + 1, :]
            qt = (q * jnp.exp(b - ref)).astype(BF16)
            kt = (k * jnp.exp(ref - b)).astype(BF16)
            a = lax.dot_general(qt, kt, NT_DIMS, preferred_element_type=F32)
            qs.append(q)
            ks.append(k)
            scores.append(jnp.where(it[5], a, 0.0).astype(BF16))
        intra, updates, tots = [], [], []
        for it, r, b, k, a in zip(items, rows, bs, ks, scores):
            v = v_ref[0, r, :]
            tot = b[it[4]:it[4] + 1, :]
            kd = (k * jnp.exp(tot - b)).astype(BF16)
            intra.append(jnp.dot(a, v, preferred_element_type=F32))
            updates.append(lax.dot_general(v, kd, TN_DIMS, preferred_element_type=F32))
            tots.append(tot)
        for it, r, b, q, o_in, upd, tot in zip(items, rows, bs, qs, intra, updates, tots):
            s_ref = it[1]
            st = s_ref[...]
            o = o_in + lax.dot_general((q * jnp.exp(b)).astype(BF16), st.astype(BF16), NT_DIMS,
                                       preferred_element_type=F32)
            s_ref[...] = st * jnp.exp(tot) + upd
            if second_half:
                finish(r, acc_ref[r, :] + o)
            else:
                acc_ref[r, :] = o

    half_trips = n // 2 // un

    def first_half(i, carry):
        trip(i, False)
        return carry

    def second_half(i, carry):
        trip(i, True)
        return carry

    lax.fori_loop(0, half_trips, first_half, 0)
    lax.fori_loop(half_trips, 2 * half_trips, second_half, 0)


def _gla(gq, gk, gv, sg, laf, lab, gain):
    b, t, _ = gq.shape
    kspec = pl.BlockSpec((1, t, GLA_DK), lambda bi, h: (bi, 0, h))
    vspec = pl.BlockSpec((1, t, GLA_DV), lambda bi, h: (bi, 0, h))
    return pl.pallas_call(
        _gla_kernel,
        grid=(b, GLA_HEADS),
        in_specs=[kspec, kspec, vspec, vspec, kspec, kspec,
                  pl.BlockSpec((1, GLA_DV), lambda bi, h: (0, 0))],
        out_specs=vspec,
        out_shape=jax.ShapeDtypeStruct((b, t, GLA_WIDTH), BF16),
        scratch_shapes=[
            pltpu.VMEM((t, GLA_DV), F32),
            pltpu.VMEM((GLA_DV, GLA_DK), F32),
            pltpu.VMEM((GLA_DV, GLA_DK), F32),
        ],
        compiler_params=_params(("parallel", "parallel")),
        name="gla_bidir",
    )(gq, gk, gv, sg, laf, lab, gain)


def _out_proj_kernel(x_ref, oa_ref, og_ref, ag_ref, wa_ref, wg_ref, fg_ref, h_ref, hn_ref):
    oa = _rms(oa_ref[...].astype(F32), ag_ref[...]).astype(BF16)
    h = (x_ref[...]
         + jnp.dot(oa, wa_ref[...], preferred_element_type=F32)
         + jnp.dot(og_ref[...], wg_ref[...], preferred_element_type=F32))
    h_ref[...] = h
    hn_ref[...] = _rms(h, fg_ref[...]).astype(BF16)


def _out_proj(x2, oa, og, ag, wa, wg, fg):
    n, d = x2.shape
    tm = PROJ_TM
    full = lambda i: (0, 0)
    row = lambda i: (i, 0)
    return pl.pallas_call(
        _out_proj_kernel,
        grid=(n // tm,),
        in_specs=[
            pl.BlockSpec((tm, d), row),
            pl.BlockSpec((tm, ATTN_WIDTH), row),
            pl.BlockSpec((tm, GLA_WIDTH), row),
            pl.BlockSpec((1, ATTN_WIDTH), full),
            pl.BlockSpec(wa.shape, full),
            pl.BlockSpec(wg.shape, full),
            pl.BlockSpec((1, d), full),
        ],
        out_specs=[pl.BlockSpec((tm, d), row), pl.BlockSpec((tm, d), row)],
        out_shape=[jax.ShapeDtypeStruct((n, d), F32), jax.ShapeDtypeStruct((n, d), BF16)],
        compiler_params=_params(("parallel",)),
        name="out_proj",
    )(x2, oa, og, ag, wa, wg, fg)


def _cmp_exchange(rows, i, j):
    a, b = rows[i], rows[j]
    if b is None:
        return
    if a is None:
        rows[i], rows[j] = b, None
        return
    rows[i], rows[j] = jnp.maximum(a, b), jnp.minimum(a, b)


def _bitonic_merge_desc(rows):
    n = len(rows)
    j = n // 2
    while j >= 1:
        for i in range(n):
            l = i ^ j
            if l > i:
                _cmp_exchange(rows, i, l)
        j //= 2


def _bitonic_sort_desc(rows):
    n = len(rows)
    k = 2
    while k <= n:
        j = k // 2
        while j >= 1:
            for i in range(n):
                l = i ^ j
                if l > i:
                    if (i & k) == 0:
                        _cmp_exchange(rows, i, l)
                    else:
                        _cmp_exchange(rows, l, i)
            j //= 2
        k *= 2


def _top_of_union(a, b):
    n = len(a)
    out = []
    for i in range(n):
        x, y = a[i], b[n - 1 - i]
        out.append(x if y is None else (y if x is None else jnp.maximum(x, y)))
    _bitonic_merge_desc(out)
    return out


def _top16_rows(s):
    l = s.shape[1]
    s3 = s.reshape(N_KEYS // SUBLANES, SUBLANES, l)
    rows = [s3[i] for i in range(N_KEYS // SUBLANES)]
    _bitonic_sort_desc(rows)
    for shift in (4, 2, 1):
        other = [pltpu.roll(r, shift, axis=0) for r in rows]
        rows = _top_of_union(rows, other)
    return rows


_CANDIDATES = [(i, j) for i in range(PEER_TOPK) for j in range(PEER_TOPK)
               if (i + 1) * (j + 1) <= PEER_TOPK]


def _bf16_bits(x):
    bits = lax.bitcast_convert_type(x, jnp.uint32)
    return (bits + jnp.uint32(0x7FFF) + ((bits >> 16) & jnp.uint32(1))) >> 16


def _bf16_pair_words(x):
    hi = _bf16_bits(x)
    return lax.bitcast_convert_type((hi << 16) | hi, jnp.int32)


def _bf16_packed_rows(x):
    hi = _bf16_bits(x)
    half = x.shape[0] // 2
    return lax.bitcast_convert_type(hi[:half] | (hi[half:] << 16), jnp.int32)


def _route_kernel(hn_ref, wqt_ref, keys_ref, rank_ref, e2_ref, cnt_ref, w1_ref, qt_ref):
    hh = pl.program_id(1)

    @pl.when(hh == 0)
    def _():
        qt_ref[...] = lax.dot_general(wqt_ref[...], hn_ref[...], NT_DIMS,
                                      preferred_element_type=F32).astype(BF16)

    base = pl.multiple_of(hh * PEER_QDIM, PEER_QDIM)
    q1 = qt_ref[pl.ds(base, PEER_HALF), :]
    q2 = qt_ref[pl.ds(base + PEER_HALF, PEER_HALF), :]
    s1 = jnp.dot(keys_ref[0, 0], q1, preferred_element_type=F32)
    s2 = jnp.dot(keys_ref[0, 1], q2, preferred_element_type=F32)
    tb = s1.shape[1]
    lgs = tb // LANES
    a = _top16_rows(s1)
    b = _top16_rows(s2)

    sub = lax.broadcasted_iota(jnp.int32, (SUBLANES, LANES), 0)

    def compact(r):
        out = r[:, :LANES]
        for g in range(1, lgs):
            out = jnp.where(sub == g, r[:, g * LANES:(g + 1) * LANES], out)
        return out

    def expand(c):
        return jnp.concatenate(
            [jnp.broadcast_to(c[g:g + 1, :], (SUBLANES, LANES)) for g in range(lgs)], axis=1)

    ac = [compact(r) for r in a]
    bc = [compact(r) for r in b]
    cand = [ac[i] + bc[j] for (i, j) in _CANDIDATES]
    cand += [None] * (64 - len(cand))
    groups = []
    for gi in range(4):
        grp = cand[gi * 16:(gi + 1) * 16]
        _bitonic_sort_desc(grp)
        groups.append(grp)
    top = _top_of_union(_top_of_union(groups[0], groups[1]),
                        _top_of_union(groups[2], groups[3]))
    z = jnp.zeros_like(top[0])
    for tk in top:
        z = z + jnp.exp(tk - top[0])
    tau = expand(top[PEER_TOPK - 1])
    inv_z = expand(1.0 / z)
    full = lambda r: jnp.broadcast_to(r[0:1, :], (N_KEYS, tb))
    tau_f = full(tau)
    cnt_bits = jnp.zeros((N_KEYS, tb), jnp.int32)
    rank_bits = jnp.zeros((N_KEYS, tb), jnp.int32)
    for j in range(PEER_TOPK):
        bj = full(b[j])
        bits = int(np.asarray(j + 1, dtype=np.float32).view(np.uint32)) >> 16
        cnt_bits = jnp.where(s1 + bj >= tau_f, bits, cnt_bits)
        rank_bits = jnp.where(bj > s2, bits, rank_bits)
    e2 = jnp.exp(s2 - full(b[0]))
    w1 = jnp.exp(s1 - full(a[0])) * full(inv_z)
    half = N_KEYS // 2
    rank_w = rank_bits[:half] | (rank_bits[half:] << 16)
    e2_w = _bf16_packed_rows(e2)
    cnt_w = cnt_bits | (cnt_bits << 16)
    w1_w = _bf16_pair_words(w1)
    for lg in range(tb // LANES):
        lanes = slice(lg * LANES, (lg + 1) * LANES)
        rank_ref[lg, 0] = rank_w[:, lanes]
        e2_ref[lg, 0] = e2_w[:, lanes]
        cnt_ref[lg, 0] = cnt_w[:, lanes]
        w1_ref[lg, 0] = w1_w[:, lanes]


def _peer_route(hn, wqt, keys):
    n, d = hn.shape
    tb = min(ROUTE_TB, n)
    lgs = tb // LANES
    tab = jax.ShapeDtypeStruct((n // LANES, PEER_HEADS, N_KEYS, LANES), jnp.int32)
    tab16 = jax.ShapeDtypeStruct((n // LANES, PEER_HEADS, N_KEYS // 2, LANES), jnp.int32)
    tspec = pl.BlockSpec((lgs, 1, N_KEYS, LANES), lambda i, h: (i, h, 0, 0))
    tspec16 = pl.BlockSpec((lgs, 1, N_KEYS // 2, LANES), lambda i, h: (i, h, 0, 0))
    return pl.pallas_call(
        _route_kernel,
        grid=(n // tb, PEER_HEADS),
        in_specs=[
            pl.BlockSpec((tb, d), lambda i, h: (i, 0)),
            pl.BlockSpec(wqt.shape, lambda i, h: (0, 0), pipeline_mode=pl.Buffered(1)),
            pl.BlockSpec((1, 2, N_KEYS, PEER_HALF), lambda i, h: (h, 0, 0, 0)),
        ],
        out_specs=[tspec16, tspec16, tspec, tspec],
        out_shape=[tab16, tab16, tab, tab],
        scratch_shapes=[pltpu.VMEM((PEER_HEADS * PEER_QDIM, tb), BF16)],
        compiler_params=_params(("parallel", "arbitrary")),
        name="peer_route",
    )(hn, wqt, keys)


def _gelu(x):
    return 0.5 * x * (1.0 + lax.erf(x * (2.0 ** -0.5)))


def _peer_kernel(hn_ref, h_ref, u0_ref, u_ref, vt_ref, rank_ref, e2_ref, cnt_ref, w1_ref, fg_ref,
                 y_ref, acc_ref, hta_ref, htb_ref, w_ref):
    j = pl.program_id(1)
    nj = pl.num_programs(1)
    lane_groups, te, _ = hta_ref.shape
    rows_per_step = te // N_KEYS

    def score(u_tile_ref, dst_ref, hn=None):
        hn = hn_ref[...] if hn is None else hn
        ht = lax.dot_general(u_tile_ref[...], hn, NT_DIMS,
                             preferred_element_type=F32)
        for lg in range(lane_groups):
            dst_ref[lg] = ht[:, lg * LANES:(lg + 1) * LANES]

    def released_by(gates):
        hn = hn_ref[...]
        tb, d = hn.shape
        n_tiles, k_tiles = tb // MXU_DIM, d // MXU_DIM
        groups = MXU_DIM // BF16_ROWS
        assert len(gates) == n_tiles * k_tiles * groups
        rows = []
        for nt in range(n_tiles):
            for rg in range(groups):
                r0 = nt * MXU_DIM + rg * BF16_ROWS
                cols = []
                for kt in range(k_tiles):
                    g = gates[(nt * k_tiles + kt) * groups + rg]
                    z = g * jnp.zeros_like(g)
                    z = jnp.concatenate([z] * (MXU_DIM // LANES), axis=1)
                    cols.append(hn[r0:r0 + BF16_ROWS, kt * MXU_DIM:(kt + 1) * MXU_DIM] + z)
                rows.append(jnp.concatenate(cols, axis=1))
        return jnp.concatenate(rows, axis=0)

    @pl.when(j == 0)
    def _():
        acc_ref[...] = jnp.zeros_like(acc_ref)

    @pl.when((pl.program_id(0) == 0) & (j == 0))
    def _():
        score(u0_ref, hta_ref)

    def step(cur_ref, nxt_ref):
        zero = jnp.zeros((BF16_ROWS, LANES), BF16)
        gates = []

        def row_tile(ref, lg, hh, rr):
            words = jnp.broadcast_to(ref[lg, hh, rr:rr + 1, :], (SUBLANES, LANES))
            return pltpu.bitcast(words, BF16)

        for lg in range(lane_groups):
            lanes = slice(lg * LANES, (lg + 1) * LANES)
            for rr in range(rows_per_step):
                cnt = [row_tile(cnt_ref, lg, hh, rr) for hh in range(PEER_HEADS)]
                w1 = [row_tile(w1_ref, lg, hh, rr) for hh in range(PEER_HEADS)]
                for kk in range(N_KEYS // BF16_ROWS):
                    words = slice(kk * SUBLANES, (kk + 1) * SUBLANES)
                    rows = slice(rr * N_KEYS + kk * BF16_ROWS, rr * N_KEYS + (kk + 1) * BF16_ROWS)
                    gate = zero
                    for hh in range(PEER_HEADS):
                        rank = pltpu.bitcast(rank_ref[lg, hh, words, :], BF16)
                        e2 = pltpu.bitcast(e2_ref[lg, hh, words, :], BF16)
                        gate = gate + jnp.where(rank < cnt[hh], e2, zero) * w1[hh]
                    gates.append(gate)
                    w_ref[rows, lanes] = gate * _gelu(cur_ref[lg, rows, :]).astype(BF16)
        score(u_ref, nxt_ref, released_by(gates))
        acc_ref[...] += jnp.dot(vt_ref[...], w_ref[...], preferred_element_type=F32)

    @pl.when(j % 2 == 0)
    def _():
        step(hta_ref, htb_ref)

    @pl.when(j % 2 == 1)
    def _():
        step(htb_ref, hta_ref)

    @pl.when(j == nj - 1)
    def _():
        h = h_ref[...] + acc_ref[...].T
        y_ref[...] = _rms(h, fg_ref[...])


def _peer_dense(hn, h, u, vt, rank, e2, cnt, w1, fg):
    n, d = hn.shape
    e = u.shape[0]
    tb, te = PEER_TB, PEER_TE
    nj = e // te
    ni = n // tb
    assert nj % 2 == 0, "score buffers alternate with the step parity across token blocks"
    lgs = tb // LANES
    rows = te // N_KEYS
    once = pl.Buffered(1)
    tspec = pl.BlockSpec((lgs, PEER_HEADS, N_KEYS // 2, LANES), lambda i, j: (i, 0, 0, 0),
                         pipeline_mode=once)
    rspec = pl.BlockSpec((lgs, PEER_HEADS, rows, LANES), lambda i, j: (i, 0, j, 0))
    return pl.pallas_call(
        _peer_kernel,
        grid=(ni, nj),
        in_specs=[
            pl.BlockSpec((tb, d), lambda i, j: (jnp.minimum(i + (j + 1) // nj, ni - 1), 0)),
            pl.BlockSpec((tb, d), lambda i, j: (i, 0), pipeline_mode=once),
            pl.BlockSpec((te, d), lambda i, j: (0, 0), pipeline_mode=once),
            pl.BlockSpec((te, d), lambda i, j: ((j + 1) % nj, 0)),
            pl.BlockSpec((None, d, te), lambda i, j: (j, 0, 0)),
            tspec, tspec, rspec, rspec,
            pl.BlockSpec((1, d), lambda i, j: (0, 0)),
        ],
        out_specs=pl.BlockSpec((tb, d), lambda i, j: (i, 0)),
        out_shape=jax.ShapeDtypeStruct((n, d), F32),
        scratch_shapes=[
            pltpu.VMEM((d, tb), F32),
            pltpu.VMEM((lgs, te, LANES), F32),
            pltpu.VMEM((lgs, te, LANES), F32),
            pltpu.VMEM((te, tb), BF16),
        ],
        compiler_params=_params(("arbitrary", "arbitrary")),
        name="peer_dense",
    )(hn, h, u, u, vt, rank, e2, cnt, w1, fg)


def _rope_tables(t):
    n_rows = t // GRID_W
    row = jnp.repeat(jnp.arange(n_rows, dtype=F32), GRID_W)
    col = jnp.tile(jnp.arange(GRID_W, dtype=F32), n_rows)
    freqs = ROPE_THETA ** (-jnp.arange(0, AXIS_DIM, 2, dtype=F32) / AXIS_DIM)
    ang = jnp.concatenate([row[:, None] * freqs, col[:, None] * freqs], axis=-1)
    cos, sin = jnp.cos(ang), jnp.sin(ang)
    return jnp.concatenate([cos, cos], axis=-1), jnp.concatenate([-sin, sin], axis=-1)


def _prepare(norm_mix_g, w_in, q_norm_g, k_norm_g, w_gate_f2, b_gate_f, w_gate_b2, b_gate_b,
             gla_norm_g, attn_out_norm_g, w_out, norm_ffn_g, peer_w_q, peer_sub_keys, peer_u,
             peer_v, final_norm_g):
    perm = np.concatenate([np.arange(0, HEAD_DIM, 2), np.arange(1, HEAD_DIM, 2)])
    kw = GLA_HEADS * GLA_DK
    o_k = ATTN_WIDTH
    o_v = o_k + KV_WIDTH
    o_g = o_v + KV_WIDTH
    o_lr = o_g + 2 * kw + 2 * GLA_WIDTH
    wq = w_in[:, :o_k].reshape(D_MODEL, N_HEADS, HEAD_DIM)[:, :, perm].reshape(D_MODEL, o_k)
    wk = w_in[:, o_k:o_v].reshape(D_MODEL, N_KV_HEADS, HEAD_DIM)[:, :, perm].reshape(D_MODEL, KV_WIDTH)
    w_qkv = jnp.concatenate([wq, wk, w_in[:, o_v:o_g]], axis=1).astype(BF16)
    w_lr = jnp.pad(w_in[:, o_lr:], ((0, 0), (0, LANES - 2 * GATE_RANK)))
    w_gla = jnp.concatenate([w_in[:, o_g:o_lr], w_lr], axis=1).astype(BF16)
    wg = jnp.zeros((LANES, 2 * kw), F32)
    wg = wg.at[:GATE_RANK, :kw].set(w_gate_f2).at[GATE_RANK:2 * GATE_RANK, kw:].set(w_gate_b2)
    bg = jnp.concatenate([b_gate_f, b_gate_b])[None, :]

    keys = jnp.stack([peer_sub_keys[:, 0], peer_sub_keys[:, 1][:, PEER_KEY2_ROWS]], axis=1)
    return dict(
        norm_mix_g=norm_mix_g[None, :], w_qkv=w_qkv, w_gla=w_gla, wg=wg, bg=bg,
        qg=q_norm_g[perm][None, :], kg=k_norm_g[perm][None, :],
        gla_gain=gla_norm_g[None, :], attn_gain=attn_out_norm_g[None, :],
        wo_a=w_out[:ATTN_WIDTH].astype(BF16), wo_g=w_out[ATTN_WIDTH:].astype(BF16),
        ffn_g=norm_ffn_g[None, :], wqt=peer_w_q.T.astype(BF16),
        keys=keys.astype(BF16), u=peer_u.astype(BF16),
        vt=peer_v.astype(BF16).reshape(-1, PEER_TE, D_MODEL).transpose(0, 2, 1),
        final_g=final_norm_g[None, :],
    )


def _layer(x, p):
    b, t, d = x.shape
    n = b * t
    x2 = x.reshape(n, d)
    cos, sin = _rope_tables(t)
    q, k, v = _qkv_proj(x2, p["norm_mix_g"], p["w_qkv"], p["qg"], p["kg"], cos, sin, t)
    gq, gk, gv, sg, laf, lab = _gla_proj(x2, p["norm_mix_g"], p["w_gla"], p["wg"], p["bg"])
    oa = _attention(q.reshape(b, t, -1), k.reshape(b, t, -1), v.reshape(b, t, -1))
    r3 = lambda a: a.reshape(b, t, -1)
    og = _gla(r3(gq), r3(gk), r3(gv), r3(sg), r3(laf), r3(lab), p["gla_gain"])
    h, hn = _out_proj(x2, oa.reshape(n, -1), og.reshape(n, -1), p["attn_gain"],
                      p["wo_a"], p["wo_g"], p["ffn_g"])
    rank, e2, cnt, w1 = _peer_route(hn, p["wqt"], p["keys"])
    y = _peer_dense(hn, h, p["u"], p["vt"], rank, e2, cnt, w1, p["final_g"])
    return y.reshape(b, t, d)


def kernel(x_prompt, x_sample, norm_mix_g, w_in, q_norm_g, k_norm_g, w_gate_f2, b_gate_f,
           w_gate_b2, b_gate_b, gla_norm_g, attn_out_norm_g, w_out, norm_ffn_g, peer_w_q,
           peer_sub_keys, peer_u, peer_v, final_norm_g):
    p = _prepare(norm_mix_g, w_in, q_norm_g, k_norm_g, w_gate_f2, b_gate_f, w_gate_b2,
                 b_gate_b, gla_norm_g, attn_out_norm_g, w_out, norm_ffn_g, peer_w_q,
                 peer_sub_keys, peer_u, peer_v, final_norm_g)
    return (_layer(x_prompt, p), _layer(x_sample, p))
```

```python
import functools
import math

import jax
import jax.numpy as jnp
import numpy as np
from jax import lax
from jax.experimental import pallas as pl
from jax.experimental.pallas import tpu as pltpu

F32 = jnp.float32
BF16 = jnp.bfloat16

D_MODEL = 2048
GRID_W = 64
HEAD_DIM = 128
N_HEADS = 8
N_KV_HEADS = 2
GQA_GROUP = N_HEADS // N_KV_HEADS
ATTN_WIDTH = N_HEADS * HEAD_DIM
KV_WIDTH = N_KV_HEADS * HEAD_DIM
ROPE_THETA = 10000.0
AXIS_DIM = HEAD_DIM // 2
GLA_HEADS = 4
GLA_DK = 128
GLA_DV = 256
GLA_WIDTH = GLA_HEADS * GLA_DV
GATE_RANK = 16
GATE_NORMALIZER = 16.0
PEER_HEADS = 8
N_KEYS = 128
PEER_TOPK = 16
PEER_HALF = 128
PEER_QDIM = 2 * PEER_HALF
EPS = 1e-6

LANES = 128
SUBLANES = 8
BF16_ROWS = 16
MXU_DIM = 256
VMEM_LIMIT = 56 * 1024 * 1024

PROJ_TM = 512
PROJ_COLS = 512
ATTN_TQ = 256
GLA_CHUNK = 64
GLA_UNROLL = 16
ROUTE_TB = 512
PEER_TB = 512
PEER_TE = 1024

PEER_KEY2_ROWS = np.concatenate([np.arange(0, N_KEYS, 2), np.arange(1, N_KEYS, 2)])

NT_DIMS = (((1,), (1,)), ((), ()))
TN_DIMS = (((0,), (0,)), ((), ()))


def _params(sem, vmem=VMEM_LIMIT, flags=None):
    return pltpu.CompilerParams(dimension_semantics=sem, vmem_limit_bytes=vmem, flags=flags)


def _rms(x, g):
    return x * lax.rsqrt(jnp.mean(x * x, axis=-1, keepdims=True) + EPS) * g


def _qkv_kernel(x_ref, g_ref, w_ref, qg_ref, kg_ref, cos_ref, sin_ref, q_ref, k_ref, v_ref):
    xn = _rms(x_ref[...], g_ref[...]).astype(BF16)
    z = jnp.dot(xn, w_ref[...], preferred_element_type=F32)
    c = cos_ref[...]
    s = sin_ref[...]

    def norm_rope(hd, gain, scale):
        hn = _rms(hd, gain)
        rot = pltpu.roll(hn, HEAD_DIM // 2, axis=1)
        return ((hn * c + rot * s) * scale).astype(BF16)

    for hh in range(N_HEADS):
        sl = slice(hh * HEAD_DIM, (hh + 1) * HEAD_DIM)
        q_ref[:, sl] = norm_rope(z[:, sl], qg_ref[...], HEAD_DIM ** -0.5)
    for hh in range(N_KV_HEADS):
        sl = slice(hh * HEAD_DIM, (hh + 1) * HEAD_DIM)
        zsl = slice(ATTN_WIDTH + hh * HEAD_DIM, ATTN_WIDTH + (hh + 1) * HEAD_DIM)
        k_ref[:, sl] = norm_rope(z[:, zsl], kg_ref[...], 1.0)
    v_ref[...] = z[:, ATTN_WIDTH + KV_WIDTH:].astype(BF16)


def _qkv_proj(x2, g, w, qg, kg, cos, sin, seq):
    n, d = x2.shape
    tm = min(PROJ_TM, seq)
    nt = seq // tm
    wq = w.shape[1]
    full = lambda i: (0, 0)
    return pl.pallas_call(
        _qkv_kernel,
        grid=(n // tm,),
        in_specs=[
            pl.BlockSpec((tm, d), lambda i: (i, 0)),
            pl.BlockSpec((1, d), full),
            pl.BlockSpec((d, wq), full),
            pl.BlockSpec((1, HEAD_DIM), full),
            pl.BlockSpec((1, HEAD_DIM), full),
            pl.BlockSpec((tm, HEAD_DIM), lambda i: (i % nt, 0)),
            pl.BlockSpec((tm, HEAD_DIM), lambda i: (i % nt, 0)),
        ],
        out_specs=[
            pl.BlockSpec((tm, ATTN_WIDTH), lambda i: (i, 0)),
            pl.BlockSpec((tm, KV_WIDTH), lambda i: (i, 0)),
            pl.BlockSpec((tm, KV_WIDTH), lambda i: (i, 0)),
        ],
        out_shape=[
            jax.ShapeDtypeStruct((n, ATTN_WIDTH), BF16),
            jax.ShapeDtypeStruct((n, KV_WIDTH), BF16),
            jax.ShapeDtypeStruct((n, KV_WIDTH), BF16),
        ],
        compiler_params=_params(("parallel",)),
        name="qkv_proj",
    )(x2, g, w, qg, kg, cos, sin)


def _log_sigmoid(x):
    return jnp.minimum(x, 0.0) - jnp.log(1.0 + jnp.exp(-jnp.abs(x)))


def _gla_proj_kernel(x_ref, g_ref, w_ref, wg_ref, bg_ref,
                     gq_ref, gk_ref, gv_ref, sg_ref, laf_ref, lab_ref):
    xn = _rms(x_ref[...], g_ref[...]).astype(BF16)
    kw = GLA_HEADS * GLA_DK
    assert kw == PROJ_COLS and GLA_WIDTH == 2 * PROJ_COLS

    def project(g):
        cols = slice(g * PROJ_COLS, (g + 1) * PROJ_COLS)
        return jnp.dot(xn, w_ref[:, cols], preferred_element_type=F32)

    def silu(x):
        return x * (1.0 / (1.0 + jnp.exp(-x)))

    outputs = [
        (gq_ref, 0, lambda z: z * (GLA_DK ** -0.5)),
        (gk_ref, 0, lambda z: z),
        (gv_ref, 0, lambda z: z),
        (gv_ref, PROJ_COLS, lambda z: z),
        (sg_ref, 0, silu),
        (sg_ref, PROJ_COLS, silu),
    ]
    lr = jnp.dot(xn, w_ref[:, len(outputs) * PROJ_COLS:], preferred_element_type=F32)
    pre = jnp.dot(lr, wg_ref[...], preferred_element_type=F32,
                  precision=lax.Precision.HIGHEST) + bg_ref[...]
    z = project(0)
    la = _log_sigmoid(pre) * (1.0 / GATE_NORMALIZER)
    laf_ref[...] = la[:, :kw]
    lab_ref[...] = la[:, kw:]
    for g, (ref, col, fn) in enumerate(outputs):
        z_next = project(g + 1) if g + 1 < len(outputs) else None
        ref[:, col:col + PROJ_COLS] = fn(z).astype(BF16)
        z = z_next


def _gla_proj(x2, g, w, wg, bg):
    n, d = x2.shape
    tm = PROJ_TM
    kw = GLA_HEADS * GLA_DK
    full = lambda i: (0, 0)
    row = lambda i: (i, 0)
    return pl.pallas_call(
        _gla_proj_kernel,
        grid=(n // tm,),
        in_specs=[
            pl.BlockSpec((tm, d), row),
            pl.BlockSpec((1, d), full),
            pl.BlockSpec(w.shape, full),
            pl.BlockSpec(wg.shape, full),
            pl.BlockSpec(bg.shape, full),
        ],
        out_specs=[
            pl.BlockSpec((tm, kw), row),
            pl.BlockSpec((tm, kw), row),
            pl.BlockSpec((tm, GLA_WIDTH), row),
            pl.BlockSpec((tm, GLA_WIDTH), row),
            pl.BlockSpec((tm, kw), row),
            pl.BlockSpec((tm, kw), row),
        ],
        out_shape=[
            jax.ShapeDtypeStruct((n, kw), BF16),
            jax.ShapeDtypeStruct((n, kw), BF16),
            jax.ShapeDtypeStruct((n, GLA_WIDTH), BF16),
            jax.ShapeDtypeStruct((n, GLA_WIDTH), BF16),
            jax.ShapeDtypeStruct((n, kw), F32),
            jax.ShapeDtypeStruct((n, kw), F32),
        ],
        compiler_params=_params(("parallel",)),
        name="gla_proj",
    )(x2, g, w, wg, bg)


def _attn_kernel(q_ref, k_ref, v_ref, o_ref):
    k = k_ref[0]
    v = v_ref[0]

    def scores(hh):
        sl = slice(hh * HEAD_DIM, (hh + 1) * HEAD_DIM)
        return lax.dot_general(q_ref[0, :, sl], k, NT_DIMS, preferred_element_type=F32)

    s = scores(0)
    for hh in range(GQA_GROUP):
        s_next = scores(hh + 1) if hh + 1 < GQA_GROUP else None
        m = jnp.max(s, axis=-1, keepdims=True)
        p = jnp.exp(s - m)
        l = jnp.sum(p, axis=-1, keepdims=True)
        o = jnp.dot(p.astype(BF16), v, preferred_element_type=F32)
        o_ref[0, :, hh * HEAD_DIM:(hh + 1) * HEAD_DIM] = (o / l).astype(BF16)
        s = s_next


def _attention(q, k, v):
    b, t, _ = q.shape
    tq = min(ATTN_TQ, t)
    gw = GQA_GROUP * HEAD_DIM
    return pl.pallas_call(
        _attn_kernel,
        grid=(b, N_KV_HEADS, t // tq),
        in_specs=[
            pl.BlockSpec((1, tq, gw), lambda bi, g, qi: (bi, qi, g)),
            pl.BlockSpec((1, t, HEAD_DIM), lambda bi, g, qi: (bi, 0, g)),
            pl.BlockSpec((1, t, HEAD_DIM), lambda bi, g, qi: (bi, 0, g)),
        ],
        out_specs=pl.BlockSpec((1, tq, gw), lambda bi, g, qi: (bi, qi, g)),
        out_shape=jax.ShapeDtypeStruct((b, t, ATTN_WIDTH), BF16),
        compiler_params=_params(("parallel", "parallel", "arbitrary")),
        name="gqa_attention",
    )(q, k, v)


def _split_dot(tri, g):
    hi = g.astype(BF16)
    lo = (g - hi.astype(F32)).astype(BF16)
    return (jnp.dot(tri, hi, preferred_element_type=F32)
            + jnp.dot(tri, lo, preferred_element_type=F32))


def _gla_kernel(q_ref, k_ref, v_ref, sg_ref, gf_ref, gb_ref, gain_ref, o_ref, acc_ref,
                sf_ref, sb_ref):
    t = q_ref.shape[1]
    c = GLA_CHUNK
    n = t // c
    mid = c // 2
    ri = lax.broadcasted_iota(jnp.int32, (c, c), 0)
    ci = lax.broadcasted_iota(jnp.int32, (c, c), 1)
    lower = ri >= ci
    tri_lo = lower.astype(BF16)
    tri_up = (ri <= ci).astype(BF16)
    sf_ref[...] = jnp.zeros_like(sf_ref)
    sb_ref[...] = jnp.zeros_like(sb_ref)

    upper = ri <= ci
    un = min(GLA_UNROLL, n // 2)

    def finish(rows, o):
        y = _rms(o, gain_ref[...]) * sg_ref[0, rows, :].astype(F32)
        o_ref[0, rows, :] = y.astype(BF16)

    def trip(i, second_half):
        items = []
        for u in range(un):
            idx = i * un + u
            items.append((idx, sf_ref, gf_ref, tri_lo, c - 1, lower))
            items.append((n - 1 - idx, sb_ref, gb_ref, tri_up, 0, upper))
        rows = [pl.ds(pl.multiple_of(it[0] * c, c), c) for it in items]
        bs = [_split_dot(it[3], it[2][0, r, :]) for it, r in zip(items, rows)]
        qs, ks, scores = [], [], []
        for it, r, b in zip(items, rows, bs):
            q = q_ref[0, r, :].astype(F32)
            k = k_ref[0, r, :].astype(F32)
            ref = b[mid:mid + 1, :]
            qt = (q * jnp.exp(b - ref)).astype(BF16)
            kt = (k * jnp.exp(ref - b)).astype(BF16)
            a = lax.dot_general(qt, kt, NT_DIMS, preferred_element_type=F32)
            qs.append(q)
            ks.append(k)
            scores.append(jnp.where(it[5], a, 0.0).astype(BF16))
        intra, updates, tots = [], [], []
        for it, r, b, k, a in zip(items, rows, bs, ks, scores):
            v = v_ref[0, r, :]
            tot = b[it[4]:it[4] + 1, :]
            kd = (k * jnp.exp(tot - b)).astype(BF16)
            intra.append(jnp.dot(a, v, preferred_element_type=F32))
            updates.append(lax.dot_general(v, kd, TN_DIMS, preferred_element_type=F32))
            tots.append(tot)
        for it, r, b, q, o_in, upd, tot in zip(items, rows, bs, qs, intra, updates, tots):
            s_ref = it[1]
            st = s_ref[...]
            o = o_in + lax.dot_general((q * jnp.exp(b)).astype(BF16), st.astype(BF16), NT_DIMS,
                                       preferred_element_type=F32)
            s_ref[...] = st * jnp.exp(tot) + upd
            if second_half:
                finish(r, acc_ref[r, :] + o)
            else:
                acc_ref[r, :] = o

    half_trips = n // 2 // un

    def first_half(i, carry):
        trip(i, False)
        return carry

    def second_half(i, carry):
        trip(i, True)
        return carry

    lax.fori_loop(0, half_trips, first_half, 0)
    lax.fori_loop(half_trips, 2 * half_trips, second_half, 0)


def _gla(gq, gk, gv, sg, laf, lab, gain):
    b, t, _ = gq.shape
    kspec = pl.BlockSpec((1, t, GLA_DK), lambda bi, h: (bi, 0, h))
    vspec = pl.BlockSpec((1, t, GLA_DV), lambda bi, h: (bi, 0, h))
    return pl.pallas_call(
        _gla_kernel,
        grid=(b, GLA_HEADS),
        in_specs=[kspec, kspec, vspec, vspec, kspec, kspec,
                  pl.BlockSpec((1, GLA_DV), lambda bi, h: (0, 0))],
        out_specs=vspec,
        out_shape=jax.ShapeDtypeStruct((b, t, GLA_WIDTH), BF16),
        scratch_shapes=[
            pltpu.VMEM((t, GLA_DV), F32),
            pltpu.VMEM((GLA_DV, GLA_DK), F32),
            pltpu.VMEM((GLA_DV, GLA_DK), F32),
        ],
        compiler_params=_params(("parallel", "parallel")),
        name="gla_bidir",
    )(gq, gk, gv, sg, laf, lab, gain)


def _out_proj_kernel(x_ref, oa_ref, og_ref, ag_ref, wa_ref, wg_ref, fg_ref, h_ref, hn_ref):
    oa = _rms(oa_ref[...].astype(F32), ag_ref[...]).astype(BF16)
    h = (x_ref[...]
         + jnp.dot(oa, wa_ref[...], preferred_element_type=F32)
         + jnp.dot(og_ref[...], wg_ref[...], preferred_element_type=F32))
    h_ref[...] = h
    hn_ref[...] = _rms(h, fg_ref[...]).astype(BF16)


def _out_proj(x2, oa, og, ag, wa, wg, fg):
    n, d = x2.shape
    tm = PROJ_TM
    full = lambda i: (0, 0)
    row = lambda i: (i, 0)
    return pl.pallas_call(
        _out_proj_kernel,
        grid=(n // tm,),
        in_specs=[
            pl.BlockSpec((tm, d), row),
            pl.BlockSpec((tm, ATTN_WIDTH), row),
            pl.BlockSpec((tm, GLA_WIDTH), row),
            pl.BlockSpec((1, ATTN_WIDTH), full),
            pl.BlockSpec(wa.shape, full),
            pl.BlockSpec(wg.shape, full),
            pl.BlockSpec((1, d), full),
        ],
        out_specs=[pl.BlockSpec((tm, d), row), pl.BlockSpec((tm, d), row)],
        out_shape=[jax.ShapeDtypeStruct((n, d), F32), jax.ShapeDtypeStruct((n, d), BF16)],
        compiler_params=_params(("parallel",)),
        name="out_proj",
    )(x2, oa, og, ag, wa, wg, fg)


def _cmp_exchange(rows, i, j):
    a, b = rows[i], rows[j]
    if b is None:
        return
    if a is None:
        rows[i], rows[j] = b, None
        return
    rows[i], rows[j] = jnp.maximum(a, b), jnp.minimum(a, b)


def _bitonic_merge_desc(rows):
    n = len(rows)
    j = n // 2
    while j >= 1:
        for i in range(n):
            l = i ^ j
            if l > i:
                _cmp_exchange(rows, i, l)
        j //= 2


def _bitonic_sort_desc(rows):
    n = len(rows)
    k = 2
    while k <= n:
        j = k // 2
        while j >= 1:
            for i in range(n):
                l = i ^ j
                if l > i:
                    if (i & k) == 0:
                        _cmp_exchange(rows, i, l)
                    else:
                        _cmp_exchange(rows, l, i)
            j //= 2
        k *= 2


def _top_of_union(a, b):
    n = len(a)
    out = []
    for i in range(n):
        x, y = a[i], b[n - 1 - i]
        out.append(x if y is None else (y if x is None else jnp.maximum(x, y)))
    _bitonic_merge_desc(out)
    return out


def _top16_rows(s):
    l = s.shape[1]
    s3 = s.reshape(N_KEYS // SUBLANES, SUBLANES, l)
    rows = [s3[i] for i in range(N_KEYS // SUBLANES)]
    _bitonic_sort_desc(rows)
    for shift in (4, 2, 1):
        other = [pltpu.roll(r, shift, axis=0) for r in rows]
        rows = _top_of_union(rows, other)
    return rows


_CANDIDATES = [(i, j) for i in range(PEER_TOPK) for j in range(PEER_TOPK)
               if (i + 1) * (j + 1) <= PEER_TOPK]


def _bf16_bits(x):
    bits = lax.bitcast_convert_type(x, jnp.uint32)
    return (bits + jnp.uint32(0x7FFF) + ((bits >> 16) & jnp.uint32(1))) >> 16


def _bf16_pair_words(x):
    hi = _bf16_bits(x)
    return lax.bitcast_convert_type((hi << 16) | hi, jnp.int32)


def _bf16_packed_rows(x):
    hi = _bf16_bits(x)
    half = x.shape[0] // 2
    return lax.bitcast_convert_type(hi[:half] | (hi[half:] << 16), jnp.int32)


def _route_kernel(hn_ref, wqt_ref, keys_ref, rank_ref, e2_ref, cnt_ref, w1_ref, qt_ref):
    hh = pl.program_id(1)

    @pl.when(hh == 0)
    def _():
        qt_ref[...] = lax.dot_general(wqt_ref[...], hn_ref[...], NT_DIMS,
                                      preferred_element_type=F32).astype(BF16)

    base = pl.multiple_of(hh * PEER_QDIM, PEER_QDIM)
    q1 = qt_ref[pl.ds(base, PEER_HALF), :]
    q2 = qt_ref[pl.ds(base + PEER_HALF, PEER_HALF), :]
    s1 = jnp.dot(keys_ref[0, 0], q1, preferred_element_type=F32)
    s2 = jnp.dot(keys_ref[0, 1], q2, preferred_element_type=F32)
    tb = s1.shape[1]
    lgs = tb // LANES
    a = _top16_rows(s1)
    b = _top16_rows(s2)

    sub = lax.broadcasted_iota(jnp.int32, (SUBLANES, LANES), 0)

    def compact(r):
        out = r[:, :LANES]
        for g in range(1, lgs):
            out = jnp.where(sub == g, r[:, g * LANES:(g + 1) * LANES], out)
        return out

    def expand(c):
        return jnp.concatenate(
            [jnp.broadcast_to(c[g:g + 1, :], (SUBLANES, LANES)) for g in range(lgs)], axis=1)

    ac = [compact(r) for r in a]
    bc = [compact(r) for r in b]
    cand = [ac[i] + bc[j] for (i, j) in _CANDIDATES]
    cand += [None] * (64 - len(cand))
    groups = []
    for gi in range(4):
        grp = cand[gi * 16:(gi + 1) * 16]
        _bitonic_sort_desc(grp)
        groups.append(grp)
    top = _top_of_union(_top_of_union(groups[0], groups[1]),
                        _top_of_union(groups[2], groups[3]))
    z = jnp.zeros_like(top[0])
    for tk in top:
        z = z + jnp.exp(tk - top[0])
    tau = expand(top[PEER_TOPK - 1])
    inv_z = expand(1.0 / z)
    full = lambda r: jnp.broadcast_to(r[0:1, :], (N_KEYS, tb))
    tau_f = full(tau)
    cnt_bits = jnp.zeros((N_KEYS, tb), jnp.int32)
    rank_bits = jnp.zeros((N_KEYS, tb), jnp.int32)
    for j in range(PEER_TOPK):
        bj = full(b[j])
        bits = int(np.asarray(j + 1, dtype=np.float32).view(np.uint32)) >> 16
        cnt_bits = jnp.where(s1 + bj >= tau_f, bits, cnt_bits)
        rank_bits = jnp.where(bj > s2, bits, rank_bits)
    e2 = jnp.exp(s2 - full(b[0]))
    w1 = jnp.exp(s1 - full(a[0])) * full(inv_z)
    half = N_KEYS // 2
    rank_w = rank_bits[:half] | (rank_bits[half:] << 16)
    e2_w = _bf16_packed_rows(e2)
    cnt_w = cnt_bits | (cnt_bits << 16)
    w1_w = _bf16_pair_words(w1)
    for lg in range(tb // LANES):
        lanes = slice(lg * LANES, (lg + 1) * LANES)
        rank_ref[lg, 0] = rank_w[:, lanes]
        e2_ref[lg, 0] = e2_w[:, lanes]
        cnt_ref[lg, 0] = cnt_w[:, lanes]
        w1_ref[lg, 0] = w1_w[:, lanes]


def _peer_route(hn, wqt, keys):
    n, d = hn.shape
    tb = min(ROUTE_TB, n)
    lgs = tb // LANES
    tab = jax.ShapeDtypeStruct((n // LANES, PEER_HEADS, N_KEYS, LANES), jnp.int32)
    tab16 = jax.ShapeDtypeStruct((n // LANES, PEER_HEADS, N_KEYS // 2, LANES), jnp.int32)
    tspec = pl.BlockSpec((lgs, 1, N_KEYS, LANES), lambda i, h: (i, h, 0, 0))
    tspec16 = pl.BlockSpec((lgs, 1, N_KEYS // 2, LANES), lambda i, h: (i, h, 0, 0))
    return pl.pallas_call(
        _route_kernel,
        grid=(n // tb, PEER_HEADS),
        in_specs=[
            pl.BlockSpec((tb, d), lambda i, h: (i, 0)),
            pl.BlockSpec(wqt.shape, lambda i, h: (0, 0), pipeline_mode=pl.Buffered(1)),
            pl.BlockSpec((1, 2, N_KEYS, PEER_HALF), lambda i, h: (h, 0, 0, 0)),
        ],
        out_specs=[tspec16, tspec16, tspec, tspec],
        out_shape=[tab16, tab16, tab, tab],
        scratch_shapes=[pltpu.VMEM((PEER_HEADS * PEER_QDIM, tb), BF16)],
        compiler_params=_params(("parallel", "arbitrary")),
        name="peer_route",
    )(hn, wqt, keys)


def _gelu(x):
    return 0.5 * x * (1.0 + lax.erf(x * (2.0 ** -0.5)))


def _peer_kernel(hn_ref, h_ref, u0_ref, u_ref, vt_ref, rank_ref, e2_ref, cnt_ref, w1_ref, fg_ref,
                 y_ref, acc_ref, hta_ref, htb_ref, w_ref):
    j = pl.program_id(1)
    nj = pl.num_programs(1)
    lane_groups, te, _ = hta_ref.shape
    rows_per_step = te // N_KEYS

    def score(u_tile_ref, dst_ref, hn=None):
        hn = hn_ref[...] if hn is None else hn
        ht = lax.dot_general(u_tile_ref[...], hn, NT_DIMS,
                             preferred_element_type=F32)
        for lg in range(lane_groups):
            dst_ref[lg] = ht[:, lg * LANES:(lg + 1) * LANES]

    def released_by(gates):
        hn = hn_ref[...]
        tb, d = hn.shape
        n_tiles, k_tiles = tb // MXU_DIM, d // MXU_DIM
        groups = MXU_DIM // BF16_ROWS
        assert len(gates) == n_tiles * k_tiles * groups
        rows = []
        for nt in range(n_tiles):
            for rg in range(groups):
                r0 = nt * MXU_DIM + rg * BF16_ROWS
                cols = []
                for kt in range(k_tiles):
                    g = gates[(nt * k_tiles + kt) * groups + rg]
                    z = g * jnp.zeros_like(g)
                    z = jnp.concatenate([z] * (MXU_DIM // LANES), axis=1)
                    cols.append(hn[r0:r0 + BF16_ROWS, kt * MXU_DIM:(kt + 1) * MXU_DIM] + z)
                rows.append(jnp.concatenate(cols, axis=1))
        return jnp.concatenate(rows, axis=0)

    @pl.when(j == 0)
    def _():
        acc_ref[...] = jnp.zeros_like(acc_ref)

    @pl.when((pl.program_id(0) == 0) & (j == 0))
    def _():
        score(u0_ref, hta_ref)

    def step(cur_ref, nxt_ref):
        zero = jnp.zeros((BF16_ROWS, LANES), BF16)
        gates = []

        def row_tile(ref, lg, hh, rr):
            words = jnp.broadcast_to(ref[lg, hh, rr:rr + 1, :], (SUBLANES, LANES))
            return pltpu.bitcast(words, BF16)

        for lg in range(lane_groups):
            lanes = slice(lg * LANES, (lg + 1) * LANES)
            for rr in range(rows_per_step):
                cnt = [row_tile(cnt_ref, lg, hh, rr) for hh in range(PEER_HEADS)]
                w1 = [row_tile(w1_ref, lg, hh, rr) for hh in range(PEER_HEADS)]
                for kk in range(N_KEYS // BF16_ROWS):
                    words = slice(kk * SUBLANES, (kk + 1) * SUBLANES)
                    rows = slice(rr * N_KEYS + kk * BF16_ROWS, rr * N_KEYS + (kk + 1) * BF16_ROWS)
                    gate = zero
                    for hh in range(PEER_HEADS):
                        rank = pltpu.bitcast(rank_ref[lg, hh, words, :], BF16)
                        e2 = pltpu.bitcast(e2_ref[lg, hh, words, :], BF16)
                        gate = gate + jnp.where(rank < cnt[hh], e2, zero) * w1[hh]
                    gates.append(gate)
                    w_ref[rows, lanes] = gate * _gelu(cur_ref[lg, rows, :]).astype(BF16)
        score(u_ref, nxt_ref, released_by(gates))
        acc_ref[...] += jnp.dot(vt_ref[...], w_ref[...], preferred_element_type=F32)

    @pl.when(j % 2 == 0)
    def _():
        step(hta_ref, htb_ref)

    @pl.when(j % 2 == 1)
    def _():
        step(htb_ref, hta_ref)

    @pl.when(j == nj - 1)
    def _():
        h = h_ref[...] + acc_ref[...].T
        y_ref[...] = _rms(h, fg_ref[...])


def _peer_dense(hn, h, u, vt, rank, e2, cnt, w1, fg):
    n, d = hn.shape
    e = u.shape[0]
    tb, te = PEER_TB, PEER_TE
    nj = e // te
    ni = n // tb
    assert nj % 2 == 0, "score buffers alternate with the step parity across token blocks"
    lgs = tb // LANES
    rows = te // N_KEYS
    once = pl.Buffered(1)
    tspec = pl.BlockSpec((lgs, PEER_HEADS, N_KEYS // 2, LANES), lambda i, j: (i, 0, 0, 0),
                         pipeline_mode=once)
    rspec = pl.BlockSpec((lgs, PEER_HEADS, rows, LANES), lambda i, j: (i, 0, j, 0))
    return pl.pallas_call(
        _peer_kernel,
        grid=(ni, nj),
        in_specs=[
            pl.BlockSpec((tb, d), lambda i, j: (jnp.minimum(i + (j + 1) // nj, ni - 1), 0)),
            pl.BlockSpec((tb, d), lambda i, j: (i, 0), pipeline_mode=once),
            pl.BlockSpec((te, d), lambda i, j: (0, 0), pipeline_mode=once),
            pl.BlockSpec((te, d), lambda i, j: ((j + 1) % nj, 0)),
            pl.BlockSpec((None, d, te), lambda i, j: (j, 0, 0)),
            tspec, tspec, rspec, rspec,
            pl.BlockSpec((1, d), lambda i, j: (0, 0)),
        ],
        out_specs=pl.BlockSpec((tb, d), lambda i, j: (i, 0)),
        out_shape=jax.ShapeDtypeStruct((n, d), F32),
        scratch_shapes=[
            pltpu.VMEM((d, tb), F32),
            pltpu.VMEM((lgs, te, LANES), F32),
            pltpu.VMEM((lgs, te, LANES), F32),
            pltpu.VMEM((te, tb), BF16),
        ],
        compiler_params=_params(("arbitrary", "arbitrary")),
        name="peer_dense",
    )(hn, h, u, u, vt, rank, e2, cnt, w1, fg)


def _rope_tables(t):
    n_rows = t // GRID_W
    row = jnp.repeat(jnp.arange(n_rows, dtype=F32), GRID_W)
    col = jnp.tile(jnp.arange(GRID_W, dtype=F32), n_rows)
    freqs = ROPE_THETA ** (-jnp.arange(0, AXIS_DIM, 2, dtype=F32) / AXIS_DIM)
    ang = jnp.concatenate([row[:, None] * freqs, col[:, None] * freqs], axis=-1)
    cos, sin = jnp.cos(ang), jnp.sin(ang)
    return jnp.concatenate([cos, cos], axis=-1), jnp.concatenate([-sin, sin], axis=-1)


def _prepare(norm_mix_g, w_in, q_norm_g, k_norm_g, w_gate_f2, b_gate_f, w_gate_b2, b_gate_b,
             gla_norm_g, attn_out_norm_g, w_out, norm_ffn_g, peer_w_q, peer_sub_keys, peer_u,
             peer_v, final_norm_g):
    perm = np.concatenate([np.arange(0, HEAD_DIM, 2), np.arange(1, HEAD_DIM, 2)])
    kw = GLA_HEADS * GLA_DK
    o_k = ATTN_WIDTH
    o_v = o_k + KV_WIDTH
    o_g = o_v + KV_WIDTH
    o_lr = o_g + 2 * kw + 2 * GLA_WIDTH
    wq = w_in[:, :o_k].reshape(D_MODEL, N_HEADS, HEAD_DIM)[:, :, perm].reshape(D_MODEL, o_k)
    wk = w_in[:, o_k:o_v].reshape(D_MODEL, N_KV_HEADS, HEAD_DIM)[:, :, perm].reshape(D_MODEL, KV_WIDTH)
    w_qkv = jnp.concatenate([wq, wk, w_in[:, o_v:o_g]], axis=1).astype(BF16)
    w_lr = jnp.pad(w_in[:, o_lr:], ((0, 0), (0, LANES - 2 * GATE_RANK)))
    w_gla = jnp.concatenate([w_in[:, o_g:o_lr], w_lr], axis=1).astype(BF16)
    wg = jnp.zeros((LANES, 2 * kw), F32)
    wg = wg.at[:GATE_RANK, :kw].set(w_gate_f2).at[GATE_RANK:2 * GATE_RANK, kw:].set(w_gate_b2)
    bg = jnp.concatenate([b_gate_f, b_gate_b])[None, :]

    keys = jnp.stack([peer_sub_keys[:, 0], peer_sub_keys[:, 1][:, PEER_KEY2_ROWS]], axis=1)
    return dict(
        norm_mix_g=norm_mix_g[None, :], w_qkv=w_qkv, w_gla=w_gla, wg=wg, bg=bg,
        qg=q_norm_g[perm][None, :], kg=k_norm_g[perm][None, :],
        gla_gain=gla_norm_g[None, :], attn_gain=attn_out_norm_g[None, :],
        wo_a=w_out[:ATTN_WIDTH].astype(BF16), wo_g=w_out[ATTN_WIDTH:].astype(BF16),
        ffn_g=norm_ffn_g[None, :], wqt=peer_w_q.T.astype(BF16),
        keys=keys.astype(BF16), u=peer_u.astype(BF16),
        vt=peer_v.astype(BF16).reshape(-1, PEER_TE, D_MODEL).transpose(0, 2, 1),
        final_g=final_norm_g[None, :],
    )


def _layer(x, p):
    b, t, d = x.shape
    n = b * t
    x2 = x.reshape(n, d)
    cos, sin = _rope_tables(t)
    q, k, v = _qkv_proj(x2, p["norm_mix_g"], p["w_qkv"], p["qg"], p["kg"], cos, sin, t)
    gq, gk, gv, sg, laf, lab = _gla_proj(x2, p["norm_mix_g"], p["w_gla"], p["wg"], p["bg"])
    oa = _attention(q.reshape(b, t, -1), k.reshape(b, t, -1), v.reshape(b, t, -1))
    r3 = lambda a: a.reshape(b, t, -1)
    og = _gla(r3(gq), r3(gk), r3(gv), r3(sg), r3(laf), r3(lab), p["gla_gain"])
    h, hn = _out_proj(x2, oa.reshape(n, -1), og.reshape(n, -1), p["attn_gain"],
                      p["wo_a"], p["wo_g"], p["ffn_g"])
    rank, e2, cnt, w1 = _peer_route(hn, p["wqt"], p["keys"])
    y = _peer_dense(hn, h, p["u"], p["vt"], rank, e2, cnt, w1, p["final_g"])
    return y.reshape(b, t, d)


def kernel(x_prompt, x_sample, norm_mix_g, w_in, q_norm_g, k_norm_g, w_gate_f2, b_gate_f,
           w_gate_b2, b_gate_b, gla_norm_g, attn_out_norm_g, w_out, norm_ffn_g, peer_w_q,
           peer_sub_keys, peer_u, peer_v, final_norm_g):
    p = _prepare(norm_mix_g, w_in, q_norm_g, k_norm_g, w_gate_f2, b_gate_f, w_gate_b2,
                 b_gate_b, gla_norm_g, attn_out_norm_g, w_out, norm_ffn_g, peer_w_q,
                 peer_sub_keys, peer_u, peer_v, final_norm_g)
    return (_layer(x_prompt, p), _layer(x_sample, p))
```

```python
import functools
import math

import jax
import jax.numpy as jnp
import numpy as np
from jax import lax
from jax.experimental import pallas as pl
from jax.experimental.pallas import tpu as pltpu

F32 = jnp.float32
BF16 = jnp.bfloat16

D_MODEL = 2048
GRID_W = 64
HEAD_DIM = 128
N_HEADS = 8
N_KV_HEADS = 2
GQA_GROUP = N_HEADS // N_KV_HEADS
ATTN_WIDTH = N_HEADS * HEAD_DIM
KV_WIDTH = N_KV_HEADS * HEAD_DIM
ROPE_THETA = 10000.0
AXIS_DIM = HEAD_DIM // 2
GLA_HEADS = 4
GLA_DK = 128
GLA_DV = 256
GLA_WIDTH = GLA_HEADS * GLA_DV
GATE_RANK = 16
GATE_NORMALIZER = 16.0
PEER_HEADS = 8
N_KEYS = 128
PEER_TOPK = 16
PEER_HALF = 128
PEER_QDIM = 2 * PEER_HALF
EPS = 1e-6
LOG2_E = math.log2(math.e)

LANES = 128
SUBLANES = 8
BF16_ROWS = 16
MXU_DIM = 256
VMEM_LIMIT = 58 * 1024 * 1024

PROJ_TM = 512
PROJ_COLS = 512
ATTN_TQ = 256
GLA_CHUNK = 64
GLA_UNROLL = 16
ROUTE_TB = 512
PEER_TB = 512
PEER_TE = 1024

PEER_KEY2_ROWS = np.concatenate([np.arange(0, N_KEYS, 2), np.arange(1, N_KEYS, 2)])

NT_DIMS = (((1,), (1,)), ((), ()))
TN_DIMS = (((0,), (0,)), ((), ()))


def _params(sem, vmem=VMEM_LIMIT, flags=None):
    return pltpu.CompilerParams(dimension_semantics=sem, vmem_limit_bytes=vmem, flags=flags)


def _rms(x, g):
    return x * lax.rsqrt(jnp.mean(x * x, axis=-1, keepdims=True) + EPS) * g


def _qkv_kernel(x_ref, g_ref, w_ref, qg_ref, kg_ref, cos_ref, sin_ref, q_ref, k_ref, v_ref):
    xn = _rms(x_ref[...], g_ref[...]).astype(BF16)
    z = jnp.dot(xn, w_ref[...], preferred_element_type=F32)
    c = cos_ref[...]
    s = sin_ref[...]

    def norm_rope(hd, gain, scale):
        hn = _rms(hd, gain)
        rot = pltpu.roll(hn, HEAD_DIM // 2, axis=1)
        return ((hn * c + rot * s) * scale).astype(BF16)

    for hh in range(N_HEADS):
        sl = slice(hh * HEAD_DIM, (hh + 1) * HEAD_DIM)
        q_ref[:, sl] = norm_rope(z[:, sl], qg_ref[...], HEAD_DIM ** -0.5 * LOG2_E)
    for hh in range(N_KV_HEADS):
        sl = slice(hh * HEAD_DIM, (hh + 1) * HEAD_DIM)
        zsl = slice(ATTN_WIDTH + hh * HEAD_DIM, ATTN_WIDTH + (hh + 1) * HEAD_DIM)
        k_ref[:, sl] = norm_rope(z[:, zsl], kg_ref[...], 1.0)
    v_ref[...] = z[:, ATTN_WIDTH + KV_WIDTH:].astype(BF16)


def _qkv_proj(x2, g, w, qg, kg, cos, sin, seq):
    n, d = x2.shape
    tm = min(PROJ_TM, seq)
    nt = seq // tm
    wq = w.shape[1]
    full = lambda i: (0, 0)
    return pl.pallas_call(
        _qkv_kernel,
        grid=(n // tm,),
        in_specs=[
            pl.BlockSpec((tm, d), lambda i: (i, 0)),
            pl.BlockSpec((1, d), full),
            pl.BlockSpec((d, wq), full),
            pl.BlockSpec((1, HEAD_DIM), full),
            pl.BlockSpec((1, HEAD_DIM), full),
            pl.BlockSpec((tm, HEAD_DIM), lambda i: (i % nt, 0)),
            pl.BlockSpec((tm, HEAD_DIM), lambda i: (i % nt, 0)),
        ],
        out_specs=[
            pl.BlockSpec((tm, ATTN_WIDTH), lambda i: (i, 0)),
            pl.BlockSpec((tm, KV_WIDTH), lambda i: (i, 0)),
            pl.BlockSpec((tm, KV_WIDTH), lambda i: (i, 0)),
        ],
        out_shape=[
            jax.ShapeDtypeStruct((n, ATTN_WIDTH), BF16),
            jax.ShapeDtypeStruct((n, KV_WIDTH), BF16),
            jax.ShapeDtypeStruct((n, KV_WIDTH), BF16),
        ],
        compiler_params=_params(("parallel",)),
        name="qkv_proj",
    )(x2, g, w, qg, kg, cos, sin)


def _log_sigmoid(x):
    return jnp.minimum(x, 0.0) - jnp.log(1.0 + jnp.exp(-jnp.abs(x)))


def _gla_proj_kernel(x_ref, g_ref, w_ref, wg_ref, bg_ref,
                     gq_ref, gk_ref, gv_ref, sg_ref, laf_ref, lab_ref):
    xn = _rms(x_ref[...], g_ref[...]).astype(BF16)
    kw = GLA_HEADS * GLA_DK
    assert kw == PROJ_COLS and GLA_WIDTH == 2 * PROJ_COLS

    def project(g):
        cols = slice(g * PROJ_COLS, (g + 1) * PROJ_COLS)
        return jnp.dot(xn, w_ref[:, cols], preferred_element_type=F32)

    def silu(x):
        return x * (1.0 / (1.0 + jnp.exp(-x)))

    outputs = [
        (gq_ref, 0, lambda z: z * (GLA_DK ** -0.5)),
        (gk_ref, 0, lambda z: z),
        (gv_ref, 0, lambda z: z),
        (gv_ref, PROJ_COLS, lambda z: z),
        (sg_ref, 0, silu),
        (sg_ref, PROJ_COLS, silu),
    ]
    lr = jnp.dot(xn, w_ref[:, len(outputs) * PROJ_COLS:], preferred_element_type=F32)
    pre = jnp.dot(lr, wg_ref[...], preferred_element_type=F32,
                  precision=lax.Precision.HIGHEST) + bg_ref[...]
    z = project(0)
    la = _log_sigmoid(pre) * (1.0 / GATE_NORMALIZER)
    laf_ref[...] = la[:, :kw]
    lab_ref[...] = la[:, kw:]
    for g, (ref, col, fn) in enumerate(outputs):
        z_next = project(g + 1) if g + 1 < len(outputs) else None
        ref[:, col:col + PROJ_COLS] = fn(z).astype(BF16)
        z = z_next


def _gla_proj(x2, g, w, wg, bg):
    n, d = x2.shape
    tm = PROJ_TM
    kw = GLA_HEADS * GLA_DK
    full = lambda i: (0, 0)
    row = lambda i: (i, 0)
    return pl.pallas_call(
        _gla_proj_kernel,
        grid=(n // tm,),
        in_specs=[
            pl.BlockSpec((tm, d), row),
            pl.BlockSpec((1, d), full),
            pl.BlockSpec(w.shape, full),
            pl.BlockSpec(wg.shape, full),
            pl.BlockSpec(bg.shape, full),
        ],
        out_specs=[
            pl.BlockSpec((tm, kw), row),
            pl.BlockSpec((tm, kw), row),
            pl.BlockSpec((tm, GLA_WIDTH), row),
            pl.BlockSpec((tm, GLA_WIDTH), row),
            pl.BlockSpec((tm, kw), row),
            pl.BlockSpec((tm, kw), row),
        ],
        out_shape=[
            jax.ShapeDtypeStruct((n, kw), BF16),
            jax.ShapeDtypeStruct((n, kw), BF16),
            jax.ShapeDtypeStruct((n, GLA_WIDTH), BF16),
            jax.ShapeDtypeStruct((n, GLA_WIDTH), BF16),
            jax.ShapeDtypeStruct((n, kw), F32),
            jax.ShapeDtypeStruct((n, kw), F32),
        ],
        compiler_params=_params(("parallel",)),
        name="gla_proj",
    )(x2, g, w, wg, bg)


def _attn_kernel(q_ref, k_ref, v_ref, o_ref):
    k = k_ref[0]
    v = v_ref[0]

    def scores(hh):
        sl = slice(hh * HEAD_DIM, (hh + 1) * HEAD_DIM)
        return lax.dot_general(q_ref[0, :, sl], k, NT_DIMS, preferred_element_type=F32)

    s = scores(0)
    for hh in range(GQA_GROUP):
        s_next = scores(hh + 1) if hh + 1 < GQA_GROUP else None
        m = jnp.max(s, axis=-1, keepdims=True)
        p = jnp.exp2(s - m)
        l = jnp.sum(p, axis=-1, keepdims=True)
        o = jnp.dot(p.astype(BF16), v, preferred_element_type=F32)
        o_ref[0, :, hh * HEAD_DIM:(hh + 1) * HEAD_DIM] = (o / l).astype(BF16)
        s = s_next


def _attention(q, k, v):
    b, t, _ = q.shape
    tq = min(ATTN_TQ, t)
    gw = GQA_GROUP * HEAD_DIM
    return pl.pallas_call(
        _attn_kernel,
        grid=(b, N_KV_HEADS, t // tq),
        in_specs=[
            pl.BlockSpec((1, tq, gw), lambda bi, g, qi: (bi, qi, g)),
            pl.BlockSpec((1, t, HEAD_DIM), lambda bi, g, qi: (bi, 0, g)),
            pl.BlockSpec((1, t, HEAD_DIM), lambda bi, g, qi: (bi, 0, g)),
        ],
        out_specs=pl.BlockSpec((1, tq, gw), lambda bi, g, qi: (bi, qi, g)),
        out_shape=jax.ShapeDtypeStruct((b, t, ATTN_WIDTH), BF16),
        compiler_params=_params(("parallel", "parallel", "arbitrary")),
        name="gqa_attention",
    )(q, k, v)


def _split_dot(tri, g):
    hi = g.astype(BF16)
    lo = (g - hi.astype(F32)).astype(BF16)
    return (jnp.dot(tri, hi, preferred_element_type=F32)
            + jnp.dot(tri, lo, preferred_element_type=F32))


def _gla_kernel(q_ref, k_ref, v_ref, sg_ref, gf_ref, gb_ref, gain_ref, o_ref, acc_ref,
                sf_ref, sb_ref):
    t = q_ref.shape[1]
    c = GLA_CHUNK
    n = t // c
    mid = c // 2
    ri = lax.broadcasted_iota(jnp.int32, (c, c), 0)
    ci = lax.broadcasted_iota(jnp.int32, (c, c), 1)
    lower = ri >= ci
    tri_lo = lower.astype(BF16)
    tri_up = (ri <= ci).astype(BF16)
    sf_ref[...] = jnp.zeros_like(sf_ref)
    sb_ref[...] = jnp.zeros_like(sb_ref)

    upper = ri <= ci
    un = min(GLA_UNROLL, n // 2)

    def finish(rows, o):
        y = _rms(o, gain_ref[...]) * sg_ref[0, rows, :].astype(F32)
        o_ref[0, rows, :] = y.astype(BF16)

    def trip(i, second_half):
        items = []
        for u in range(un):
            idx = i * un + u
            items.append((idx, sf_ref, gf_ref, tri_lo, c - 1, lower))
            items.append((n - 1 - idx, sb_ref, gb_ref, tri_up, 0, upper))
        rows = [pl.ds(pl.multiple_of(it[0] * c, c), c) for it in items]
        bs = [_split_dot(it[3], it[2][0, r, :]) for it, r in zip(items, rows)]
        qs, ks, scores = [], [], []
        for it, r, b in zip(items, rows, bs):
            q = q_ref[0, r, :].astype(F32)
            k = k_ref[0, r, :].astype(F32)
            ref = b[mid:mid + 1, :]
            qt = (q * jnp.exp(b - ref)).astype(BF16)
            kt = (k * jnp.exp(ref - b)).astype(BF16)
            a = lax.dot_general(qt, kt, NT_DIMS, preferred_element_type=F32)
            qs.append(q)
            ks.append(k)
            scores.append(jnp.where(it[5], a, 0.0).astype(BF16))
        intra, updates, tots = [], [], []
        for it, r, b, k, a in zip(items, rows, bs, ks, scores):
            v = v_ref[0, r, :]
            tot = b[it[4]:it[4] + 1, :]
            kd = (k * jnp.exp(tot - b)).astype(BF16)
            intra.append(jnp.dot(a, v, preferred_element_type=F32))
            updates.append(lax.dot_general(v, kd, TN_DIMS, preferred_element_type=F32))
            tots.append(tot)
        for it, r, b, q, o_in, upd, tot in zip(items, rows, bs, qs, intra, updates, tots):
            s_ref = it[1]
            st = s_ref[...]
            o = o_in + lax.dot_general((q * jnp.exp(b)).astype(BF16), st.astype(BF16), NT_DIMS,
                                       preferred_element_type=F32)
            s_ref[...] = st * jnp.exp(tot) + upd
            if second_half:
                finish(r, acc_ref[r, :] + o)
            else:
                acc_ref[r, :] = o

    half_trips = n // 2 // un

    def first_half(i, carry):
        trip(i, False)
        return carry

    def second_half(i, carry):
        trip(i, True)
        return carry

    lax.fori_loop(0, half_trips, first_half, 0)
    lax.fori_loop(half_trips, 2 * half_trips, second_half, 0)


def _gla(gq, gk, gv, sg, laf, lab, gain):
    b, t, _ = gq.shape
    kspec = pl.BlockSpec((1, t, GLA_DK), lambda bi, h: (bi, 0, h))
    vspec = pl.BlockSpec((1, t, GLA_DV), lambda bi, h: (bi, 0, h))
    return pl.pallas_call(
        _gla_kernel,
        grid=(b, GLA_HEADS),
        in_specs=[kspec, kspec, vspec, vspec, kspec, kspec,
                  pl.BlockSpec((1, GLA_DV), lambda bi, h: (0, 0))],
        out_specs=vspec,
        out_shape=jax.ShapeDtypeStruct((b, t, GLA_WIDTH), BF16),
        scratch_shapes=[
            pltpu.VMEM((t, GLA_DV), F32),
            pltpu.VMEM((GLA_DV, GLA_DK), F32),
            pltpu.VMEM((GLA_DV, GLA_DK), F32),
        ],
        compiler_params=_params(("parallel", "parallel")),
        name="gla_bidir",
    )(gq, gk, gv, sg, laf, lab, gain)


def _out_proj_kernel(x_ref, oa_ref, og_ref, ag_ref, wa_ref, wg_ref, fg_ref, h_ref, hn_ref):
    oa = _rms(oa_ref[...].astype(F32), ag_ref[...]).astype(BF16)
    h = (x_ref[...]
         + jnp.dot(oa, wa_ref[...], preferred_element_type=F32)
         + jnp.dot(og_ref[...], wg_ref[...], preferred_element_type=F32))
    h_ref[...] = h
    hn_ref[...] = _rms(h, fg_ref[...]).astype(BF16)


def _out_proj(x2, oa, og, ag, wa, wg, fg):
    n, d = x2.shape
    tm = PROJ_TM
    full = lambda i: (0, 0)
    row = lambda i: (i, 0)
    return pl.pallas_call(
        _out_proj_kernel,
        grid=(n // tm,),
        in_specs=[
            pl.BlockSpec((tm, d), row),
            pl.BlockSpec((tm, ATTN_WIDTH), row),
            pl.BlockSpec((tm, GLA_WIDTH), row),
            pl.BlockSpec((1, ATTN_WIDTH), full),
            pl.BlockSpec(wa.shape, full),
            pl.BlockSpec(wg.shape, full),
            pl.BlockSpec((1, d), full),
        ],
        out_specs=[pl.BlockSpec((tm, d), row), pl.BlockSpec((tm, d), row)],
        out_shape=[jax.ShapeDtypeStruct((n, d), F32), jax.ShapeDtypeStruct((n, d), BF16)],
        compiler_params=_params(("parallel",)),
        name="out_proj",
    )(x2, oa, og, ag, wa, wg, fg)


def _cmp_exchange(rows, i, j):
    a, b = rows[i], rows[j]
    if b is None:
        return
    if a is None:
        rows[i], rows[j] = b, None
        return
    rows[i], rows[j] = jnp.maximum(a, b), jnp.minimum(a, b)


def _bitonic_merge_desc(rows):
    n = len(rows)
    j = n // 2
    while j >= 1:
        for i in range(n):
            l = i ^ j
            if l > i:
                _cmp_exchange(rows, i, l)
        j //= 2


def _bitonic_sort_desc(rows):
    n = len(rows)
    k = 2
    while k <= n:
        j = k // 2
        while j >= 1:
            for i in range(n):
                l = i ^ j
                if l > i:
                    if (i & k) == 0:
                        _cmp_exchange(rows, i, l)
                    else:
                        _cmp_exchange(rows, l, i)
            j //= 2
        k *= 2


def _top_of_union(a, b):
    n = len(a)
    out = []
    for i in range(n):
        x, y = a[i], b[n - 1 - i]
        out.append(x if y is None else (y if x is None else jnp.maximum(x, y)))
    _bitonic_merge_desc(out)
    return out


def _top16_rows(s):
    l = s.shape[1]
    s3 = s.reshape(N_KEYS // SUBLANES, SUBLANES, l)
    rows = [s3[i] for i in range(N_KEYS // SUBLANES)]
    _bitonic_sort_desc(rows)
    for shift in (4, 2, 1):
        other = [pltpu.roll(r, shift, axis=0) for r in rows]
        rows = _top_of_union(rows, other)
    return rows


_CANDIDATES = [(i, j) for i in range(PEER_TOPK) for j in range(PEER_TOPK)
               if (i + 1) * (j + 1) <= PEER_TOPK]


def _bf16_bits(x):
    bits = lax.bitcast_convert_type(x, jnp.uint32)
    return (bits + jnp.uint32(0x7FFF) + ((bits >> 16) & jnp.uint32(1))) >> 16


def _bf16_pair_words(x):
    hi = _bf16_bits(x)
    return lax.bitcast_convert_type((hi << 16) | hi, jnp.int32)


def _bf16_packed_rows(x):
    hi = _bf16_bits(x)
    half = x.shape[0] // 2
    return lax.bitcast_convert_type(hi[:half] | (hi[half:] << 16), jnp.int32)


def _route_kernel(hn_ref, wqt_ref, keys_ref, rank_ref, e2_ref, cnt_ref, w1_ref, qt_ref):
    hh = pl.program_id(1)

    @pl.when(hh == 0)
    def _():
        qt_ref[...] = lax.dot_general(wqt_ref[...], hn_ref[...], NT_DIMS,
                                      preferred_element_type=F32).astype(BF16)

    base = pl.multiple_of(hh * PEER_QDIM, PEER_QDIM)
    q1 = qt_ref[pl.ds(base, PEER_HALF), :]
    q2 = qt_ref[pl.ds(base + PEER_HALF, PEER_HALF), :]
    s1 = jnp.dot(keys_ref[0, 0], q1, preferred_element_type=F32)
    s2 = jnp.dot(keys_ref[0, 1], q2, preferred_element_type=F32)
    tb = s1.shape[1]
    lgs = tb // LANES
    a = _top16_rows(s1)
    b = _top16_rows(s2)

    sub = lax.broadcasted_iota(jnp.int32, (SUBLANES, LANES), 0)

    def compact(r):
        out = r[:, :LANES]
        for g in range(1, lgs):
            out = jnp.where(sub == g, r[:, g * LANES:(g + 1) * LANES], out)
        return out

    def expand(c):
        return jnp.concatenate(
            [jnp.broadcast_to(c[g:g + 1, :], (SUBLANES, LANES)) for g in range(lgs)], axis=1)

    ac = [compact(r) for r in a]
    bc = [compact(r) for r in b]
    cand = [ac[i] + bc[j] for (i, j) in _CANDIDATES]
    cand += [None] * (64 - len(cand))
    groups = []
    for gi in range(4):
        grp = cand[gi * 16:(gi + 1) * 16]
        _bitonic_sort_desc(grp)
        groups.append(grp)
    top = _top_of_union(_top_of_union(groups[0], groups[1]),
                        _top_of_union(groups[2], groups[3]))
    z = jnp.zeros_like(top[0])
    for tk in top:
        z = z + jnp.exp(tk - top[0])
    tau = expand(top[PEER_TOPK - 1])
    inv_z = expand(1.0 / z)
    full = lambda r: jnp.broadcast_to(r[0:1, :], (N_KEYS, tb))
    tau_f = full(tau)
    cnt_bits = jnp.zeros((N_KEYS, tb), jnp.int32)
    rank_bits = jnp.zeros((N_KEYS, tb), jnp.int32)
    for j in range(PEER_TOPK):
        bj = full(b[j])
        bits = int(np.asarray(j + 1, dtype=np.float32).view(np.uint32)) >> 16
        cnt_bits = jnp.where(s1 + bj >= tau_f, bits, cnt_bits)
        rank_bits = jnp.where(bj > s2, bits, rank_bits)
    e2 = jnp.exp(s2 - full(b[0]))
    w1 = jnp.exp(s1 - full(a[0])) * full(inv_z)
    half = N_KEYS // 2
    rank_w = rank_bits[:half] | (rank_bits[half:] << 16)
    e2_w = _bf16_packed_rows(e2)
    cnt_w = cnt_bits | (cnt_bits << 16)
    w1_w = _bf16_pair_words(w1)
    for lg in range(tb // LANES):
        lanes = slice(lg * LANES, (lg + 1) * LANES)
        rank_ref[lg, 0] = rank_w[:, lanes]
        e2_ref[lg, 0] = e2_w[:, lanes]
        cnt_ref[lg, 0] = cnt_w[:, lanes]
        w1_ref[lg, 0] = w1_w[:, lanes]


def _peer_route(hn, wqt, keys):
    n, d = hn.shape
    tb = min(ROUTE_TB, n)
    lgs = tb // LANES
    tab = jax.ShapeDtypeStruct((n // LANES, PEER_HEADS, N_KEYS, LANES), jnp.int32)
    tab16 = jax.ShapeDtypeStruct((n // LANES, PEER_HEADS, N_KEYS // 2, LANES), jnp.int32)
    tspec = pl.BlockSpec((lgs, 1, N_KEYS, LANES), lambda i, h: (i, h, 0, 0))
    tspec16 = pl.BlockSpec((lgs, 1, N_KEYS // 2, LANES), lambda i, h: (i, h, 0, 0))
    return pl.pallas_call(
        _route_kernel,
        grid=(n // tb, PEER_HEADS),
        in_specs=[
            pl.BlockSpec((tb, d), lambda i, h: (i, 0)),
            pl.BlockSpec(wqt.shape, lambda i, h: (0, 0), pipeline_mode=pl.Buffered(1)),
            pl.BlockSpec((1, 2, N_KEYS, PEER_HALF), lambda i, h: (h, 0, 0, 0)),
        ],
        out_specs=[tspec16, tspec16, tspec, tspec],
        out_shape=[tab16, tab16, tab, tab],
        scratch_shapes=[pltpu.VMEM((PEER_HEADS * PEER_QDIM, tb), BF16)],
        compiler_params=_params(("parallel", "arbitrary")),
        name="peer_route",
    )(hn, wqt, keys)


def _gelu(x):
    return 0.5 * x * (1.0 + lax.erf(x * (2.0 ** -0.5)))


def _peer_kernel(hn_ref, h_ref, u0_ref, u_ref, vt_ref, rank_ref, e2_ref, cnt_ref, w1_ref, fg_ref,
                 y_ref, acc_ref, hta_ref, htb_ref, w_ref):
    j = pl.program_id(1)
    nj = pl.num_programs(1)
    lane_groups, te, _ = hta_ref.shape
    rows_per_step = te // N_KEYS

    def score(u_tile_ref, dst_ref, hn=None):
        hn = hn_ref[...] if hn is None else hn
        ht = lax.dot_general(u_tile_ref[...], hn, NT_DIMS,
                             preferred_element_type=F32)
        for lg in range(lane_groups):
            dst_ref[lg] = ht[:, lg * LANES:(lg + 1) * LANES]

    def released_by(gates):
        hn = hn_ref[...]
        tb, d = hn.shape
        n_tiles, k_tiles = tb // MXU_DIM, d // MXU_DIM
        groups = MXU_DIM // BF16_ROWS
        assert len(gates) == n_tiles * k_tiles * groups
        rows = []
        for nt in range(n_tiles):
            for rg in range(groups):
                r0 = nt * MXU_DIM + rg * BF16_ROWS
                cols = []
                for kt in range(k_tiles):
                    g = gates[(nt * k_tiles + kt) * groups + rg]
                    z = g * jnp.zeros_like(g)
                    z = jnp.concatenate([z] * (MXU_DIM // LANES), axis=1)
                    cols.append(hn[r0:r0 + BF16_ROWS, kt * MXU_DIM:(kt + 1) * MXU_DIM] + z)
                rows.append(jnp.concatenate(cols, axis=1))
        return jnp.concatenate(rows, axis=0)

    @pl.when(j == 0)
    def _():
        acc_ref[...] = jnp.zeros_like(acc_ref)

    @pl.when((pl.program_id(0) == 0) & (j == 0))
    def _():
        score(u0_ref, hta_ref)

    def step(cur_ref, nxt_ref):
        zero = jnp.zeros((BF16_ROWS, LANES), BF16)
        gates = []

        def row_tile(ref, lg, hh, rr):
            words = jnp.broadcast_to(ref[lg, hh, rr:rr + 1, :], (SUBLANES, LANES))
            return pltpu.bitcast(words, BF16)

        for lg in range(lane_groups):
            lanes = slice(lg * LANES, (lg + 1) * LANES)
            for rr in range(rows_per_step):
                cnt = [row_tile(cnt_ref, lg, hh, rr) for hh in range(PEER_HEADS)]
                w1 = [row_tile(w1_ref, lg, hh, rr) for hh in range(PEER_HEADS)]
                for kk in range(N_KEYS // BF16_ROWS):
                    words = slice(kk * SUBLANES, (kk + 1) * SUBLANES)
                    rows = slice(rr * N_KEYS + kk * BF16_ROWS, rr * N_KEYS + (kk + 1) * BF16_ROWS)
                    gate = zero
                    for hh in range(PEER_HEADS):
                        rank = pltpu.bitcast(rank_ref[lg, hh, words, :], BF16)
                        e2 = pltpu.bitcast(e2_ref[lg, hh, words, :], BF16)
                        gate = gate + jnp.where(rank < cnt[hh], e2, zero) * w1[hh]
                    gates.append(gate)
                    w_ref[rows, lanes] = gate * _gelu(cur_ref[lg, rows, :]).astype(BF16)
        score(u_ref, nxt_ref, released_by(gates))
        acc_ref[...] += jnp.dot(vt_ref[...], w_ref[...], preferred_element_type=F32)

    @pl.when(j % 2 == 0)
    def _():
        step(hta_ref, htb_ref)

    @pl.when(j % 2 == 1)
    def _():
        step(htb_ref, hta_ref)

    @pl.when(j == nj - 1)
    def _():
        h = h_ref[...] + acc_ref[...].T
        y_ref[...] = _rms(h, fg_ref[...])


def _peer_dense(hn, h, u, vt, rank, e2, cnt, w1, fg):
    n, d = hn.shape
    e = u.shape[0]
    tb, te = PEER_TB, PEER_TE
    nj = e // te
    ni = n // tb
    assert nj % 2 == 0, "score buffers alternate with the step parity across token blocks"
    lgs = tb // LANES
    rows = te // N_KEYS
    once = pl.Buffered(1)
    tspec = pl.BlockSpec((lgs, PEER_HEADS, N_KEYS // 2, LANES), lambda i, j: (i, 0, 0, 0))
    rspec = pl.BlockSpec((lgs, PEER_HEADS, rows, LANES), lambda i, j: (i, 0, j, 0))
    return pl.pallas_call(
        _peer_kernel,
        grid=(ni, nj),
        in_specs=[
            pl.BlockSpec((tb, d), lambda i, j: (jnp.minimum(i + (j + 1) // nj, ni - 1), 0)),
            pl.BlockSpec((tb, d), lambda i, j: (i, 0)),
            pl.BlockSpec((te, d), lambda i, j: (0, 0), pipeline_mode=once),
            pl.BlockSpec((te, d), lambda i, j: ((j + 1) % nj, 0)),
            pl.BlockSpec((None, d, te), lambda i, j: (j, 0, 0)),
            tspec, tspec, rspec, rspec,
            pl.BlockSpec((1, d), lambda i, j: (0, 0)),
        ],
        out_specs=pl.BlockSpec((tb, d), lambda i, j: (i, 0)),
        out_shape=jax.ShapeDtypeStruct((n, d), F32),
        scratch_shapes=[
            pltpu.VMEM((d, tb), F32),
            pltpu.VMEM((lgs, te, LANES), F32),
            pltpu.VMEM((lgs, te, LANES), F32),
            pltpu.VMEM((te, tb), BF16),
        ],
        compiler_params=_params(("arbitrary", "arbitrary")),
        name="peer_dense",
    )(hn, h, u, u, vt, rank, e2, cnt, w1, fg)


def _rope_tables(t):
    n_rows = t // GRID_W
    row = jnp.repeat(jnp.arange(n_rows, dtype=F32), GRID_W)
    col = jnp.tile(jnp.arange(GRID_W, dtype=F32), n_rows)
    freqs = ROPE_THETA ** (-jnp.arange(0, AXIS_DIM, 2, dtype=F32) / AXIS_DIM)
    ang = jnp.concatenate([row[:, None] * freqs, col[:, None] * freqs], axis=-1)
    cos, sin = jnp.cos(ang), jnp.sin(ang)
    return jnp.concatenate([cos, cos], axis=-1), jnp.concatenate([-sin, sin], axis=-1)


def _prepare(norm_mix_g, w_in, q_norm_g, k_norm_g, w_gate_f2, b_gate_f, w_gate_b2, b_gate_b,
             gla_norm_g, attn_out_norm_g, w_out, norm_ffn_g, peer_w_q, peer_sub_keys, peer_u,
             peer_v, final_norm_g):
    perm = np.concatenate([np.arange(0, HEAD_DIM, 2), np.arange(1, HEAD_DIM, 2)])
    kw = GLA_HEADS * GLA_DK
    o_k = ATTN_WIDTH
    o_v = o_k + KV_WIDTH
    o_g = o_v + KV_WIDTH
    o_lr = o_g + 2 * kw + 2 * GLA_WIDTH
    wq = w_in[:, :o_k].reshape(D_MODEL, N_HEADS, HEAD_DIM)[:, :, perm].reshape(D_MODEL, o_k)
    wk = w_in[:, o_k:o_v].reshape(D_MODEL, N_KV_HEADS, HEAD_DIM)[:, :, perm].reshape(D_MODEL, KV_WIDTH)
    w_qkv = jnp.concatenate([wq, wk, w_in[:, o_v:o_g]], axis=1).astype(BF16)
    w_lr = jnp.pad(w_in[:, o_lr:], ((0, 0), (0, LANES - 2 * GATE_RANK)))
    w_gla = jnp.concatenate([w_in[:, o_g:o_lr], w_lr], axis=1).astype(BF16)
    wg = jnp.zeros((LANES, 2 * kw), F32)
    wg = wg.at[:GATE_RANK, :kw].set(w_gate_f2).at[GATE_RANK:2 * GATE_RANK, kw:].set(w_gate_b2)
    bg = jnp.concatenate([b_gate_f, b_gate_b])[None, :]

    keys = jnp.stack([peer_sub_keys[:, 0], peer_sub_keys[:, 1][:, PEER_KEY2_ROWS]], axis=1)
    return dict(
        norm_mix_g=norm_mix_g[None, :], w_qkv=w_qkv, w_gla=w_gla, wg=wg, bg=bg,
        qg=q_norm_g[perm][None, :], kg=k_norm_g[perm][None, :],
        gla_gain=gla_norm_g[None, :], attn_gain=attn_out_norm_g[None, :],
        wo_a=w_out[:ATTN_WIDTH].astype(BF16), wo_g=w_out[ATTN_WIDTH:].astype(BF16),
        ffn_g=norm_ffn_g[None, :], wqt=peer_w_q.T.astype(BF16),
        keys=keys.astype(BF16), u=peer_u.astype(BF16),
        vt=peer_v.astype(BF16).reshape(-1, PEER_TE, D_MODEL).transpose(0, 2, 1),
        final_g=final_norm_g[None, :],
    )


def _layer(x, p):
    b, t, d = x.shape
    n = b * t
    x2 = x.reshape(n, d)
    cos, sin = _rope_tables(t)
    q, k, v = _qkv_proj(x2, p["norm_mix_g"], p["w_qkv"], p["qg"], p["kg"], cos, sin, t)
    gq, gk, gv, sg, laf, lab = _gla_proj(x2, p["norm_mix_g"], p["w_gla"], p["wg"], p["bg"])
    oa = _attention(q.reshape(b, t, -1), k.reshape(b, t, -1), v.reshape(b, t, -1))
    r3 = lambda a: a.reshape(b, t, -1)
    og = _gla(r3(gq), r3(gk), r3(gv), r3(sg), r3(laf), r3(lab), p["gla_gain"])
    h, hn = _out_proj(x2, oa.reshape(n, -1), og.reshape(n, -1), p["attn_gain"],
                      p["wo_a"], p["wo_g"], p["ffn_g"])
    rank, e2, cnt, w1 = _peer_route(hn, p["wqt"], p["keys"])
    y = _peer_dense(hn, h, p["u"], p["vt"], rank, e2, cnt, w1, p["final_g"])
    return y.reshape(b, t, d)


def kernel(x_prompt, x_sample, norm_mix_g, w_in, q_norm_g, k_norm_g, w_gate_f2, b_gate_f,
           w_gate_b2, b_gate_b, gla_norm_g, attn_out_norm_g, w_out, norm_ffn_g, peer_w_q,
           peer_sub_keys, peer_u, peer_v, final_norm_g):
    p = _prepare(norm_mix_g, w_in, q_norm_g, k_norm_g, w_gate_f2, b_gate_f, w_gate_b2,
                 b_gate_b, gla_norm_g, attn_out_norm_g, w_out, norm_ffn_g, peer_w_q,
                 peer_sub_keys, peer_u, peer_v, final_norm_g)
    return (_layer(x_prompt, p), _layer(x_sample, p))
```

```python
import functools
import math

import jax
import jax.numpy as jnp
import numpy as np
from jax import lax
from jax.experimental import pallas as pl
from jax.experimental.pallas import tpu as pltpu

F32 = jnp.float32
BF16 = jnp.bfloat16

D_MODEL = 2048
GRID_W = 64
HEAD_DIM = 128
N_HEADS = 8
N_KV_HEADS = 2
GQA_GROUP = N_HEADS // N_KV_HEADS
ATTN_WIDTH = N_HEADS * HEAD_DIM
KV_WIDTH = N_KV_HEADS * HEAD_DIM
ROPE_THETA = 10000.0
AXIS_DIM = HEAD_DIM // 2
GLA_HEADS = 4
GLA_DK = 128
GLA_DV = 256
GLA_WIDTH = GLA_HEADS * GLA_DV
GATE_RANK = 16
GATE_NORMALIZER = 16.0
PEER_HEADS = 8
N_KEYS = 128
PEER_TOPK = 16
PEER_HALF = 128
PEER_QDIM = 2 * PEER_HALF
EPS = 1e-6
LOG2_E = math.log2(math.e)

LANES = 128
SUBLANES = 8
BF16_ROWS = 16
MXU_DIM = 256
VMEM_LIMIT = 58 * 1024 * 1024

PROJ_TM = 512
PROJ_COLS = 512
ATTN_TQ = 256
GLA_CHUNK = 64
GLA_UNROLL = 16
ROUTE_TB = 512
ROUTE_HEADS = 2
PEER_TB = 512
PEER_TE = 1024

PEER_KEY2_ROWS = np.concatenate([np.arange(0, N_KEYS, 2), np.arange(1, N_KEYS, 2)])

NT_DIMS = (((1,), (1,)), ((), ()))
TN_DIMS = (((0,), (0,)), ((), ()))


def _params(sem, vmem=VMEM_LIMIT, flags=None):
    return pltpu.CompilerParams(dimension_semantics=sem, vmem_limit_bytes=vmem, flags=flags)


def _rms(x, g):
    return x * lax.rsqrt(jnp.mean(x * x, axis=-1, keepdims=True) + EPS) * g


def _qkv_kernel(x_ref, g_ref, w_ref, qg_ref, kg_ref, cos_ref, sin_ref, q_ref, k_ref, v_ref):
    xn = _rms(x_ref[...], g_ref[...]).astype(BF16)
    z = jnp.dot(xn, w_ref[...], preferred_element_type=F32)
    c = cos_ref[...]
    s = sin_ref[...]

    def norm_rope(hd, gain, scale):
        hn = _rms(hd, gain)
        rot = pltpu.roll(hn, HEAD_DIM // 2, axis=1)
        return ((hn * c + rot * s) * scale).astype(BF16)

    for hh in range(N_HEADS):
        sl = slice(hh * HEAD_DIM, (hh + 1) * HEAD_DIM)
        q_ref[:, sl] = norm_rope(z[:, sl], qg_ref[...], HEAD_DIM ** -0.5 * LOG2_E)
    for hh in range(N_KV_HEADS):
        sl = slice(hh * HEAD_DIM, (hh + 1) * HEAD_DIM)
        zsl = slice(ATTN_WIDTH + hh * HEAD_DIM, ATTN_WIDTH + (hh + 1) * HEAD_DIM)
        k_ref[:, sl] = norm_rope(z[:, zsl], kg_ref[...], 1.0)
    v_ref[...] = z[:, ATTN_WIDTH + KV_WIDTH:].astype(BF16)


def _qkv_proj(x2, g, w, qg, kg, cos, sin, seq):
    n, d = x2.shape
    tm = min(PROJ_TM, seq)
    nt = seq // tm
    wq = w.shape[1]
    full = lambda i: (0, 0)
    return pl.pallas_call(
        _qkv_kernel,
        grid=(n // tm,),
        in_specs=[
            pl.BlockSpec((tm, d), lambda i: (i, 0)),
            pl.BlockSpec((1, d), full),
            pl.BlockSpec((d, wq), full),
            pl.BlockSpec((1, HEAD_DIM), full),
            pl.BlockSpec((1, HEAD_DIM), full),
            pl.BlockSpec((tm, HEAD_DIM), lambda i: (i % nt, 0)),
            pl.BlockSpec((tm, HEAD_DIM), lambda i: (i % nt, 0)),
        ],
        out_specs=[
            pl.BlockSpec((tm, ATTN_WIDTH), lambda i: (i, 0)),
            pl.BlockSpec((tm, KV_WIDTH), lambda i: (i, 0)),
            pl.BlockSpec((tm, KV_WIDTH), lambda i: (i, 0)),
        ],
        out_shape=[
            jax.ShapeDtypeStruct((n, ATTN_WIDTH), BF16),
            jax.ShapeDtypeStruct((n, KV_WIDTH), BF16),
            jax.ShapeDtypeStruct((n, KV_WIDTH), BF16),
        ],
        compiler_params=_params(("parallel",)),
        name="qkv_proj",
    )(x2, g, w, qg, kg, cos, sin)


def _log_sigmoid(x):
    return jnp.minimum(x, 0.0) - jnp.log(1.0 + jnp.exp(-jnp.abs(x)))


def _gla_proj_kernel(x_ref, g_ref, w_ref, wg_ref, bg_ref,
                     gq_ref, gk_ref, gv_ref, sg_ref, laf_ref, lab_ref):
    xn = _rms(x_ref[...], g_ref[...]).astype(BF16)
    kw = GLA_HEADS * GLA_DK
    assert kw == PROJ_COLS and GLA_WIDTH == 2 * PROJ_COLS

    def project(g):
        cols = slice(g * PROJ_COLS, (g + 1) * PROJ_COLS)
        return jnp.dot(xn, w_ref[:, cols], preferred_element_type=F32)

    def silu(x):
        return x * (1.0 / (1.0 + jnp.exp(-x)))

    outputs = [
        (gq_ref, 0, lambda z: z * (GLA_DK ** -0.5)),
        (gk_ref, 0, lambda z: z),
        (gv_ref, 0, lambda z: z),
        (gv_ref, PROJ_COLS, lambda z: z),
        (sg_ref, 0, silu),
        (sg_ref, PROJ_COLS, silu),
    ]
    lr = jnp.dot(xn, w_ref[:, len(outputs) * PROJ_COLS:], preferred_element_type=F32)
    pre = jnp.dot(lr, wg_ref[...], preferred_element_type=F32,
                  precision=lax.Precision.HIGHEST) + bg_ref[...]
    z = project(0)
    la = _log_sigmoid(pre) * (1.0 / GATE_NORMALIZER)
    laf_ref[...] = la[:, :kw]
    lab_ref[...] = la[:, kw:]
    for g, (ref, col, fn) in enumerate(outputs):
        z_next = project(g + 1) if g + 1 < len(outputs) else None
        ref[:, col:col + PROJ_COLS] = fn(z).astype(BF16)
        z = z_next


def _gla_proj(x2, g, w, wg, bg):
    n, d = x2.shape
    tm = PROJ_TM
    kw = GLA_HEADS * GLA_DK
    full = lambda i: (0, 0)
    row = lambda i: (i, 0)
    return pl.pallas_call(
        _gla_proj_kernel,
        grid=(n // tm,),
        in_specs=[
            pl.BlockSpec((tm, d), row),
            pl.BlockSpec((1, d), full),
            pl.BlockSpec(w.shape, full),
            pl.BlockSpec(wg.shape, full),
            pl.BlockSpec(bg.shape, full),
        ],
        out_specs=[
            pl.BlockSpec((tm, kw), row),
            pl.BlockSpec((tm, kw), row),
            pl.BlockSpec((tm, GLA_WIDTH), row),
            pl.BlockSpec((tm, GLA_WIDTH), row),
            pl.BlockSpec((tm, kw), row),
            pl.BlockSpec((tm, kw), row),
        ],
        out_shape=[
            jax.ShapeDtypeStruct((n, kw), BF16),
            jax.ShapeDtypeStruct((n, kw), BF16),
            jax.ShapeDtypeStruct((n, GLA_WIDTH), BF16),
            jax.ShapeDtypeStruct((n, GLA_WIDTH), BF16),
            jax.ShapeDtypeStruct((n, kw), F32),
            jax.ShapeDtypeStruct((n, kw), F32),
        ],
        compiler_params=_params(("parallel",)),
        name="gla_proj",
    )(x2, g, w, wg, bg)


def _attn_kernel(q_ref, k_ref, v_ref, o_ref):
    k = k_ref[0]
    v = v_ref[0]

    def scores(hh):
        sl = slice(hh * HEAD_DIM, (hh + 1) * HEAD_DIM)
        return lax.dot_general(q_ref[0, :, sl], k, NT_DIMS, preferred_element_type=F32)

    s = scores(0)
    for hh in range(GQA_GROUP):
        s_next = scores(hh + 1) if hh + 1 < GQA_GROUP else None
        m = jnp.max(s, axis=-1, keepdims=True)
        p = jnp.exp2(s - m)
        l = jnp.sum(p, axis=-1, keepdims=True)
        o = jnp.dot(p.astype(BF16), v, preferred_element_type=F32)
        o_ref[0, :, hh * HEAD_DIM:(hh + 1) * HEAD_DIM] = (o / l).astype(BF16)
        s = s_next


def _attention(q, k, v):
    b, t, _ = q.shape
    tq = min(ATTN_TQ, t)
    gw = GQA_GROUP * HEAD_DIM
    return pl.pallas_call(
        _attn_kernel,
        grid=(b, N_KV_HEADS, t // tq),
        in_specs=[
            pl.BlockSpec((1, tq, gw), lambda bi, g, qi: (bi, qi, g)),
            pl.BlockSpec((1, t, HEAD_DIM), lambda bi, g, qi: (bi, 0, g)),
            pl.BlockSpec((1, t, HEAD_DIM), lambda bi, g, qi: (bi, 0, g)),
        ],
        out_specs=pl.BlockSpec((1, tq, gw), lambda bi, g, qi: (bi, qi, g)),
        out_shape=jax.ShapeDtypeStruct((b, t, ATTN_WIDTH), BF16),
        compiler_params=_params(("parallel", "parallel", "arbitrary")),
        name="gqa_attention",
    )(q, k, v)


def _split_dot(tri, g):
    hi = g.astype(BF16)
    lo = (g - hi.astype(F32)).astype(BF16)
    return (jnp.dot(tri, hi, preferred_element_type=F32)
            + jnp.dot(tri, lo, preferred_element_type=F32))


def _gla_kernel(q_ref, k_ref, v_ref, sg_ref, gf_ref, gb_ref, gain_ref, o_ref, acc_ref,
                sf_ref, sb_ref):
    t = q_ref.shape[1]
    c = GLA_CHUNK
    n = t // c
    mid = c // 2
    ri = lax.broadcasted_iota(jnp.int32, (c, c), 0)
    ci = lax.broadcasted_iota(jnp.int32, (c, c), 1)
    lower = ri >= ci
    tri_lo = lower.astype(BF16)
    tri_up = (ri <= ci).astype(BF16)
    sf_ref[...] = jnp.zeros_like(sf_ref)
    sb_ref[...] = jnp.zeros_like(sb_ref)

    upper = ri <= ci
    un = min(GLA_UNROLL, n // 2)

    def finish(rows, o):
        y = _rms(o, gain_ref[...]) * sg_ref[0, rows, :].astype(F32)
        o_ref[0, rows, :] = y.astype(BF16)

    def trip(i, second_half):
        items = []
        for u in range(un):
            idx = i * un + u
            items.append((idx, sf_ref, gf_ref, tri_lo, c - 1, lower))
            items.append((n - 1 - idx, sb_ref, gb_ref, tri_up, 0, upper))
        rows = [pl.ds(pl.multiple_of(it[0] * c, c), c) for it in items]
        bs = [_split_dot(it[3], it[2][0, r, :]) for it, r in zip(items, rows)]
        qs, ks, scores = [], [], []
        for it, r, b in zip(items, rows, bs):
            q = q_ref[0, r, :].astype(F32)
            k = k_ref[0, r, :].astype(F32)
            ref = b[mid:mid + 1, :]
            qt = (q * jnp.exp(b - ref)).astype(BF16)
            kt = (k * jnp.exp(ref - b)).astype(BF16)
            a = lax.dot_general(qt, kt, NT_DIMS, preferred_element_type=F32)
            qs.append(q)
            ks.append(k)
            scores.append(jnp.where(it[5], a, 0.0).astype(BF16))
        intra, updates, tots = [], [], []
        for it, r, b, k, a in zip(items, rows, bs, ks, scores):
            v = v_ref[0, r, :]
            tot = b[it[4]:it[4] + 1, :]
            kd = (k * jnp.exp(tot - b)).astype(BF16)
            intra.append(jnp.dot(a, v, preferred_element_type=F32))
            updates.append(lax.dot_general(v, kd, TN_DIMS, preferred_element_type=F32))
            tots.append(tot)
        for it, r, b, q, o_in, upd, tot in zip(items, rows, bs, qs, intra, updates, tots):
            s_ref = it[1]
            st = s_ref[...]
            o = o_in + lax.dot_general((q * jnp.exp(b)).astype(BF16), st.astype(BF16), NT_DIMS,
                                       preferred_element_type=F32)
            s_ref[...] = st * jnp.exp(tot) + upd
            if second_half:
                finish(r, acc_ref[r, :] + o)
            else:
                acc_ref[r, :] = o

    half_trips = n // 2 // un

    def first_half(i, carry):
        trip(i, False)
        return carry

    def second_half(i, carry):
        trip(i, True)
        return carry

    lax.fori_loop(0, half_trips, first_half, 0)
    lax.fori_loop(half_trips, 2 * half_trips, second_half, 0)


def _gla(gq, gk, gv, sg, laf, lab, gain):
    b, t, _ = gq.shape
    kspec = pl.BlockSpec((1, t, GLA_DK), lambda bi, h: (bi, 0, h))
    vspec = pl.BlockSpec((1, t, GLA_DV), lambda bi, h: (bi, 0, h))
    return pl.pallas_call(
        _gla_kernel,
        grid=(b, GLA_HEADS),
        in_specs=[kspec, kspec, vspec, vspec, kspec, kspec,
                  pl.BlockSpec((1, GLA_DV), lambda bi, h: (0, 0))],
        out_specs=vspec,
        out_shape=jax.ShapeDtypeStruct((b, t, GLA_WIDTH), BF16),
        scratch_shapes=[
            pltpu.VMEM((t, GLA_DV), F32),
            pltpu.VMEM((GLA_DV, GLA_DK), F32),
            pltpu.VMEM((GLA_DV, GLA_DK), F32),
        ],
        compiler_params=_params(("parallel", "parallel")),
        name="gla_bidir",
    )(gq, gk, gv, sg, laf, lab, gain)


def _out_proj_kernel(x_ref, oa_ref, og_ref, ag_ref, wa_ref, wg_ref, fg_ref, h_ref, hn_ref):
    oa = _rms(oa_ref[...].astype(F32), ag_ref[...]).astype(BF16)
    h = (x_ref[...]
         + jnp.dot(oa, wa_ref[...], preferred_element_type=F32)
         + jnp.dot(og_ref[...], wg_ref[...], preferred_element_type=F32))
    h_ref[...] = h
    hn_ref[...] = _rms(h, fg_ref[...]).astype(BF16)


def _out_proj(x2, oa, og, ag, wa, wg, fg):
    n, d = x2.shape
    tm = PROJ_TM
    full = lambda i: (0, 0)
    row = lambda i: (i, 0)
    return pl.pallas_call(
        _out_proj_kernel,
        grid=(n // tm,),
        in_specs=[
            pl.BlockSpec((tm, d), row),
            pl.BlockSpec((tm, ATTN_WIDTH), row),
            pl.BlockSpec((tm, GLA_WIDTH), row),
            pl.BlockSpec((1, ATTN_WIDTH), full),
            pl.BlockSpec(wa.shape, full),
            pl.BlockSpec(wg.shape, full),
            pl.BlockSpec((1, d), full),
        ],
        out_specs=[pl.BlockSpec((tm, d), row), pl.BlockSpec((tm, d), row)],
        out_shape=[jax.ShapeDtypeStruct((n, d), F32), jax.ShapeDtypeStruct((n, d), BF16)],
        compiler_params=_params(("parallel",)),
        name="out_proj",
    )(x2, oa, og, ag, wa, wg, fg)


def _cmp_exchange(rows, i, j):
    a, b = rows[i], rows[j]
    if b is None:
        return
    if a is None:
        rows[i], rows[j] = b, None
        return
    rows[i], rows[j] = jnp.maximum(a, b), jnp.minimum(a, b)


def _bitonic_merge_desc(rows):
    n = len(rows)
    j = n // 2
    while j >= 1:
        for i in range(n):
            l = i ^ j
            if l > i:
                _cmp_exchange(rows, i, l)
        j //= 2


def _bitonic_sort_desc(rows):
    n = len(rows)
    k = 2
    while k <= n:
        j = k // 2
        while j >= 1:
            for i in range(n):
                l = i ^ j
                if l > i:
                    if (i & k) == 0:
                        _cmp_exchange(rows, i, l)
                    else:
                        _cmp_exchange(rows, l, i)
            j //= 2
        k *= 2


def _top_of_union(a, b):
    n = len(a)
    out = []
    for i in range(n):
        x, y = a[i], b[n - 1 - i]
        out.append(x if y is None else (y if x is None else jnp.maximum(x, y)))
    _bitonic_merge_desc(out)
    return out


def _top16_rows(s):
    l = s.shape[1]
    s3 = s.reshape(N_KEYS // SUBLANES, SUBLANES, l)
    rows = [s3[i] for i in range(N_KEYS // SUBLANES)]
    _bitonic_sort_desc(rows)
    for shift in (4, 2, 1):
        other = [pltpu.roll(r, shift, axis=0) for r in rows]
        rows = _top_of_union(rows, other)
    return rows


_CANDIDATES = [(i, j) for i in range(PEER_TOPK) for j in range(PEER_TOPK)
               if (i + 1) * (j + 1) <= PEER_TOPK]


def _bf16_bits(x):
    bits = lax.bitcast_convert_type(x, jnp.uint32)
    return (bits + jnp.uint32(0x7FFF) + ((bits >> 16) & jnp.uint32(1))) >> 16


def _bf16_pair_words(x):
    hi = _bf16_bits(x)
    return lax.bitcast_convert_type((hi << 16) | hi, jnp.int32)


def _bf16_packed_rows(x):
    hi = _bf16_bits(x)
    half = x.shape[0] // 2
    return lax.bitcast_convert_type(hi[:half] | (hi[half:] << 16), jnp.int32)


def _route_kernel(hn_ref, wqt_ref, keys_ref, sel_ref, row_ref, qt_ref):
    step = pl.program_id(1)

    @pl.when(step == 0)
    def _():
        qt_ref[...] = lax.dot_general(wqt_ref[...], hn_ref[...], NT_DIMS,
                                      preferred_element_type=F32).astype(BF16)

    for k in range(ROUTE_HEADS):
        _route_head(step * ROUTE_HEADS + k, k, keys_ref, sel_ref, row_ref, qt_ref)


def _route_head(hh, k, keys_ref, sel_ref, row_ref, qt_ref):
    base = pl.multiple_of(hh * PEER_QDIM, PEER_QDIM)
    q1 = qt_ref[pl.ds(base, PEER_HALF), :]
    q2 = qt_ref[pl.ds(base + PEER_HALF, PEER_HALF), :]
    s1 = jnp.dot(keys_ref[k, 0], q1, preferred_element_type=F32)
    s2 = jnp.dot(keys_ref[k, 1], q2, preferred_element_type=F32)
    tb = s1.shape[1]
    lgs = tb // LANES
    a = _top16_rows(s1)
    b = _top16_rows(s2)

    sub = lax.broadcasted_iota(jnp.int32, (SUBLANES, LANES), 0)

    def compact(r):
        out = r[:, :LANES]
        for g in range(1, lgs):
            out = jnp.where(sub == g, r[:, g * LANES:(g + 1) * LANES], out)
        return out

    def expand(c):
        return jnp.concatenate(
            [jnp.broadcast_to(c[g:g + 1, :], (SUBLANES, LANES)) for g in range(lgs)], axis=1)

    ac = [compact(r) for r in a]
    bc = [compact(r) for r in b]
    cand = [ac[i] + bc[j] for (i, j) in _CANDIDATES]
    cand += [None] * (64 - len(cand))
    groups = []
    for gi in range(4):
        grp = cand[gi * 16:(gi + 1) * 16]
        _bitonic_sort_desc(grp)
        groups.append(grp)
    top = _top_of_union(_top_of_union(groups[0], groups[1]),
                        _top_of_union(groups[2], groups[3]))
    z = jnp.zeros_like(top[0])
    for tk in top:
        z = z + jnp.exp(tk - top[0])
    tau = expand(top[PEER_TOPK - 1])
    inv_z = expand(1.0 / z)
    full = lambda r: jnp.broadcast_to(r[0:1, :], (N_KEYS, tb))
    tau_f = full(tau)
    cnt_bits = jnp.zeros((N_KEYS, tb), jnp.int32)
    rank_bits = jnp.zeros((N_KEYS, tb), jnp.int32)
    for j in range(PEER_TOPK):
        bj = full(b[j])
        bits = int(np.asarray(j + 1, dtype=np.float32).view(np.uint32)) >> 16
        cnt_bits = jnp.where(s1 + bj >= tau_f, bits, cnt_bits)
        rank_bits = jnp.where(bj > s2, bits, rank_bits)
    e2 = jnp.exp(s2 - full(b[0]))
    w1 = jnp.exp(s1 - full(a[0])) * full(inv_z)
    half = N_KEYS // 2
    rank_w = rank_bits[:half] | (rank_bits[half:] << 16)
    e2_w = _bf16_packed_rows(e2)
    cnt_w = cnt_bits | (cnt_bits << 16)
    w1_w = _bf16_pair_words(w1)
    for lg in range(tb // LANES):
        lanes = slice(lg * LANES, (lg + 1) * LANES)
        sel_ref[lg, k, 0] = rank_w[:, lanes]
        sel_ref[lg, k, 1] = e2_w[:, lanes]
        row_ref[lg, k, 0] = cnt_w[:, lanes]
        row_ref[lg, k, 1] = w1_w[:, lanes]


def _peer_route(hn, wqt, keys):
    n, d = hn.shape
    tb = min(ROUTE_TB, n)
    lgs = tb // LANES
    rows = jax.ShapeDtypeStruct((n // LANES, PEER_HEADS, 2, N_KEYS, LANES), jnp.int32)
    sels = jax.ShapeDtypeStruct((n // LANES, PEER_HEADS, 2, N_KEYS // 2, LANES), jnp.int32)
    rspec = pl.BlockSpec((lgs, ROUTE_HEADS, 2, N_KEYS, LANES), lambda i, h: (i, h, 0, 0, 0))
    sspec = pl.BlockSpec((lgs, ROUTE_HEADS, 2, N_KEYS // 2, LANES), lambda i, h: (i, h, 0, 0, 0))
    return pl.pallas_call(
        _route_kernel,
        grid=(n // tb, PEER_HEADS // ROUTE_HEADS),
        in_specs=[
            pl.BlockSpec((tb, d), lambda i, h: (i, 0)),
            pl.BlockSpec(wqt.shape, lambda i, h: (0, 0), pipeline_mode=pl.Buffered(1)),
            pl.BlockSpec((ROUTE_HEADS, 2, N_KEYS, PEER_HALF), lambda i, h: (h, 0, 0, 0)),
        ],
        out_specs=[sspec, rspec],
        out_shape=[sels, rows],
        scratch_shapes=[pltpu.VMEM((PEER_HEADS * PEER_QDIM, tb), BF16)],
        compiler_params=_params(("parallel", "arbitrary")),
        name="peer_route",
    )(hn, wqt, keys)


def _gelu(x):
    return 0.5 * x * (1.0 + lax.erf(x * (2.0 ** -0.5)))


def _peer_kernel(hn_ref, h_ref, u0_ref, u_ref, vt_ref, sel_ref, row_ref, fg_ref,
                 y_ref, acc_ref, hta_ref, htb_ref, w_ref):
    j = pl.program_id(1)
    nj = pl.num_programs(1)
    lane_groups, te, _ = hta_ref.shape
    rows_per_step = te // N_KEYS

    def score(u_tile_ref, dst_ref, hn=None):
        hn = hn_ref[...] if hn is None else hn
        ht = lax.dot_general(u_tile_ref[...], hn, NT_DIMS,
                             preferred_element_type=F32)
        for lg in range(lane_groups):
            dst_ref[lg] = ht[:, lg * LANES:(lg + 1) * LANES]

    def released_by(gates):
        hn = hn_ref[...]
        tb, d = hn.shape
        n_tiles, k_tiles = tb // MXU_DIM, d // MXU_DIM
        groups = MXU_DIM // BF16_ROWS
        assert len(gates) == n_tiles * k_tiles * groups
        rows = []
        for nt in range(n_tiles):
            for rg in range(groups):
                r0 = nt * MXU_DIM + rg * BF16_ROWS
                cols = []
                for kt in range(k_tiles):
                    g = gates[(nt * k_tiles + kt) * groups + rg]
                    z = g * jnp.zeros_like(g)
                    z = jnp.concatenate([z] * (MXU_DIM // LANES), axis=1)
                    cols.append(hn[r0:r0 + BF16_ROWS, kt * MXU_DIM:(kt + 1) * MXU_DIM] + z)
                rows.append(jnp.concatenate(cols, axis=1))
        return jnp.concatenate(rows, axis=0)

    @pl.when(j == 0)
    def _():
        acc_ref[...] = jnp.zeros_like(acc_ref)

    @pl.when((pl.program_id(0) == 0) & (j == 0))
    def _():
        score(u0_ref, hta_ref)

    def step(cur_ref, nxt_ref):
        zero = jnp.zeros((BF16_ROWS, LANES), BF16)
        gates = []

        def row_tile(which, lg, hh, rr):
            words = jnp.broadcast_to(row_ref[lg, hh, which, rr:rr + 1, :], (SUBLANES, LANES))
            return pltpu.bitcast(words, BF16)

        for lg in range(lane_groups):
            lanes = slice(lg * LANES, (lg + 1) * LANES)
            for rr in range(rows_per_step):
                cnt = [row_tile(0, lg, hh, rr) for hh in range(PEER_HEADS)]
                w1 = [row_tile(1, lg, hh, rr) for hh in range(PEER_HEADS)]
                for kk in range(N_KEYS // BF16_ROWS):
                    words = slice(kk * SUBLANES, (kk + 1) * SUBLANES)
                    rows = slice(rr * N_KEYS + kk * BF16_ROWS, rr * N_KEYS + (kk + 1) * BF16_ROWS)
                    gate = zero
                    for hh in range(PEER_HEADS):
                        rank = pltpu.bitcast(sel_ref[lg, hh, 0, words, :], BF16)
                        e2 = pltpu.bitcast(sel_ref[lg, hh, 1, words, :], BF16)
                        gate = gate + jnp.where(rank < cnt[hh], e2, zero) * w1[hh]
                    gates.append(gate)
                    w_ref[rows, lanes] = gate * _gelu(cur_ref[lg, rows, :]).astype(BF16)
        score(u_ref, nxt_ref, released_by(gates))
        acc_ref[...] += jnp.dot(vt_ref[...], w_ref[...], preferred_element_type=F32)

    @pl.when(j % 2 == 0)
    def _():
        step(hta_ref, htb_ref)

    @pl.when(j % 2 == 1)
    def _():
        step(htb_ref, hta_ref)

    @pl.when(j == nj - 1)
    def _():
        h = h_ref[...] + acc_ref[...].T
        y_ref[...] = _rms(h, fg_ref[...])


def _peer_dense(hn, h, u, vt, sel, row, fg):
    n, d = hn.shape
    e = u.shape[0]
    tb, te = PEER_TB, PEER_TE
    nj = e // te
    ni = n // tb
    assert nj % 2 == 0, "score buffers alternate with the step parity across token blocks"
    lgs = tb // LANES
    rows = te // N_KEYS
    once = pl.Buffered(1)
    tspec = pl.BlockSpec((lgs, PEER_HEADS, 2, N_KEYS // 2, LANES), lambda i, j: (i, 0, 0, 0, 0))
    rspec = pl.BlockSpec((lgs, PEER_HEADS, 2, rows, LANES), lambda i, j: (i, 0, 0, j, 0))
    return pl.pallas_call(
        _peer_kernel,
        grid=(ni, nj),
        in_specs=[
            pl.BlockSpec((tb, d), lambda i, j: (jnp.minimum(i + (j + 1) // nj, ni - 1), 0)),
            pl.BlockSpec((tb, d), lambda i, j: (i, 0)),
            pl.BlockSpec((te, d), lambda i, j: (0, 0), pipeline_mode=once),
            pl.BlockSpec((te, d), lambda i, j: ((j + 1) % nj, 0)),
            pl.BlockSpec((None, d, te), lambda i, j: (j, 0, 0)),
            tspec, rspec,
            pl.BlockSpec((1, d), lambda i, j: (0, 0)),
        ],
        out_specs=pl.BlockSpec((tb, d), lambda i, j: (i, 0)),
        out_shape=jax.ShapeDtypeStruct((n, d), F32),
        scratch_shapes=[
            pltpu.VMEM((d, tb), F32),
            pltpu.VMEM((lgs, te, LANES), F32),
            pltpu.VMEM((lgs, te, LANES), F32),
            pltpu.VMEM((te, tb), BF16),
        ],
        compiler_params=_params(("arbitrary", "arbitrary")),
        name="peer_dense",
    )(hn, h, u, u, vt, sel, row, fg)


def _rope_tables(t):
    n_rows = t // GRID_W
    row = jnp.repeat(jnp.arange(n_rows, dtype=F32), GRID_W)
    col = jnp.tile(jnp.arange(GRID_W, dtype=F32), n_rows)
    freqs = ROPE_THETA ** (-jnp.arange(0, AXIS_DIM, 2, dtype=F32) / AXIS_DIM)
    ang = jnp.concatenate([row[:, None] * freqs, col[:, None] * freqs], axis=-1)
    cos, sin = jnp.cos(ang), jnp.sin(ang)
    return jnp.concatenate([cos, cos], axis=-1), jnp.concatenate([-sin, sin], axis=-1)


def _prepare(norm_mix_g, w_in, q_norm_g, k_norm_g, w_gate_f2, b_gate_f, w_gate_b2, b_gate_b,
             gla_norm_g, attn_out_norm_g, w_out, norm_ffn_g, peer_w_q, peer_sub_keys, peer_u,
             peer_v, final_norm_g):
    perm = np.concatenate([np.arange(0, HEAD_DIM, 2), np.arange(1, HEAD_DIM, 2)])
    kw = GLA_HEADS * GLA_DK
    o_k = ATTN_WIDTH
    o_v = o_k + KV_WIDTH
    o_g = o_v + KV_WIDTH
    o_lr = o_g + 2 * kw + 2 * GLA_WIDTH
    wq = w_in[:, :o_k].reshape(D_MODEL, N_HEADS, HEAD_DIM)[:, :, perm].reshape(D_MODEL, o_k)
    wk = w_in[:, o_k:o_v].reshape(D_MODEL, N_KV_HEADS, HEAD_DIM)[:, :, perm].reshape(D_MODEL, KV_WIDTH)
    w_qkv = jnp.concatenate([wq, wk, w_in[:, o_v:o_g]], axis=1).astype(BF16)
    w_lr = jnp.pad(w_in[:, o_lr:], ((0, 0), (0, LANES - 2 * GATE_RANK)))
    w_gla = jnp.concatenate([w_in[:, o_g:o_lr], w_lr], axis=1).astype(BF16)
    wg = jnp.zeros((LANES, 2 * kw), F32)
    wg = wg.at[:GATE_RANK, :kw].set(w_gate_f2).at[GATE_RANK:2 * GATE_RANK, kw:].set(w_gate_b2)
    bg = jnp.concatenate([b_gate_f, b_gate_b])[None, :]

    keys = jnp.stack([peer_sub_keys[:, 0], peer_sub_keys[:, 1][:, PEER_KEY2_ROWS]], axis=1)
    return dict(
        norm_mix_g=norm_mix_g[None, :], w_qkv=w_qkv, w_gla=w_gla, wg=wg, bg=bg,
        qg=q_norm_g[perm][None, :], kg=k_norm_g[perm][None, :],
        gla_gain=gla_norm_g[None, :], attn_gain=attn_out_norm_g[None, :],
        wo_a=w_out[:ATTN_WIDTH].astype(BF16), wo_g=w_out[ATTN_WIDTH:].astype(BF16),
        ffn_g=norm_ffn_g[None, :], wqt=peer_w_q.T.astype(BF16),
        keys=keys.astype(BF16), u=peer_u.astype(BF16),
        vt=peer_v.astype(BF16).reshape(-1, PEER_TE, D_MODEL).transpose(0, 2, 1),
        final_g=final_norm_g[None, :],
    )


def _layer(x, p):
    b, t, d = x.shape
    n = b * t
    x2 = x.reshape(n, d)
    cos, sin = _rope_tables(t)
    q, k, v = _qkv_proj(x2, p["norm_mix_g"], p["w_qkv"], p["qg"], p["kg"], cos, sin, t)
    gq, gk, gv, sg, laf, lab = _gla_proj(x2, p["norm_mix_g"], p["w_gla"], p["wg"], p["bg"])
    oa = _attention(q.reshape(b, t, -1), k.reshape(b, t, -1), v.reshape(b, t, -1))
    r3 = lambda a: a.reshape(b, t, -1)
    og = _gla(r3(gq), r3(gk), r3(gv), r3(sg), r3(laf), r3(lab), p["gla_gain"])
    h, hn = _out_proj(x2, oa.reshape(n, -1), og.reshape(n, -1), p["attn_gain"],
                      p["wo_a"], p["wo_g"], p["ffn_g"])
    sel, row = _peer_route(hn, p["wqt"], p["keys"])
    y = _peer_dense(hn, h, p["u"], p["vt"], sel, row, p["final_g"])
    return y.reshape(b, t, d)


def kernel(x_prompt, x_sample, norm_mix_g, w_in, q_norm_g, k_norm_g, w_gate_f2, b_gate_f,
           w_gate_b2, b_gate_b, gla_norm_g, attn_out_norm_g, w_out, norm_ffn_g, peer_w_q,
           peer_sub_keys, peer_u, peer_v, final_norm_g):
    p = _prepare(norm_mix_g, w_in, q_norm_g, k_norm_g, w_gate_f2, b_gate_f, w_gate_b2,
                 b_gate_b, gla_norm_g, attn_out_norm_g, w_out, norm_ffn_g, peer_w_q,
                 peer_sub_keys, peer_u, peer_v, final_norm_g)
    return (_layer(x_prompt, p), _layer(x_sample, p))
```

```python
import functools
import math

import jax
import jax.numpy as jnp
import numpy as np
from jax import lax
from jax.experimental import pallas as pl
from jax.experimental.pallas import tpu as pltpu

F32 = jnp.float32
BF16 = jnp.bfloat16

D_MODEL = 2048
GRID_W = 64
HEAD_DIM = 128
N_HEADS = 8
N_KV_HEADS = 2
GQA_GROUP = N_HEADS // N_KV_HEADS
ATTN_WIDTH = N_HEADS * HEAD_DIM
KV_WIDTH = N_KV_HEADS * HEAD_DIM
ROPE_THETA = 10000.0
AXIS_DIM = HEAD_DIM // 2
GLA_HEADS = 4
GLA_DK = 128
GLA_DV = 256
GLA_WIDTH = GLA_HEADS * GLA_DV
GATE_RANK = 16
GATE_NORMALIZER = 16.0
PEER_HEADS = 8
N_KEYS = 128
PEER_TOPK = 16
PEER_HALF = 128
PEER_QDIM = 2 * PEER_HALF
EPS = 1e-6
LOG2_E = math.log2(math.e)

LANES = 128
SUBLANES = 8
BF16_ROWS = 16
MXU_DIM = 256
VMEM_LIMIT = 58 * 1024 * 1024

PROJ_TM = 512
PROJ_COLS = 512
ATTN_TQ = 256
GLA_CHUNK = 64
GLA_UNROLL = 16
ROUTE_TB = 512
ROUTE_HEADS = 2
PEER_TB = 512
PEER_TE = 1024

PEER_KEY2_ROWS = np.concatenate([np.arange(0, N_KEYS, 2), np.arange(1, N_KEYS, 2)])

NT_DIMS = (((1,), (1,)), ((), ()))
TN_DIMS = (((0,), (0,)), ((), ()))


def _params(sem):
    return pltpu.CompilerParams(dimension_semantics=sem, vmem_limit_bytes=VMEM_LIMIT)


def _rms(x, g):
    return x * lax.rsqrt(jnp.mean(x * x, axis=-1, keepdims=True) + EPS) * g


def _qkv_kernel(x_ref, g_ref, w_ref, qg_ref, kg_ref, cos_ref, sin_ref, q_ref, k_ref, v_ref):
    xn = _rms(x_ref[...], g_ref[...]).astype(BF16)
    z = jnp.dot(xn, w_ref[...], preferred_element_type=F32)
    c = cos_ref[...]
    s = sin_ref[...]

    def norm_rope(hd, gain, scale):
        hn = _rms(hd, gain)
        rot = pltpu.roll(hn, HEAD_DIM // 2, axis=1)
        return ((hn * c + rot * s) * scale).astype(BF16)

    for hh in range(N_HEADS):
        sl = slice(hh * HEAD_DIM, (hh + 1) * HEAD_DIM)
        q_ref[:, sl] = norm_rope(z[:, sl], qg_ref[...], HEAD_DIM ** -0.5 * LOG2_E)
    for hh in range(N_KV_HEADS):
        sl = slice(hh * HEAD_DIM, (hh + 1) * HEAD_DIM)
        zsl = slice(ATTN_WIDTH + hh * HEAD_DIM, ATTN_WIDTH + (hh + 1) * HEAD_DIM)
        k_ref[:, sl] = norm_rope(z[:, zsl], kg_ref[...], 1.0)
    v_ref[...] = z[:, ATTN_WIDTH + KV_WIDTH:].astype(BF16)


def _qkv_proj(x2, g, w, qg, kg, cos, sin, seq):
    n, d = x2.shape
    tm = min(PROJ_TM, seq)
    nt = seq // tm
    wq = w.shape[1]
    full = lambda i: (0, 0)
    return pl.pallas_call(
        _qkv_kernel,
        grid=(n // tm,),
        in_specs=[
            pl.BlockSpec((tm, d), lambda i: (i, 0)),
            pl.BlockSpec((1, d), full),
            pl.BlockSpec((d, wq), full),
            pl.BlockSpec((1, HEAD_DIM), full),
            pl.BlockSpec((1, HEAD_DIM), full),
            pl.BlockSpec((tm, HEAD_DIM), lambda i: (i % nt, 0)),
            pl.BlockSpec((tm, HEAD_DIM), lambda i: (i % nt, 0)),
        ],
        out_specs=[
            pl.BlockSpec((tm, ATTN_WIDTH), lambda i: (i, 0)),
            pl.BlockSpec((tm, KV_WIDTH), lambda i: (i, 0)),
            pl.BlockSpec((tm, KV_WIDTH), lambda i: (i, 0)),
        ],
        out_shape=[
            jax.ShapeDtypeStruct((n, ATTN_WIDTH), BF16),
            jax.ShapeDtypeStruct((n, KV_WIDTH), BF16),
            jax.ShapeDtypeStruct((n, KV_WIDTH), BF16),
        ],
        compiler_params=_params(("parallel",)),
        name="qkv_proj",
    )(x2, g, w, qg, kg, cos, sin)


def _dot_bf16x3(a, b):
    a_hi = a.astype(BF16)
    b_hi = b.astype(BF16)
    a_lo = (a - a_hi.astype(F32)).astype(BF16)
    b_lo = (b - b_hi.astype(F32)).astype(BF16)
    dot = functools.partial(jnp.dot, preferred_element_type=F32)
    return dot(a_hi, b_hi) + dot(a_lo, b_hi) + dot(a_hi, b_lo)


def _log_sigmoid(x):
    return jnp.minimum(x, 0.0) - jnp.log(1.0 + jnp.exp(-jnp.abs(x)))


def _gla_proj_kernel(x_ref, g_ref, w_ref, wg_ref, bg_ref,
                     gq_ref, gk_ref, gv_ref, sg_ref, laf_ref, lab_ref):
    xn = _rms(x_ref[...], g_ref[...]).astype(BF16)
    kw = GLA_HEADS * GLA_DK
    assert kw == PROJ_COLS and GLA_WIDTH == 2 * PROJ_COLS

    def project(g):
        cols = slice(g * PROJ_COLS, (g + 1) * PROJ_COLS)
        return jnp.dot(xn, w_ref[:, cols], preferred_element_type=F32)

    def silu(x):
        return x * (1.0 / (1.0 + jnp.exp(-x)))

    outputs = [
        (gq_ref, 0, lambda z: z * (GLA_DK ** -0.5)),
        (gk_ref, 0, lambda z: z),
        (gv_ref, 0, lambda z: z),
        (gv_ref, PROJ_COLS, lambda z: z),
        (sg_ref, 0, silu),
        (sg_ref, PROJ_COLS, silu),
    ]
    lr = jnp.dot(xn, w_ref[:, len(outputs) * PROJ_COLS:], preferred_element_type=F32)
    pre = _dot_bf16x3(lr, wg_ref[...]) + bg_ref[...]
    z = project(0)
    la = _log_sigmoid(pre) * (1.0 / GATE_NORMALIZER)
    laf_ref[...] = la[:, :kw]
    lab_ref[...] = la[:, kw:]
    for g, (ref, col, fn) in enumerate(outputs):
        z_next = project(g + 1) if g + 1 < len(outputs) else None
        ref[:, col:col + PROJ_COLS] = fn(z).astype(BF16)
        z = z_next


def _gla_proj(x2, g, w, wg, bg):
    n, d = x2.shape
    tm = PROJ_TM
    kw = GLA_HEADS * GLA_DK
    full = lambda i: (0, 0)
    row = lambda i: (i, 0)
    return pl.pallas_call(
        _gla_proj_kernel,
        grid=(n // tm,),
        in_specs=[
            pl.BlockSpec((tm, d), row),
            pl.BlockSpec((1, d), full),
            pl.BlockSpec(w.shape, full),
            pl.BlockSpec(wg.shape, full),
            pl.BlockSpec(bg.shape, full),
        ],
        out_specs=[
            pl.BlockSpec((tm, kw), row),
            pl.BlockSpec((tm, kw), row),
            pl.BlockSpec((tm, GLA_WIDTH), row),
            pl.BlockSpec((tm, GLA_WIDTH), row),
            pl.BlockSpec((tm, kw), row),
            pl.BlockSpec((tm, kw), row),
        ],
        out_shape=[
            jax.ShapeDtypeStruct((n, kw), BF16),
            jax.ShapeDtypeStruct((n, kw), BF16),
            jax.ShapeDtypeStruct((n, GLA_WIDTH), BF16),
            jax.ShapeDtypeStruct((n, GLA_WIDTH), BF16),
            jax.ShapeDtypeStruct((n, kw), F32),
            jax.ShapeDtypeStruct((n, kw), F32),
        ],
        compiler_params=_params(("parallel",)),
        name="gla_proj",
    )(x2, g, w, wg, bg)


def _attn_kernel(q_ref, k_ref, v_ref, o_ref):
    k = k_ref[0]
    v = v_ref[0]

    def scores(hh):
        sl = slice(hh * HEAD_DIM, (hh + 1) * HEAD_DIM)
        return lax.dot_general(q_ref[0, :, sl], k, NT_DIMS, preferred_element_type=F32)

    s = scores(0)
    for hh in range(GQA_GROUP):
        s_next = scores(hh + 1) if hh + 1 < GQA_GROUP else None
        m = jnp.max(s, axis=-1, keepdims=True)
        p = jnp.exp2(s - m)
        l = jnp.sum(p, axis=-1, keepdims=True)
        o = jnp.dot(p.astype(BF16), v, preferred_element_type=F32)
        o_ref[0, :, hh * HEAD_DIM:(hh + 1) * HEAD_DIM] = (o / l).astype(BF16)
        s = s_next


def _attention(q, k, v):
    b, t, _ = q.shape
    tq = min(ATTN_TQ, t)
    gw = GQA_GROUP * HEAD_DIM
    return pl.pallas_call(
        _attn_kernel,
        grid=(b, N_KV_HEADS, t // tq),
        in_specs=[
            pl.BlockSpec((1, tq, gw), lambda bi, g, qi: (bi, qi, g)),
            pl.BlockSpec((1, t, HEAD_DIM), lambda bi, g, qi: (bi, 0, g)),
            pl.BlockSpec((1, t, HEAD_DIM), lambda bi, g, qi: (bi, 0, g)),
        ],
        out_specs=pl.BlockSpec((1, tq, gw), lambda bi, g, qi: (bi, qi, g)),
        out_shape=jax.ShapeDtypeStruct((b, t, ATTN_WIDTH), BF16),
        compiler_params=_params(("parallel", "parallel", "arbitrary")),
        name="gqa_attention",
    )(q, k, v)


def _split_dot(tri, g):
    hi = g.astype(BF16)
    lo = (g - hi.astype(F32)).astype(BF16)
    return (jnp.dot(tri, hi, preferred_element_type=F32)
            + jnp.dot(tri, lo, preferred_element_type=F32))


def _gla_kernel(q_ref, k_ref, v_ref, sg_ref, gf_ref, gb_ref, gain_ref, o_ref, acc_ref,
                sf_ref, sb_ref):
    t = q_ref.shape[1]
    c = GLA_CHUNK
    n = t // c
    mid = c // 2
    ri = lax.broadcasted_iota(jnp.int32, (c, c), 0)
    ci = lax.broadcasted_iota(jnp.int32, (c, c), 1)
    lower = ri >= ci
    tri_lo = lower.astype(BF16)
    tri_up = (ri <= ci).astype(BF16)
    sf_ref[...] = jnp.zeros_like(sf_ref)
    sb_ref[...] = jnp.zeros_like(sb_ref)

    upper = ri <= ci
    un = min(GLA_UNROLL, n // 2)

    def finish(rows, o):
        y = _rms(o, gain_ref[...]) * sg_ref[0, rows, :].astype(F32)
        o_ref[0, rows, :] = y.astype(BF16)

    def trip(i, second_half):
        items = []
        for u in range(un):
            idx = i * un + u
            items.append((idx, sf_ref, gf_ref, tri_lo, c - 1, lower))
            items.append((n - 1 - idx, sb_ref, gb_ref, tri_up, 0, upper))
        rows = [pl.ds(pl.multiple_of(it[0] * c, c), c) for it in items]
        bs = [_split_dot(it[3], it[2][0, r, :]) for it, r in zip(items, rows)]
        qs, ks, scores = [], [], []
        for it, r, b in zip(items, rows, bs):
            q = q_ref[0, r, :].astype(F32)
            k = k_ref[0, r, :].astype(F32)
            ref = b[mid:mid + 1, :]
            qt = (q * jnp.exp(b - ref)).astype(BF16)
            kt = (k * jnp.exp(ref - b)).astype(BF16)
            a = lax.dot_general(qt, kt, NT_DIMS, preferred_element_type=F32)
            qs.append(q)
            ks.append(k)
            scores.append(jnp.where(it[5], a, 0.0).astype(BF16))
        intra, updates, tots = [], [], []
        for it, r, b, k, a in zip(items, rows, bs, ks, scores):
            v = v_ref[0, r, :]
            tot = b[it[4]:it[4] + 1, :]
            kd = (k * jnp.exp(tot - b)).astype(BF16)
            intra.append(jnp.dot(a, v, preferred_element_type=F32))
            updates.append(lax.dot_general(v, kd, TN_DIMS, preferred_element_type=F32))
            tots.append(tot)
        for it, r, b, q, o_in, upd, tot in zip(items, rows, bs, qs, intra, updates, tots):
            s_ref = it[1]
            st = s_ref[...]
            o = o_in + lax.dot_general((q * jnp.exp(b)).astype(BF16), st.astype(BF16), NT_DIMS,
                                       preferred_element_type=F32)
            s_ref[...] = st * jnp.exp(tot) + upd
            if second_half:
                finish(r, acc_ref[r, :] + o)
            else:
                acc_ref[r, :] = o

    half_trips = n // 2 // un

    def first_half(i, carry):
        trip(i, False)
        return carry

    def second_half(i, carry):
        trip(i, True)
        return carry

    lax.fori_loop(0, half_trips, first_half, 0)
    lax.fori_loop(half_trips, 2 * half_trips, second_half, 0)


def _gla(gq, gk, gv, sg, laf, lab, gain):
    b, t, _ = gq.shape
    kspec = pl.BlockSpec((1, t, GLA_DK), lambda bi, h: (bi, 0, h))
    vspec = pl.BlockSpec((1, t, GLA_DV), lambda bi, h: (bi, 0, h))
    return pl.pallas_call(
        _gla_kernel,
        grid=(b, GLA_HEADS),
        in_specs=[kspec, kspec, vspec, vspec, kspec, kspec,
                  pl.BlockSpec((1, GLA_DV), lambda bi, h: (0, 0))],
        out_specs=vspec,
        out_shape=jax.ShapeDtypeStruct((b, t, GLA_WIDTH), BF16),
        scratch_shapes=[
            pltpu.VMEM((t, GLA_DV), F32),
            pltpu.VMEM((GLA_DV, GLA_DK), F32),
            pltpu.VMEM((GLA_DV, GLA_DK), F32),
        ],
        compiler_params=_params(("parallel", "parallel")),
        name="gla_bidir",
    )(gq, gk, gv, sg, laf, lab, gain)


def _out_proj_kernel(x_ref, oa_ref, og_ref, ag_ref, wa_ref, wg_ref, fg_ref, h_ref, hn_ref):
    oa = _rms(oa_ref[...].astype(F32), ag_ref[...]).astype(BF16)
    h = (x_ref[...]
         + jnp.dot(oa, wa_ref[...], preferred_element_type=F32)
         + jnp.dot(og_ref[...], wg_ref[...], preferred_element_type=F32))
    h_ref[...] = h
    hn_ref[...] = _rms(h, fg_ref[...]).astype(BF16)


def _out_proj(x2, oa, og, ag, wa, wg, fg):
    n, d = x2.shape
    tm = PROJ_TM
    full = lambda i: (0, 0)
    row = lambda i: (i, 0)
    return pl.pallas_call(
        _out_proj_kernel,
        grid=(n // tm,),
        in_specs=[
            pl.BlockSpec((tm, d), row),
            pl.BlockSpec((tm, ATTN_WIDTH), row),
            pl.BlockSpec((tm, GLA_WIDTH), row),
            pl.BlockSpec((1, ATTN_WIDTH), full),
            pl.BlockSpec(wa.shape, full),
            pl.BlockSpec(wg.shape, full),
            pl.BlockSpec((1, d), full),
        ],
        out_specs=[pl.BlockSpec((tm, d), row), pl.BlockSpec((tm, d), row)],
        out_shape=[jax.ShapeDtypeStruct((n, d), F32), jax.ShapeDtypeStruct((n, d), BF16)],
        compiler_params=_params(("parallel",)),
        name="out_proj",
    )(x2, oa, og, ag, wa, wg, fg)


def _cmp_exchange(rows, i, j):
    a, b = rows[i], rows[j]
    if b is None:
        return
    if a is None:
        rows[i], rows[j] = b, None
        return
    rows[i], rows[j] = jnp.maximum(a, b), jnp.minimum(a, b)


def _bitonic_merge_desc(rows):
    n = len(rows)
    j = n // 2
    while j >= 1:
        for i in range(n):
            l = i ^ j
            if l > i:
                _cmp_exchange(rows, i, l)
        j //= 2


def _bitonic_sort_desc(rows):
    n = len(rows)
    k = 2
    while k <= n:
        j = k // 2
        while j >= 1:
            for i in range(n):
                l = i ^ j
                if l > i:
                    if (i & k) == 0:
                        _cmp_exchange(rows, i, l)
                    else:
                        _cmp_exchange(rows, l, i)
            j //= 2
        k *= 2


def _top_of_union(a, b):
    n = len(a)
    out = []
    for i in range(n):
        x, y = a[i], b[n - 1 - i]
        out.append(x if y is None else (y if x is None else jnp.maximum(x, y)))
    _bitonic_merge_desc(out)
    return out


def _top16_rows(s):
    l = s.shape[1]
    s3 = s.reshape(N_KEYS // SUBLANES, SUBLANES, l)
    rows = [s3[i] for i in range(N_KEYS // SUBLANES)]
    _bitonic_sort_desc(rows)
    for shift in (4, 2, 1):
        other = [pltpu.roll(r, shift, axis=0) for r in rows]
        rows = _top_of_union(rows, other)
    return rows


_CANDIDATES = [(i, j) for i in range(PEER_TOPK) for j in range(PEER_TOPK)
               if (i + 1) * (j + 1) <= PEER_TOPK]


def _bf16_bits(x):
    bits = lax.bitcast_convert_type(x, jnp.uint32)
    return (bits + jnp.uint32(0x7FFF) + ((bits >> 16) & jnp.uint32(1))) >> 16


def _bf16_pair_words(x):
    hi = _bf16_bits(x)
    return lax.bitcast_convert_type((hi << 16) | hi, jnp.int32)


def _bf16_packed_rows(x):
    hi = _bf16_bits(x)
    half = x.shape[0] // 2
    return lax.bitcast_convert_type(hi[:half] | (hi[half:] << 16), jnp.int32)


def _route_kernel(hn_ref, wqt_ref, keys_ref, sel_ref, row_ref, qt_ref):
    step = pl.program_id(1)

    @pl.when(step == 0)
    def _():
        qt_ref[...] = lax.dot_general(wqt_ref[...], hn_ref[...], NT_DIMS,
                                      preferred_element_type=F32).astype(BF16)

    for k in range(ROUTE_HEADS):
        _route_head(step * ROUTE_HEADS + k, k, keys_ref, sel_ref, row_ref, qt_ref)


def _route_head(hh, k, keys_ref, sel_ref, row_ref, qt_ref):
    base = pl.multiple_of(hh * PEER_QDIM, PEER_QDIM)
    q1 = qt_ref[pl.ds(base, PEER_HALF), :]
    q2 = qt_ref[pl.ds(base + PEER_HALF, PEER_HALF), :]
    s1 = jnp.dot(keys_ref[k, 0], q1, preferred_element_type=F32)
    s2 = jnp.dot(keys_ref[k, 1], q2, preferred_element_type=F32)
    tb = s1.shape[1]
    lgs = tb // LANES
    a = _top16_rows(s1)
    b = _top16_rows(s2)

    sub = lax.broadcasted_iota(jnp.int32, (SUBLANES, LANES), 0)

    def compact(r):
        out = r[:, :LANES]
        for g in range(1, lgs):
            out = jnp.where(sub == g, r[:, g * LANES:(g + 1) * LANES], out)
        return out

    def expand(c):
        return jnp.concatenate(
            [jnp.broadcast_to(c[g:g + 1, :], (SUBLANES, LANES)) for g in range(lgs)], axis=1)

    ac = [compact(r) for r in a]
    bc = [compact(r) for r in b]
    cand = [ac[i] + bc[j] for (i, j) in _CANDIDATES]
    cand += [None] * (64 - len(cand))
    groups = []
    for gi in range(4):
        grp = cand[gi * 16:(gi + 1) * 16]
        _bitonic_sort_desc(grp)
        groups.append(grp)
    top = _top_of_union(_top_of_union(groups[0], groups[1]),
                        _top_of_union(groups[2], groups[3]))
    z = jnp.zeros_like(top[0])
    for tk in top:
        z = z + jnp.exp(tk - top[0])
    tau = expand(top[PEER_TOPK - 1])
    inv_z = expand(1.0 / z)
    full = lambda r: jnp.broadcast_to(r[0:1, :], (N_KEYS, tb))
    tau_f = full(tau)
    cnt_bits = jnp.zeros((N_KEYS, tb), jnp.int32)
    rank_bits = jnp.zeros((N_KEYS, tb), jnp.int32)
    for j in range(PEER_TOPK):
        bj = full(b[j])
        bits = int(np.asarray(j + 1, dtype=np.float32).view(np.uint32)) >> 16
        cnt_bits = jnp.where(s1 + bj >= tau_f, bits, cnt_bits)
        rank_bits = jnp.where(bj > s2, bits, rank_bits)
    e2 = jnp.exp(s2 - full(b[0]))
    w1 = jnp.exp(s1 - full(a[0])) * full(inv_z)
    half = N_KEYS // 2
    rank_w = rank_bits[:half] | (rank_bits[half:] << 16)
    e2_w = _bf16_packed_rows(e2)
    cnt_w = cnt_bits | (cnt_bits << 16)
    w1_w = _bf16_pair_words(w1)
    for lg in range(tb // LANES):
        lanes = slice(lg * LANES, (lg + 1) * LANES)
        sel_ref[lg, k, 0] = rank_w[:, lanes]
        sel_ref[lg, k, 1] = e2_w[:, lanes]
        row_ref[lg, k, 0] = cnt_w[:, lanes]
        row_ref[lg, k, 1] = w1_w[:, lanes]


def _peer_route(hn, wqt, keys):
    n, d = hn.shape
    tb = min(ROUTE_TB, n)
    lgs = tb // LANES
    rows = jax.ShapeDtypeStruct((n // LANES, PEER_HEADS, 2, N_KEYS, LANES), jnp.int32)
    sels = jax.ShapeDtypeStruct((n // LANES, PEER_HEADS, 2, N_KEYS // 2, LANES), jnp.int32)
    rspec = pl.BlockSpec((lgs, ROUTE_HEADS, 2, N_KEYS, LANES), lambda i, h: (i, h, 0, 0, 0))
    sspec = pl.BlockSpec((lgs, ROUTE_HEADS, 2, N_KEYS // 2, LANES), lambda i, h: (i, h, 0, 0, 0))
    return pl.pallas_call(
        _route_kernel,
        grid=(n // tb, PEER_HEADS // ROUTE_HEADS),
        in_specs=[
            pl.BlockSpec((tb, d), lambda i, h: (i, 0)),
            pl.BlockSpec(wqt.shape, lambda i, h: (0, 0), pipeline_mode=pl.Buffered(1)),
            pl.BlockSpec((ROUTE_HEADS, 2, N_KEYS, PEER_HALF), lambda i, h: (h, 0, 0, 0)),
        ],
        out_specs=[sspec, rspec],
        out_shape=[sels, rows],
        scratch_shapes=[pltpu.VMEM((PEER_HEADS * PEER_QDIM, tb), BF16)],
        compiler_params=_params(("parallel", "arbitrary")),
        name="peer_route",
    )(hn, wqt, keys)


def _gelu(x):
    return 0.5 * x * (1.0 + lax.erf(x * (2.0 ** -0.5)))


def _peer_kernel(hn_ref, h_ref, u0_ref, u_ref, vt_ref, sel_ref, row_ref, fg_ref,
                 y_ref, acc_ref, hta_ref, htb_ref, w_ref):
    j = pl.program_id(1)
    nj = pl.num_programs(1)
    lane_groups, te, _ = hta_ref.shape
    rows_per_step = te // N_KEYS

    def score(u_tile_ref, dst_ref, hn=None):
        hn = hn_ref[...] if hn is None else hn
        ht = lax.dot_general(u_tile_ref[...], hn, NT_DIMS,
                             preferred_element_type=F32)
        for lg in range(lane_groups):
            dst_ref[lg] = ht[:, lg * LANES:(lg + 1) * LANES]

    def released_by(gates):
        hn = hn_ref[...]
        tb, d = hn.shape
        n_tiles, k_tiles = tb // MXU_DIM, d // MXU_DIM
        groups = MXU_DIM // BF16_ROWS
        assert len(gates) == n_tiles * k_tiles * groups
        rows = []
        for nt in range(n_tiles):
            for rg in range(groups):
                r0 = nt * MXU_DIM + rg * BF16_ROWS
                cols = []
                for kt in range(k_tiles):
                    g = gates[(nt * k_tiles + kt) * groups + rg]
                    z = g * jnp.zeros_like(g)
                    z = jnp.concatenate([z] * (MXU_DIM // LANES), axis=1)
                    cols.append(hn[r0:r0 + BF16_ROWS, kt * MXU_DIM:(kt + 1) * MXU_DIM] + z)
                rows.append(jnp.concatenate(cols, axis=1))
        return jnp.concatenate(rows, axis=0)

    @pl.when(j == 0)
    def _():
        acc_ref[...] = jnp.zeros_like(acc_ref)

    @pl.when((pl.program_id(0) == 0) & (j == 0))
    def _():
        score(u0_ref, hta_ref)

    def step(cur_ref, nxt_ref):
        zero = jnp.zeros((BF16_ROWS, LANES), BF16)
        gates = []

        def row_tile(which, lg, hh, rr):
            words = jnp.broadcast_to(row_ref[lg, hh, which, rr:rr + 1, :], (SUBLANES, LANES))
            return pltpu.bitcast(words, BF16)

        for lg in range(lane_groups):
            lanes = slice(lg * LANES, (lg + 1) * LANES)
            for rr in range(rows_per_step):
                cnt = [row_tile(0, lg, hh, rr) for hh in range(PEER_HEADS)]
                w1 = [row_tile(1, lg, hh, rr) for hh in range(PEER_HEADS)]
                for kk in range(N_KEYS // BF16_ROWS):
                    words = slice(kk * SUBLANES, (kk + 1) * SUBLANES)
                    rows = slice(rr * N_KEYS + kk * BF16_ROWS, rr * N_KEYS + (kk + 1) * BF16_ROWS)
                    gate = zero
                    for hh in range(PEER_HEADS):
                        rank = pltpu.bitcast(sel_ref[lg, hh, 0, words, :], BF16)
                        e2 = pltpu.bitcast(sel_ref[lg, hh, 1, words, :], BF16)
                        gate = gate + jnp.where(rank < cnt[hh], e2, zero) * w1[hh]
                    gates.append(gate)
                    w_ref[rows, lanes] = gate * _gelu(cur_ref[lg, rows, :]).astype(BF16)
        score(u_ref, nxt_ref, released_by(gates))
        acc_ref[...] += jnp.dot(vt_ref[...], w_ref[...], preferred_element_type=F32)

    @pl.when(j % 2 == 0)
    def _():
        step(hta_ref, htb_ref)

    @pl.when(j % 2 == 1)
    def _():
        step(htb_ref, hta_ref)

    @pl.when(j == nj - 1)
    def _():
        h = h_ref[...] + acc_ref[...].T
        y_ref[...] = _rms(h, fg_ref[...])


def _peer_dense(hn, h, u, vt, sel, row, fg):
    n, d = hn.shape
    e = u.shape[0]
    tb, te = PEER_TB, PEER_TE
    nj = e // te
    ni = n // tb
    assert nj % 2 == 0, "score buffers alternate with the step parity across token blocks"
    lgs = tb // LANES
    rows = te // N_KEYS
    once = pl.Buffered(1)
    tspec = pl.BlockSpec((lgs, PEER_HEADS, 2, N_KEYS // 2, LANES), lambda i, j: (i, 0, 0, 0, 0))
    rspec = pl.BlockSpec((lgs, PEER_HEADS, 2, rows, LANES), lambda i, j: (i, 0, 0, j, 0))
    return pl.pallas_call(
        _peer_kernel,
        grid=(ni, nj),
        in_specs=[
            pl.BlockSpec((tb, d), lambda i, j: (jnp.minimum(i + (j + 1) // nj, ni - 1), 0)),
            pl.BlockSpec((tb, d), lambda i, j: (i, 0)),
            pl.BlockSpec((te, d), lambda i, j: (0, 0), pipeline_mode=once),
            pl.BlockSpec((te, d), lambda i, j: ((j + 1) % nj, 0)),
            pl.BlockSpec((None, d, te), lambda i, j: (j, 0, 0)),
            tspec, rspec,
            pl.BlockSpec((1, d), lambda i, j: (0, 0)),
        ],
        out_specs=pl.BlockSpec((tb, d), lambda i, j: (i, 0)),
        out_shape=jax.ShapeDtypeStruct((n, d), F32),
        scratch_shapes=[
            pltpu.VMEM((d, tb), F32),
            pltpu.VMEM((lgs, te, LANES), F32),
            pltpu.VMEM((lgs, te, LANES), F32),
            pltpu.VMEM((te, tb), BF16),
        ],
        compiler_params=_params(("arbitrary", "arbitrary")),
        name="peer_dense",
    )(hn, h, u, u, vt, sel, row, fg)


def _rope_tables(t):
    n_rows = t // GRID_W
    row = jnp.repeat(jnp.arange(n_rows, dtype=F32), GRID_W)
    col = jnp.tile(jnp.arange(GRID_W, dtype=F32), n_rows)
    freqs = ROPE_THETA ** (-jnp.arange(0, AXIS_DIM, 2, dtype=F32) / AXIS_DIM)
    ang = jnp.concatenate([row[:, None] * freqs, col[:, None] * freqs], axis=-1)
    cos, sin = jnp.cos(ang), jnp.sin(ang)
    return jnp.concatenate([cos, cos], axis=-1), jnp.concatenate([-sin, sin], axis=-1)


def _prepare(norm_mix_g, w_in, q_norm_g, k_norm_g, w_gate_f2, b_gate_f, w_gate_b2, b_gate_b,
             gla_norm_g, attn_out_norm_g, w_out, norm_ffn_g, peer_w_q, peer_sub_keys, peer_u,
             peer_v, final_norm_g):
    perm = np.concatenate([np.arange(0, HEAD_DIM, 2), np.arange(1, HEAD_DIM, 2)])
    kw = GLA_HEADS * GLA_DK
    o_k = ATTN_WIDTH
    o_v = o_k + KV_WIDTH
    o_g = o_v + KV_WIDTH
    o_lr = o_g + 2 * kw + 2 * GLA_WIDTH
    wq = w_in[:, :o_k].reshape(D_MODEL, N_HEADS, HEAD_DIM)[:, :, perm].reshape(D_MODEL, o_k)
    wk = w_in[:, o_k:o_v].reshape(D_MODEL, N_KV_HEADS, HEAD_DIM)[:, :, perm].reshape(D_MODEL, KV_WIDTH)
    w_qkv = jnp.concatenate([wq, wk, w_in[:, o_v:o_g]], axis=1).astype(BF16)
    w_lr = jnp.pad(w_in[:, o_lr:], ((0, 0), (0, LANES - 2 * GATE_RANK)))
    w_gla = jnp.concatenate([w_in[:, o_g:o_lr], w_lr], axis=1).astype(BF16)
    wg = jnp.zeros((LANES, 2 * kw), F32)
    wg = wg.at[:GATE_RANK, :kw].set(w_gate_f2).at[GATE_RANK:2 * GATE_RANK, kw:].set(w_gate_b2)
    bg = jnp.concatenate([b_gate_f, b_gate_b])[None, :]

    keys = jnp.stack([peer_sub_keys[:, 0], peer_sub_keys[:, 1][:, PEER_KEY2_ROWS]], axis=1)
    return dict(
        norm_mix_g=norm_mix_g[None, :], w_qkv=w_qkv, w_gla=w_gla, wg=wg, bg=bg,
        qg=q_norm_g[perm][None, :], kg=k_norm_g[perm][None, :],
        gla_gain=gla_norm_g[None, :], attn_gain=attn_out_norm_g[None, :],
        wo_a=w_out[:ATTN_WIDTH].astype(BF16), wo_g=w_out[ATTN_WIDTH:].astype(BF16),
        ffn_g=norm_ffn_g[None, :], wqt=peer_w_q.T.astype(BF16),
        keys=keys.astype(BF16), u=peer_u.astype(BF16),
        vt=peer_v.astype(BF16).reshape(-1, PEER_TE, D_MODEL).transpose(0, 2, 1),
        final_g=final_norm_g[None, :],
    )


def _layer(x, p):
    b, t, d = x.shape
    n = b * t
    x2 = x.reshape(n, d)
    cos, sin = _rope_tables(t)
    q, k, v = _qkv_proj(x2, p["norm_mix_g"], p["w_qkv"], p["qg"], p["kg"], cos, sin, t)
    gq, gk, gv, sg, laf, lab = _gla_proj(x2, p["norm_mix_g"], p["w_gla"], p["wg"], p["bg"])
    oa = _attention(q.reshape(b, t, -1), k.reshape(b, t, -1), v.reshape(b, t, -1))
    r3 = lambda a: a.reshape(b, t, -1)
    og = _gla(r3(gq), r3(gk), r3(gv), r3(sg), r3(laf), r3(lab), p["gla_gain"])
    h, hn = _out_proj(x2, oa.reshape(n, -1), og.reshape(n, -1), p["attn_gain"],
                      p["wo_a"], p["wo_g"], p["ffn_g"])
    sel, row = _peer_route(hn, p["wqt"], p["keys"])
    y = _peer_dense(hn, h, p["u"], p["vt"], sel, row, p["final_g"])
    return y.reshape(b, t, d)


def kernel(x_prompt, x_sample, norm_mix_g, w_in, q_norm_g, k_norm_g, w_gate_f2, b_gate_f,
           w_gate_b2, b_gate_b, gla_norm_g, attn_out_norm_g, w_out, norm_ffn_g, peer_w_q,
           peer_sub_keys, peer_u, peer_v, final_norm_g):
    p = _prepare(norm_mix_g, w_in, q_norm_g, k_norm_g, w_gate_f2, b_gate_f, w_gate_b2,
                 b_gate_b, gla_norm_g, attn_out_norm_g, w_out, norm_ffn_g, peer_w_q,
                 peer_sub_keys, peer_u, peer_v, final_norm_g)
    return (_layer(x_prompt, p), _layer(x_sample, p))
```

```python
import functools
import math

import jax
import jax.numpy as jnp
import numpy as np
from jax import lax
from jax.experimental import pallas as pl
from jax.experimental.pallas import tpu as pltpu

F32 = jnp.float32
BF16 = jnp.bfloat16

D_MODEL = 2048
GRID_W = 64
HEAD_DIM = 128
N_HEADS = 8
N_KV_HEADS = 2
GQA_GROUP = N_HEADS // N_KV_HEADS
ATTN_WIDTH = N_HEADS * HEAD_DIM
KV_WIDTH = N_KV_HEADS * HEAD_DIM
ROPE_THETA = 10000.0
AXIS_DIM = HEAD_DIM // 2
GLA_HEADS = 4
GLA_DK = 128
GLA_DV = 256
GLA_WIDTH = GLA_HEADS * GLA_DV
GATE_RANK = 16
GATE_NORMALIZER = 16.0
PEER_HEADS = 8
N_KEYS = 128
PEER_TOPK = 16
PEER_HALF = 128
PEER_QDIM = 2 * PEER_HALF
EPS = 1e-6
LOG2_E = math.log2(math.e)

LANES = 128
SUBLANES = 8
BF16_ROWS = 16
MXU_DIM = 256
VMEM_LIMIT = 58 * 1024 * 1024

PROJ_TM = 512
PROJ_COLS = 512
ATTN_TQ = 256
GLA_CHUNK = 64
GLA_UNROLL = 16
ROUTE_TB = 512
ROUTE_HEADS = 2
PEER_TB = 512
RELEASE_NUM, RELEASE_DEN = 1, 2
PEER_TE = 1024

PEER_KEY2_ROWS = np.concatenate([np.arange(0, N_KEYS, 2), np.arange(1, N_KEYS, 2)])

NT_DIMS = (((1,), (1,)), ((), ()))
TN_DIMS = (((0,), (0,)), ((), ()))


def _params(sem):
    return pltpu.CompilerParams(dimension_semantics=sem, vmem_limit_bytes=VMEM_LIMIT)


def _rms(x, g):
    return x * lax.rsqrt(jnp.mean(x * x, axis=-1, keepdims=True) + EPS) * g


def _qkv_kernel(x_ref, g_ref, w_ref, qg_ref, kg_ref, cos_ref, sin_ref, q_ref, k_ref, v_ref):
    xn = _rms(x_ref[...], g_ref[...]).astype(BF16)
    z = jnp.dot(xn, w_ref[...], preferred_element_type=F32)
    c = cos_ref[...]
    s = sin_ref[...]

    def norm_rope(hd, gain, scale):
        hn = _rms(hd, gain)
        rot = pltpu.roll(hn, HEAD_DIM // 2, axis=1)
        return ((hn * c + rot * s) * scale).astype(BF16)

    for hh in range(N_HEADS):
        sl = slice(hh * HEAD_DIM, (hh + 1) * HEAD_DIM)
        q_ref[:, sl] = norm_rope(z[:, sl], qg_ref[...], HEAD_DIM ** -0.5 * LOG2_E)
    for hh in range(N_KV_HEADS):
        sl = slice(hh * HEAD_DIM, (hh + 1) * HEAD_DIM)
        zsl = slice(ATTN_WIDTH + hh * HEAD_DIM, ATTN_WIDTH + (hh + 1) * HEAD_DIM)
        k_ref[:, sl] = norm_rope(z[:, zsl], kg_ref[...], 1.0)
    v_ref[...] = z[:, ATTN_WIDTH + KV_WIDTH:].astype(BF16)


def _qkv_proj(x2, g, w, qg, kg, cos, sin, seq):
    n, d = x2.shape
    tm = min(PROJ_TM, seq)
    nt = seq // tm
    wq = w.shape[1]
    full = lambda i: (0, 0)
    return pl.pallas_call(
        _qkv_kernel,
        grid=(n // tm,),
        in_specs=[
            pl.BlockSpec((tm, d), lambda i: (i, 0)),
            pl.BlockSpec((1, d), full),
            pl.BlockSpec((d, wq), full),
            pl.BlockSpec((1, HEAD_DIM), full),
            pl.BlockSpec((1, HEAD_DIM), full),
            pl.BlockSpec((tm, HEAD_DIM), lambda i: (i % nt, 0)),
            pl.BlockSpec((tm, HEAD_DIM), lambda i: (i % nt, 0)),
        ],
        out_specs=[
            pl.BlockSpec((tm, ATTN_WIDTH), lambda i: (i, 0)),
            pl.BlockSpec((tm, KV_WIDTH), lambda i: (i, 0)),
            pl.BlockSpec((tm, KV_WIDTH), lambda i: (i, 0)),
        ],
        out_shape=[
            jax.ShapeDtypeStruct((n, ATTN_WIDTH), BF16),
            jax.ShapeDtypeStruct((n, KV_WIDTH), BF16),
            jax.ShapeDtypeStruct((n, KV_WIDTH), BF16),
        ],
        compiler_params=_params(("parallel",)),
        name="qkv_proj",
    )(x2, g, w, qg, kg, cos, sin)


def _dot_bf16x3(a, b):
    a_hi = a.astype(BF16)
    b_hi = b.astype(BF16)
    a_lo = (a - a_hi.astype(F32)).astype(BF16)
    b_lo = (b - b_hi.astype(F32)).astype(BF16)
    dot = functools.partial(jnp.dot, preferred_element_type=F32)
    return dot(a_hi, b_hi) + dot(a_lo, b_hi) + dot(a_hi, b_lo)


def _log_sigmoid(x):
    return jnp.minimum(x, 0.0) - jnp.log(1.0 + jnp.exp(-jnp.abs(x)))


def _gla_proj_kernel(x_ref, g_ref, w_ref, wg_ref, bg_ref,
                     gq_ref, gk_ref, gv_ref, sg_ref, laf_ref, lab_ref):
    xn = _rms(x_ref[...], g_ref[...]).astype(BF16)
    kw = GLA_HEADS * GLA_DK
    assert kw == PROJ_COLS and GLA_WIDTH == 2 * PROJ_COLS

    def project(g):
        cols = slice(g * PROJ_COLS, (g + 1) * PROJ_COLS)
        return jnp.dot(xn, w_ref[:, cols], preferred_element_type=F32)

    def silu(x):
        return x * (1.0 / (1.0 + jnp.exp(-x)))

    outputs = [
        (gq_ref, 0, lambda z: z * (GLA_DK ** -0.5)),
        (gk_ref, 0, lambda z: z),
        (gv_ref, 0, lambda z: z),
        (gv_ref, PROJ_COLS, lambda z: z),
        (sg_ref, 0, silu),
        (sg_ref, PROJ_COLS, silu),
    ]
    lr = jnp.dot(xn, w_ref[:, len(outputs) * PROJ_COLS:], preferred_element_type=F32)
    pre = _dot_bf16x3(lr, wg_ref[...]) + bg_ref[...]
    z = project(0)
    la = _log_sigmoid(pre) * (1.0 / GATE_NORMALIZER)
    laf_ref[...] = la[:, :kw]
    lab_ref[...] = la[:, kw:]
    for g, (ref, col, fn) in enumerate(outputs):
        z_next = project(g + 1) if g + 1 < len(outputs) else None
        ref[:, col:col + PROJ_COLS] = fn(z).astype(BF16)
        z = z_next


def _gla_proj(x2, g, w, wg, bg):
    n, d = x2.shape
    tm = PROJ_TM
    kw = GLA_HEADS * GLA_DK
    full = lambda i: (0, 0)
    row = lambda i: (i, 0)
    return pl.pallas_call(
        _gla_proj_kernel,
        grid=(n // tm,),
        in_specs=[
            pl.BlockSpec((tm, d), row),
            pl.BlockSpec((1, d), full),
            pl.BlockSpec(w.shape, full),
            pl.BlockSpec(wg.shape, full),
            pl.BlockSpec(bg.shape, full),
        ],
        out_specs=[
            pl.BlockSpec((tm, kw), row),
            pl.BlockSpec((tm, kw), row),
            pl.BlockSpec((tm, GLA_WIDTH), row),
            pl.BlockSpec((tm, GLA_WIDTH), row),
            pl.BlockSpec((tm, kw), row),
            pl.BlockSpec((tm, kw), row),
        ],
        out_shape=[
            jax.ShapeDtypeStruct((n, kw), BF16),
            jax.ShapeDtypeStruct((n, kw), BF16),
            jax.ShapeDtypeStruct((n, GLA_WIDTH), BF16),
            jax.ShapeDtypeStruct((n, GLA_WIDTH), BF16),
            jax.ShapeDtypeStruct((n, kw), F32),
            jax.ShapeDtypeStruct((n, kw), F32),
        ],
        compiler_params=_params(("parallel",)),
        name="gla_proj",
    )(x2, g, w, wg, bg)


def _attn_kernel(q_ref, k_ref, v_ref, o_ref):
    k = k_ref[0]
    v = v_ref[0]

    def scores(hh):
        sl = slice(hh * HEAD_DIM, (hh + 1) * HEAD_DIM)
        return lax.dot_general(q_ref[0, :, sl], k, NT_DIMS, preferred_element_type=F32)

    s = scores(0)
    for hh in range(GQA_GROUP):
        s_next = scores(hh + 1) if hh + 1 < GQA_GROUP else None
        m = jnp.max(s, axis=-1, keepdims=True)
        p = jnp.exp2(s - m)
        l = jnp.sum(p, axis=-1, keepdims=True)
        o = jnp.dot(p.astype(BF16), v, preferred_element_type=F32)
        o_ref[0, :, hh * HEAD_DIM:(hh + 1) * HEAD_DIM] = (o / l).astype(BF16)
        s = s_next


def _attention(q, k, v):
    b, t, _ = q.shape
    tq = min(ATTN_TQ, t)
    gw = GQA_GROUP * HEAD_DIM
    return pl.pallas_call(
        _attn_kernel,
        grid=(b, N_KV_HEADS, t // tq),
        in_specs=[
            pl.BlockSpec((1, tq, gw), lambda bi, g, qi: (bi, qi, g)),
            pl.BlockSpec((1, t, HEAD_DIM), lambda bi, g, qi: (bi, 0, g)),
            pl.BlockSpec((1, t, HEAD_DIM), lambda bi, g, qi: (bi, 0, g)),
        ],
        out_specs=pl.BlockSpec((1, tq, gw), lambda bi, g, qi: (bi, qi, g)),
        out_shape=jax.ShapeDtypeStruct((b, t, ATTN_WIDTH), BF16),
        compiler_params=_params(("parallel", "parallel", "arbitrary")),
        name="gqa_attention",
    )(q, k, v)


def _split_dot(tri, g):
    hi = g.astype(BF16)
    lo = (g - hi.astype(F32)).astype(BF16)
    return (jnp.dot(tri, hi, preferred_element_type=F32)
            + jnp.dot(tri, lo, preferred_element_type=F32))


def _gla_kernel(q_ref, k_ref, v_ref, sg_ref, gf_ref, gb_ref, gain_ref, o_ref, acc_ref,
                sf_ref, sb_ref):
    t = q_ref.shape[1]
    c = GLA_CHUNK
    n = t // c
    mid = c // 2
    ri = lax.broadcasted_iota(jnp.int32, (c, c), 0)
    ci = lax.broadcasted_iota(jnp.int32, (c, c), 1)
    lower = ri >= ci
    tri_lo = lower.astype(BF16)
    tri_up = (ri <= ci).astype(BF16)
    sf_ref[...] = jnp.zeros_like(sf_ref)
    sb_ref[...] = jnp.zeros_like(sb_ref)

    upper = ri <= ci
    un = min(GLA_UNROLL, n // 2)

    def finish(rows, o):
        y = _rms(o, gain_ref[...]) * sg_ref[0, rows, :].astype(F32)
        o_ref[0, rows, :] = y.astype(BF16)

    def trip(i, second_half):
        items = []
        for u in range(un):
            idx = i * un + u
            items.append((idx, sf_ref, gf_ref, tri_lo, c - 1, lower))
            items.append((n - 1 - idx, sb_ref, gb_ref, tri_up, 0, upper))
        rows = [pl.ds(pl.multiple_of(it[0] * c, c), c) for it in items]
        bs = [_split_dot(it[3], it[2][0, r, :]) for it, r in zip(items, rows)]
        qs, ks, scores = [], [], []
        for it, r, b in zip(items, rows, bs):
            q = q_ref[0, r, :].astype(F32)
            k = k_ref[0, r, :].astype(F32)
            ref = b[mid:mid + 1, :]
            qt = (q * jnp.exp(b - ref)).astype(BF16)
            kt = (k * jnp.exp(ref - b)).astype(BF16)
            a = lax.dot_general(qt, kt, NT_DIMS, preferred_element_type=F32)
            qs.append(q)
            ks.append(k)
            scores.append(jnp.where(it[5], a, 0.0).astype(BF16))
        intra, updates, tots = [], [], []
        for it, r, b, k, a in zip(items, rows, bs, ks, scores):
            v = v_ref[0, r, :]
            tot = b[it[4]:it[4] + 1, :]
            kd = (k * jnp.exp(tot - b)).astype(BF16)
            intra.append(jnp.dot(a, v, preferred_element_type=F32))
            updates.append(lax.dot_general(v, kd, TN_DIMS, preferred_element_type=F32))
            tots.append(tot)
        for it, r, b, q, o_in, upd, tot in zip(items, rows, bs, qs, intra, updates, tots):
            s_ref = it[1]
            st = s_ref[...]
            o = o_in + lax.dot_general((q * jnp.exp(b)).astype(BF16), st.astype(BF16), NT_DIMS,
                                       preferred_element_type=F32)
            s_ref[...] = st * jnp.exp(tot) + upd
            if second_half:
                finish(r, acc_ref[r, :] + o)
            else:
                acc_ref[r, :] = o

    half_trips = n // 2 // un

    def first_half(i, carry):
        trip(i, False)
        return carry

    def second_half(i, carry):
        trip(i, True)
        return carry

    lax.fori_loop(0, half_trips, first_half, 0)
    lax.fori_loop(half_trips, 2 * half_trips, second_half, 0)


def _gla(gq, gk, gv, sg, laf, lab, gain):
    b, t, _ = gq.shape
    kspec = pl.BlockSpec((1, t, GLA_DK), lambda bi, h: (bi, 0, h))
    vspec = pl.BlockSpec((1, t, GLA_DV), lambda bi, h: (bi, 0, h))
    return pl.pallas_call(
        _gla_kernel,
        grid=(b, GLA_HEADS),
        in_specs=[kspec, kspec, vspec, vspec, kspec, kspec,
                  pl.BlockSpec((1, GLA_DV), lambda bi, h: (0, 0))],
        out_specs=vspec,
        out_shape=jax.ShapeDtypeStruct((b, t, GLA_WIDTH), BF16),
        scratch_shapes=[
            pltpu.VMEM((t, GLA_DV), F32),
            pltpu.VMEM((GLA_DV, GLA_DK), F32),
            pltpu.VMEM((GLA_DV, GLA_DK), F32),
        ],
        compiler_params=_params(("parallel", "parallel")),
        name="gla_bidir",
    )(gq, gk, gv, sg, laf, lab, gain)


def _out_proj_kernel(x_ref, oa_ref, og_ref, ag_ref, wa_ref, wg_ref, fg_ref, h_ref, hn_ref):
    oa = _rms(oa_ref[...].astype(F32), ag_ref[...]).astype(BF16)
    h = (x_ref[...]
         + jnp.dot(oa, wa_ref[...], preferred_element_type=F32)
         + jnp.dot(og_ref[...], wg_ref[...], preferred_element_type=F32))
    h_ref[...] = h
    hn_ref[...] = _rms(h, fg_ref[...]).astype(BF16)


def _out_proj(x2, oa, og, ag, wa, wg, fg):
    n, d = x2.shape
    tm = PROJ_TM
    full = lambda i: (0, 0)
    row = lambda i: (i, 0)
    return pl.pallas_call(
        _out_proj_kernel,
        grid=(n // tm,),
        in_specs=[
            pl.BlockSpec((tm, d), row),
            pl.BlockSpec((tm, ATTN_WIDTH), row),
            pl.BlockSpec((tm, GLA_WIDTH), row),
            pl.BlockSpec((1, ATTN_WIDTH), full),
            pl.BlockSpec(wa.shape, full),
            pl.BlockSpec(wg.shape, full),
            pl.BlockSpec((1, d), full),
        ],
        out_specs=[pl.BlockSpec((tm, d), row), pl.BlockSpec((tm, d), row)],
        out_shape=[jax.ShapeDtypeStruct((n, d), F32), jax.ShapeDtypeStruct((n, d), BF16)],
        compiler_params=_params(("parallel",)),
        name="out_proj",
    )(x2, oa, og, ag, wa, wg, fg)


def _cmp_exchange(rows, i, j):
    a, b = rows[i], rows[j]
    if b is None:
        return
    if a is None:
        rows[i], rows[j] = b, None
        return
    rows[i], rows[j] = jnp.maximum(a, b), jnp.minimum(a, b)


def _bitonic_merge_desc(rows):
    n = len(rows)
    j = n // 2
    while j >= 1:
        for i in range(n):
            l = i ^ j
            if l > i:
                _cmp_exchange(rows, i, l)
        j //= 2


def _bitonic_sort_desc(rows):
    n = len(rows)
    k = 2
    while k <= n:
        j = k // 2
        while j >= 1:
            for i in range(n):
                l = i ^ j
                if l > i:
                    if (i & k) == 0:
                        _cmp_exchange(rows, i, l)
                    else:
                        _cmp_exchange(rows, l, i)
            j //= 2
        k *= 2


def _top_of_union(a, b):
    n = len(a)
    out = []
    for i in range(n):
        x, y = a[i], b[n - 1 - i]
        out.append(x if y is None else (y if x is None else jnp.maximum(x, y)))
    _bitonic_merge_desc(out)
    return out


def _top16_rows(s):
    l = s.shape[1]
    s3 = s.reshape(N_KEYS // SUBLANES, SUBLANES, l)
    rows = [s3[i] for i in range(N_KEYS // SUBLANES)]
    _bitonic_sort_desc(rows)
    for shift in (4, 2, 1):
        other = [pltpu.roll(r, shift, axis=0) for r in rows]
        rows = _top_of_union(rows, other)
    return rows


_CANDIDATES = [(i, j) for i in range(PEER_TOPK) for j in range(PEER_TOPK)
               if (i + 1) * (j + 1) <= PEER_TOPK]


def _bf16_bits(x):
    bits = lax.bitcast_convert_type(x, jnp.uint32)
    return (bits + jnp.uint32(0x7FFF) + ((bits >> 16) & jnp.uint32(1))) >> 16


def _bf16_pair_words(x):
    hi = _bf16_bits(x)
    return lax.bitcast_convert_type((hi << 16) | hi, jnp.int32)


def _bf16_packed_rows(x):
    hi = _bf16_bits(x)
    half = x.shape[0] // 2
    return lax.bitcast_convert_type(hi[:half] | (hi[half:] << 16), jnp.int32)


def _route_kernel(hn_ref, wqt_ref, keys_ref, sel_ref, row_ref, qt_ref):
    step = pl.program_id(1)

    @pl.when(step == 0)
    def _():
        qt_ref[...] = lax.dot_general(wqt_ref[...], hn_ref[...], NT_DIMS,
                                      preferred_element_type=F32).astype(BF16)

    for k in range(ROUTE_HEADS):
        _route_head(step * ROUTE_HEADS + k, k, keys_ref, sel_ref, row_ref, qt_ref)


def _route_head(hh, k, keys_ref, sel_ref, row_ref, qt_ref):
    base = pl.multiple_of(hh * PEER_QDIM, PEER_QDIM)
    q1 = qt_ref[pl.ds(base, PEER_HALF), :]
    q2 = qt_ref[pl.ds(base + PEER_HALF, PEER_HALF), :]
    s1 = jnp.dot(keys_ref[k, 0], q1, preferred_element_type=F32)
    s2 = jnp.dot(keys_ref[k, 1], q2, preferred_element_type=F32)
    tb = s1.shape[1]
    lgs = tb // LANES
    a = _top16_rows(s1)
    b = _top16_rows(s2)

    sub = lax.broadcasted_iota(jnp.int32, (SUBLANES, LANES), 0)

    def compact(r):
        out = r[:, :LANES]
        for g in range(1, lgs):
            out = jnp.where(sub == g, r[:, g * LANES:(g + 1) * LANES], out)
        return out

    def expand(c):
        return jnp.concatenate(
            [jnp.broadcast_to(c[g:g + 1, :], (SUBLANES, LANES)) for g in range(lgs)], axis=1)

    ac = [compact(r) for r in a]
    bc = [compact(r) for r in b]
    cand = [ac[i] + bc[j] for (i, j) in _CANDIDATES]
    cand += [None] * (64 - len(cand))
    groups = []
    for gi in range(4):
        grp = cand[gi * 16:(gi + 1) * 16]
        _bitonic_sort_desc(grp)
        groups.append(grp)
    top = _top_of_union(_top_of_union(groups[0], groups[1]),
                        _top_of_union(groups[2], groups[3]))
    z = jnp.zeros_like(top[0])
    for tk in top:
        z = z + jnp.exp(tk - top[0])
    tau = expand(top[PEER_TOPK - 1])
    inv_z = expand(1.0 / z)
    full = lambda r: jnp.broadcast_to(r[0:1, :], (N_KEYS, tb))
    tau_f = full(tau)
    cnt_bits = jnp.zeros((N_KEYS, tb), jnp.int32)
    rank_bits = jnp.zeros((N_KEYS, tb), jnp.int32)
    for j in range(PEER_TOPK):
        bj = full(b[j])
        bits = int(np.asarray(j + 1, dtype=np.float32).view(np.uint32)) >> 16
        cnt_bits = jnp.where(s1 + bj >= tau_f, bits, cnt_bits)
        rank_bits = jnp.where(bj > s2, bits, rank_bits)
    e2 = jnp.exp(s2 - full(b[0]))
    w1 = jnp.exp(s1 - full(a[0])) * full(inv_z)
    half = N_KEYS // 2
    rank_w = rank_bits[:half] | (rank_bits[half:] << 16)
    e2_w = _bf16_packed_rows(e2)
    cnt_w = cnt_bits | (cnt_bits << 16)
    w1_w = _bf16_pair_words(w1)
    for lg in range(tb // LANES):
        lanes = slice(lg * LANES, (lg + 1) * LANES)
        sel_ref[lg, k, 0] = rank_w[:, lanes]
        sel_ref[lg, k, 1] = e2_w[:, lanes]
        row_ref[lg, k, 0] = cnt_w[:, lanes]
        row_ref[lg, k, 1] = w1_w[:, lanes]


def _peer_route(hn, wqt, keys):
    n, d = hn.shape
    tb = min(ROUTE_TB, n)
    lgs = tb // LANES
    rows = jax.ShapeDtypeStruct((n // LANES, PEER_HEADS, 2, N_KEYS, LANES), jnp.int32)
    sels = jax.ShapeDtypeStruct((n // LANES, PEER_HEADS, 2, N_KEYS // 2, LANES), jnp.int32)
    rspec = pl.BlockSpec((lgs, ROUTE_HEADS, 2, N_KEYS, LANES), lambda i, h: (i, h, 0, 0, 0))
    sspec = pl.BlockSpec((lgs, ROUTE_HEADS, 2, N_KEYS // 2, LANES), lambda i, h: (i, h, 0, 0, 0))
    return pl.pallas_call(
        _route_kernel,
        grid=(n // tb, PEER_HEADS // ROUTE_HEADS),
        in_specs=[
            pl.BlockSpec((tb, d), lambda i, h: (i, 0)),
            pl.BlockSpec(wqt.shape, lambda i, h: (0, 0), pipeline_mode=pl.Buffered(1)),
            pl.BlockSpec((ROUTE_HEADS, 2, N_KEYS, PEER_HALF), lambda i, h: (h, 0, 0, 0)),
        ],
        out_specs=[sspec, rspec],
        out_shape=[sels, rows],
        scratch_shapes=[pltpu.VMEM((PEER_HEADS * PEER_QDIM, tb), BF16)],
        compiler_params=_params(("parallel", "arbitrary")),
        name="peer_route",
    )(hn, wqt, keys)


def _gelu(x):
    return 0.5 * x * (1.0 + lax.erf(x * (2.0 ** -0.5)))


def _peer_kernel(hn_ref, h_ref, u0_ref, u_ref, vt_ref, sel_ref, row_ref, fg_ref,
                 y_ref, acc_ref, hta_ref, htb_ref, w_ref):
    j = pl.program_id(1)
    nj = pl.num_programs(1)
    lane_groups, te, _ = hta_ref.shape
    rows_per_step = te // N_KEYS

    def score(u_tile_ref, dst_ref, hn=None):
        hn = hn_ref[...] if hn is None else hn
        ht = lax.dot_general(u_tile_ref[...], hn, NT_DIMS,
                             preferred_element_type=F32)
        for lg in range(lane_groups):
            dst_ref[lg] = ht[:, lg * LANES:(lg + 1) * LANES]

    def released_by(gates):
        hn = hn_ref[...]
        tb, d = hn.shape
        n_tiles, k_tiles = tb // MXU_DIM, d // MXU_DIM
        groups = MXU_DIM // BF16_ROWS
        assert len(gates) == n_tiles * k_tiles * groups
        rows = []
        for nt in range(n_tiles):
            for rg in range(groups):
                r0 = nt * MXU_DIM + rg * BF16_ROWS
                cols = []
                for kt in range(k_tiles):
                    g = gates[((nt * k_tiles + kt) * groups + rg) * RELEASE_NUM // RELEASE_DEN]
                    z = g * jnp.zeros_like(g)
                    z = jnp.concatenate([z] * (MXU_DIM // LANES), axis=1)
                    cols.append(hn[r0:r0 + BF16_ROWS, kt * MXU_DIM:(kt + 1) * MXU_DIM] + z)
                rows.append(jnp.concatenate(cols, axis=1))
        return jnp.concatenate(rows, axis=0)

    @pl.when(j == 0)
    def _():
        acc_ref[...] = jnp.zeros_like(acc_ref)

    @pl.when((pl.program_id(0) == 0) & (j == 0))
    def _():
        score(u0_ref, hta_ref)

    def step(cur_ref, nxt_ref):
        zero = jnp.zeros((BF16_ROWS, LANES), BF16)
        gates = []

        def row_tile(which, lg, hh, rr):
            words = jnp.broadcast_to(row_ref[lg, hh, which, rr:rr + 1, :], (SUBLANES, LANES))
            return pltpu.bitcast(words, BF16)

        for lg in range(lane_groups):
            lanes = slice(lg * LANES, (lg + 1) * LANES)
            for rr in range(rows_per_step):
                cnt = [row_tile(0, lg, hh, rr) for hh in range(PEER_HEADS)]
                w1 = [row_tile(1, lg, hh, rr) for hh in range(PEER_HEADS)]
                for kk in range(N_KEYS // BF16_ROWS):
                    words = slice(kk * SUBLANES, (kk + 1) * SUBLANES)
                    rows = slice(rr * N_KEYS + kk * BF16_ROWS, rr * N_KEYS + (kk + 1) * BF16_ROWS)
                    gate = zero
                    for hh in range(PEER_HEADS):
                        rank = pltpu.bitcast(sel_ref[lg, hh, 0, words, :], BF16)
                        e2 = pltpu.bitcast(sel_ref[lg, hh, 1, words, :], BF16)
                        gate = gate + jnp.where(rank < cnt[hh], e2, zero) * w1[hh]
                    gates.append(gate)
                    w_ref[rows, lanes] = gate * _gelu(cur_ref[lg, rows, :]).astype(BF16)
        score(u_ref, nxt_ref, released_by(gates))
        acc_ref[...] += jnp.dot(vt_ref[...], w_ref[...], preferred_element_type=F32)

    @pl.when(j % 2 == 0)
    def _():
        step(hta_ref, htb_ref)

    @pl.when(j % 2 == 1)
    def _():
        step(htb_ref, hta_ref)

    @pl.when(j == nj - 1)
    def _():
        h = h_ref[...] + acc_ref[...].T
        y_ref[...] = _rms(h, fg_ref[...])


def _peer_dense(hn, h, u, vt, sel, row, fg):
    n, d = hn.shape
    e = u.shape[0]
    tb, te = PEER_TB, PEER_TE
    nj = e // te
    ni = n // tb
    assert nj % 2 == 0, "score buffers alternate with the step parity across token blocks"
    lgs = tb // LANES
    rows = te // N_KEYS
    once = pl.Buffered(1)
    tspec = pl.BlockSpec((lgs, PEER_HEADS, 2, N_KEYS // 2, LANES), lambda i, j: (i, 0, 0, 0, 0))
    rspec = pl.BlockSpec((lgs, PEER_HEADS, 2, rows, LANES), lambda i, j: (i, 0, 0, j, 0))
    return pl.pallas_call(
        _peer_kernel,
        grid=(ni, nj),
        in_specs=[
            pl.BlockSpec((tb, d), lambda i, j: (jnp.minimum(i + (j + 1) // nj, ni - 1), 0)),
            pl.BlockSpec((tb, d), lambda i, j: (i, 0)),
            pl.BlockSpec((te, d), lambda i, j: (0, 0), pipeline_mode=once),
            pl.BlockSpec((te, d), lambda i, j: ((j + 1) % nj, 0)),
            pl.BlockSpec((None, d, te), lambda i, j: (j, 0, 0)),
            tspec, rspec,
            pl.BlockSpec((1, d), lambda i, j: (0, 0)),
        ],
        out_specs=pl.BlockSpec((tb, d), lambda i, j: (i, 0)),
        out_shape=jax.ShapeDtypeStruct((n, d), F32),
        scratch_shapes=[
            pltpu.VMEM((d, tb), F32),
            pltpu.VMEM((lgs, te, LANES), F32),
            pltpu.VMEM((lgs, te, LANES), F32),
            pltpu.VMEM((te, tb), BF16),
        ],
        compiler_params=_params(("arbitrary", "arbitrary")),
        name="peer_dense",
    )(hn, h, u, u, vt, sel, row, fg)


def _rope_tables(t):
    n_rows = t // GRID_W
    row = jnp.repeat(jnp.arange(n_rows, dtype=F32), GRID_W)
    col = jnp.tile(jnp.arange(GRID_W, dtype=F32), n_rows)
    freqs = ROPE_THETA ** (-jnp.arange(0, AXIS_DIM, 2, dtype=F32) / AXIS_DIM)
    ang = jnp.concatenate([row[:, None] * freqs, col[:, None] * freqs], axis=-1)
    cos, sin = jnp.cos(ang), jnp.sin(ang)
    return jnp.concatenate([cos, cos], axis=-1), jnp.concatenate([-sin, sin], axis=-1)


def _prepare(norm_mix_g, w_in, q_norm_g, k_norm_g, w_gate_f2, b_gate_f, w_gate_b2, b_gate_b,
             gla_norm_g, attn_out_norm_g, w_out, norm_ffn_g, peer_w_q, peer_sub_keys, peer_u,
             peer_v, final_norm_g):
    perm = np.concatenate([np.arange(0, HEAD_DIM, 2), np.arange(1, HEAD_DIM, 2)])
    kw = GLA_HEADS * GLA_DK
    o_k = ATTN_WIDTH
    o_v = o_k + KV_WIDTH
    o_g = o_v + KV_WIDTH
    o_lr = o_g + 2 * kw + 2 * GLA_WIDTH
    wq = w_in[:, :o_k].reshape(D_MODEL, N_HEADS, HEAD_DIM)[:, :, perm].reshape(D_MODEL, o_k)
    wk = w_in[:, o_k:o_v].reshape(D_MODEL, N_KV_HEADS, HEAD_DIM)[:, :, perm].reshape(D_MODEL, KV_WIDTH)
    w_qkv = jnp.concatenate([wq, wk, w_in[:, o_v:o_g]], axis=1).astype(BF16)
    w_lr = jnp.pad(w_in[:, o_lr:], ((0, 0), (0, LANES - 2 * GATE_RANK)))
    w_gla = jnp.concatenate([w_in[:, o_g:o_lr], w_lr], axis=1).astype(BF16)
    wg = jnp.zeros((LANES, 2 * kw), F32)
    wg = wg.at[:GATE_RANK, :kw].set(w_gate_f2).at[GATE_RANK:2 * GATE_RANK, kw:].set(w_gate_b2)
    bg = jnp.concatenate([b_gate_f, b_gate_b])[None, :]

    keys = jnp.stack([peer_sub_keys[:, 0], peer_sub_keys[:, 1][:, PEER_KEY2_ROWS]], axis=1)
    return dict(
        norm_mix_g=norm_mix_g[None, :], w_qkv=w_qkv, w_gla=w_gla, wg=wg, bg=bg,
        qg=q_norm_g[perm][None, :], kg=k_norm_g[perm][None, :],
        gla_gain=gla_norm_g[None, :], attn_gain=attn_out_norm_g[None, :],
        wo_a=w_out[:ATTN_WIDTH].astype(BF16), wo_g=w_out[ATTN_WIDTH:].astype(BF16),
        ffn_g=norm_ffn_g[None, :], wqt=peer_w_q.T.astype(BF16),
        keys=keys.astype(BF16), u=peer_u.astype(BF16),
        vt=peer_v.astype(BF16).reshape(-1, PEER_TE, D_MODEL).transpose(0, 2, 1),
        final_g=final_norm_g[None, :],
    )


def _layer(x, p):
    b, t, d = x.shape
    n = b * t
    x2 = x.reshape(n, d)
    cos, sin = _rope_tables(t)
    q, k, v = _qkv_proj(x2, p["norm_mix_g"], p["w_qkv"], p["qg"], p["kg"], cos, sin, t)
    gq, gk, gv, sg, laf, lab = _gla_proj(x2, p["norm_mix_g"], p["w_gla"], p["wg"], p["bg"])
    oa = _attention(q.reshape(b, t, -1), k.reshape(b, t, -1), v.reshape(b, t, -1))
    r3 = lambda a: a.reshape(b, t, -1)
    og = _gla(r3(gq), r3(gk), r3(gv), r3(sg), r3(laf), r3(lab), p["gla_gain"])
    h, hn = _out_proj(x2, oa.reshape(n, -1), og.reshape(n, -1), p["attn_gain"],
                      p["wo_a"], p["wo_g"], p["ffn_g"])
    sel, row = _peer_route(hn, p["wqt"], p["keys"])
    y = _peer_dense(hn, h, p["u"], p["vt"], sel, row, p["final_g"])
    return y.reshape(b, t, d)


def kernel(x_prompt, x_sample, norm_mix_g, w_in, q_norm_g, k_norm_g, w_gate_f2, b_gate_f,
           w_gate_b2, b_gate_b, gla_norm_g, attn_out_norm_g, w_out, norm_ffn_g, peer_w_q,
           peer_sub_keys, peer_u, peer_v, final_norm_g):
    p = _prepare(norm_mix_g, w_in, q_norm_g, k_norm_g, w_gate_f2, b_gate_f, w_gate_b2,
                 b_gate_b, gla_norm_g, attn_out_norm_g, w_out, norm_ffn_g, peer_w_q,
                 peer_sub_keys, peer_u, peer_v, final_norm_g)
    return (_layer(x_prompt, p), _layer(x_sample, p))
```

```python
import functools
import math

import jax
import jax.numpy as jnp
import numpy as np
from jax import lax
from jax.experimental import pallas as pl
from jax.experimental.pallas import tpu as pltpu

F32 = jnp.float32
BF16 = jnp.bfloat16

D_MODEL = 2048
GRID_W = 64
HEAD_DIM = 128
N_HEADS = 8
N_KV_HEADS = 2
GQA_GROUP = N_HEADS // N_KV_HEADS
ATTN_WIDTH = N_HEADS * HEAD_DIM
KV_WIDTH = N_KV_HEADS * HEAD_DIM
ROPE_THETA = 10000.0
AXIS_DIM = HEAD_DIM // 2
GLA_HEADS = 4
GLA_DK = 128
GLA_DV = 256
GLA_WIDTH = GLA_HEADS * GLA_DV
GATE_RANK = 16
GATE_NORMALIZER = 16.0
PEER_HEADS = 8
N_KEYS = 128
PEER_TOPK = 16
PEER_HALF = 128
PEER_QDIM = 2 * PEER_HALF
EPS = 1e-6
LOG2_E = math.log2(math.e)

LANES = 128
SUBLANES = 8
BF16_ROWS = 16
MXU_DIM = 256
VMEM_LIMIT = 58 * 1024 * 1024

PROJ_TM = 512
PROJ_COLS = 512
ATTN_TQ = 256
ATTN_AHEAD = 2
GLA_CHUNK = 64
GLA_UNROLL = 16
ROUTE_TB = 512
ROUTE_HEADS = 2
PEER_TB = 512
RELEASE_NUM, RELEASE_DEN = 1, 2
PEER_TE = 1024

PEER_KEY2_ROWS = np.concatenate([np.arange(0, N_KEYS, 2), np.arange(1, N_KEYS, 2)])

NT_DIMS = (((1,), (1,)), ((), ()))
TN_DIMS = (((0,), (0,)), ((), ()))


def _params(sem):
    return pltpu.CompilerParams(dimension_semantics=sem, vmem_limit_bytes=VMEM_LIMIT)


def _rms(x, g):
    return x * lax.rsqrt(jnp.mean(x * x, axis=-1, keepdims=True) + EPS) * g


def _qkv_kernel(x_ref, g_ref, w_ref, qg_ref, kg_ref, cos_ref, sin_ref, q_ref, k_ref, v_ref):
    xn = _rms(x_ref[...], g_ref[...]).astype(BF16)
    z = jnp.dot(xn, w_ref[...], preferred_element_type=F32)
    c = cos_ref[...]
    s = sin_ref[...]

    def norm_rope(hd, gain, scale):
        hn = _rms(hd, gain)
        rot = pltpu.roll(hn, HEAD_DIM // 2, axis=1)
        return ((hn * c + rot * s) * scale).astype(BF16)

    for hh in range(N_HEADS):
        sl = slice(hh * HEAD_DIM, (hh + 1) * HEAD_DIM)
        q_ref[:, sl] = norm_rope(z[:, sl], qg_ref[...], HEAD_DIM ** -0.5 * LOG2_E)
    for hh in range(N_KV_HEADS):
        sl = slice(hh * HEAD_DIM, (hh + 1) * HEAD_DIM)
        zsl = slice(ATTN_WIDTH + hh * HEAD_DIM, ATTN_WIDTH + (hh + 1) * HEAD_DIM)
        k_ref[:, sl] = norm_rope(z[:, zsl], kg_ref[...], 1.0)
    v_ref[...] = z[:, ATTN_WIDTH + KV_WIDTH:].astype(BF16)


def _qkv_proj(x2, g, w, qg, kg, cos, sin, seq):
    n, d = x2.shape
    tm = min(PROJ_TM, seq)
    nt = seq // tm
    wq = w.shape[1]
    full = lambda i: (0, 0)
    return pl.pallas_call(
        _qkv_kernel,
        grid=(n // tm,),
        in_specs=[
            pl.BlockSpec((tm, d), lambda i: (i, 0)),
            pl.BlockSpec((1, d), full),
            pl.BlockSpec((d, wq), full),
            pl.BlockSpec((1, HEAD_DIM), full),
            pl.BlockSpec((1, HEAD_DIM), full),
            pl.BlockSpec((tm, HEAD_DIM), lambda i: (i % nt, 0)),
            pl.BlockSpec((tm, HEAD_DIM), lambda i: (i % nt, 0)),
        ],
        out_specs=[
            pl.BlockSpec((tm, ATTN_WIDTH), lambda i: (i, 0)),
            pl.BlockSpec((tm, KV_WIDTH), lambda i: (i, 0)),
            pl.BlockSpec((tm, KV_WIDTH), lambda i: (i, 0)),
        ],
        out_shape=[
            jax.ShapeDtypeStruct((n, ATTN_WIDTH), BF16),
            jax.ShapeDtypeStruct((n, KV_WIDTH), BF16),
            jax.ShapeDtypeStruct((n, KV_WIDTH), BF16),
        ],
        compiler_params=_params(("parallel",)),
        name="qkv_proj",
    )(x2, g, w, qg, kg, cos, sin)


def _dot_bf16x3(a, b):
    a_hi = a.astype(BF16)
    b_hi = b.astype(BF16)
    a_lo = (a - a_hi.astype(F32)).astype(BF16)
    b_lo = (b - b_hi.astype(F32)).astype(BF16)
    dot = functools.partial(jnp.dot, preferred_element_type=F32)
    return dot(a_hi, b_hi) + dot(a_lo, b_hi) + dot(a_hi, b_lo)


def _log_sigmoid(x):
    return jnp.minimum(x, 0.0) - jnp.log(1.0 + jnp.exp(-jnp.abs(x)))


def _gla_proj_kernel(x_ref, g_ref, w_ref, wg_ref, bg_ref,
                     gq_ref, gk_ref, gv_ref, sg_ref, laf_ref, lab_ref):
    xn = _rms(x_ref[...], g_ref[...]).astype(BF16)
    kw = GLA_HEADS * GLA_DK
    assert kw == PROJ_COLS and GLA_WIDTH == 2 * PROJ_COLS

    def project(g):
        cols = slice(g * PROJ_COLS, (g + 1) * PROJ_COLS)
        return jnp.dot(xn, w_ref[:, cols], preferred_element_type=F32)

    def silu(x):
        return x * (1.0 / (1.0 + jnp.exp(-x)))

    outputs = [
        (gq_ref, 0, lambda z: z * (GLA_DK ** -0.5)),
        (gk_ref, 0, lambda z: z),
        (gv_ref, 0, lambda z: z),
        (gv_ref, PROJ_COLS, lambda z: z),
        (sg_ref, 0, silu),
        (sg_ref, PROJ_COLS, silu),
    ]
    lr = jnp.dot(xn, w_ref[:, len(outputs) * PROJ_COLS:], preferred_element_type=F32)
    pre = _dot_bf16x3(lr, wg_ref[...]) + bg_ref[...]
    z = project(0)
    la = _log_sigmoid(pre) * (1.0 / GATE_NORMALIZER)
    laf_ref[...] = la[:, :kw]
    lab_ref[...] = la[:, kw:]
    for g, (ref, col, fn) in enumerate(outputs):
        z_next = project(g + 1) if g + 1 < len(outputs) else None
        ref[:, col:col + PROJ_COLS] = fn(z).astype(BF16)
        z = z_next


def _gla_proj(x2, g, w, wg, bg):
    n, d = x2.shape
    tm = PROJ_TM
    kw = GLA_HEADS * GLA_DK
    full = lambda i: (0, 0)
    row = lambda i: (i, 0)
    return pl.pallas_call(
        _gla_proj_kernel,
        grid=(n // tm,),
        in_specs=[
            pl.BlockSpec((tm, d), row),
            pl.BlockSpec((1, d), full),
            pl.BlockSpec(w.shape, full),
            pl.BlockSpec(wg.shape, full),
            pl.BlockSpec(bg.shape, full),
        ],
        out_specs=[
            pl.BlockSpec((tm, kw), row),
            pl.BlockSpec((tm, kw), row),
            pl.BlockSpec((tm, GLA_WIDTH), row),
            pl.BlockSpec((tm, GLA_WIDTH), row),
            pl.BlockSpec((tm, kw), row),
            pl.BlockSpec((tm, kw), row),
        ],
        out_shape=[
            jax.ShapeDtypeStruct((n, kw), BF16),
            jax.ShapeDtypeStruct((n, kw), BF16),
            jax.ShapeDtypeStruct((n, GLA_WIDTH), BF16),
            jax.ShapeDtypeStruct((n, GLA_WIDTH), BF16),
            jax.ShapeDtypeStruct((n, kw), F32),
            jax.ShapeDtypeStruct((n, kw), F32),
        ],
        compiler_params=_params(("parallel",)),
        name="gla_proj",
    )(x2, g, w, wg, bg)


def _attn_kernel(q_ref, k_ref, v_ref, o_ref):
    k = k_ref[0]
    v = v_ref[0]

    def scores(hh):
        sl = slice(hh * HEAD_DIM, (hh + 1) * HEAD_DIM)
        return lax.dot_general(q_ref[0, :, sl], k, NT_DIMS, preferred_element_type=F32)

    ahead = [scores(hh) for hh in range(ATTN_AHEAD)]
    for hh in range(GQA_GROUP):
        if hh + ATTN_AHEAD < GQA_GROUP:
            ahead.append(scores(hh + ATTN_AHEAD))
        s = ahead[hh]
        m = jnp.max(s, axis=-1, keepdims=True)
        p = jnp.exp2(s - m)
        l = jnp.sum(p, axis=-1, keepdims=True)
        o = jnp.dot(p.astype(BF16), v, preferred_element_type=F32)
        o_ref[0, :, hh * HEAD_DIM:(hh + 1) * HEAD_DIM] = (o / l).astype(BF16)


def _attention(q, k, v):
    b, t, _ = q.shape
    tq = min(ATTN_TQ, t)
    gw = GQA_GROUP * HEAD_DIM
    return pl.pallas_call(
        _attn_kernel,
        grid=(b, N_KV_HEADS, t // tq),
        in_specs=[
            pl.BlockSpec((1, tq, gw), lambda bi, g, qi: (bi, qi, g)),
            pl.BlockSpec((1, t, HEAD_DIM), lambda bi, g, qi: (bi, 0, g)),
            pl.BlockSpec((1, t, HEAD_DIM), lambda bi, g, qi: (bi, 0, g)),
        ],
        out_specs=pl.BlockSpec((1, tq, gw), lambda bi, g, qi: (bi, qi, g)),
        out_shape=jax.ShapeDtypeStruct((b, t, ATTN_WIDTH), BF16),
        compiler_params=_params(("parallel", "parallel", "arbitrary")),
        name="gqa_attention",
    )(q, k, v)


def _split_dot(tri, g):
    hi = g.astype(BF16)
    lo = (g - hi.astype(F32)).astype(BF16)
    return (jnp.dot(tri, hi, preferred_element_type=F32)
            + jnp.dot(tri, lo, preferred_element_type=F32))


def _gla_kernel(q_ref, k_ref, v_ref, sg_ref, gf_ref, gb_ref, gain_ref, o_ref, acc_ref,
                sf_ref, sb_ref):
    t = q_ref.shape[1]
    c = GLA_CHUNK
    n = t // c
    mid = c // 2
    ri = lax.broadcasted_iota(jnp.int32, (c, c), 0)
    ci = lax.broadcasted_iota(jnp.int32, (c, c), 1)
    lower = ri >= ci
    tri_lo = lower.astype(BF16)
    tri_up = (ri <= ci).astype(BF16)
    sf_ref[...] = jnp.zeros_like(sf_ref)
    sb_ref[...] = jnp.zeros_like(sb_ref)

    upper = ri <= ci
    un = min(GLA_UNROLL, n // 2)

    def finish(rows, o):
        y = _rms(o, gain_ref[...]) * sg_ref[0, rows, :].astype(F32)
        o_ref[0, rows, :] = y.astype(BF16)

    def trip(i, second_half):
        items = []
        for u in range(un):
            idx = i * un + u
            items.append((idx, sf_ref, gf_ref, tri_lo, c - 1, lower))
            items.append((n - 1 - idx, sb_ref, gb_ref, tri_up, 0, upper))
        rows = [pl.ds(pl.multiple_of(it[0] * c, c), c) for it in items]
        bs = [_split_dot(it[3], it[2][0, r, :]) for it, r in zip(items, rows)]
        qs, ks, scores = [], [], []
        for it, r, b in zip(items, rows, bs):
            q = q_ref[0, r, :].astype(F32)
            k = k_ref[0, r, :].astype(F32)
            ref = b[mid:mid + 1, :]
            qt = (q * jnp.exp(b - ref)).astype(BF16)
            kt = (k * jnp.exp(ref - b)).astype(BF16)
            a = lax.dot_general(qt, kt, NT_DIMS, preferred_element_type=F32)
            qs.append(q)
            ks.append(k)
            scores.append(jnp.where(it[5], a, 0.0).astype(BF16))
        intra, updates, tots = [], [], []
        for it, r, b, k, a in zip(items, rows, bs, ks, scores):
            v = v_ref[0, r, :]
            tot = b[it[4]:it[4] + 1, :]
            kd = (k * jnp.exp(tot - b)).astype(BF16)
            intra.append(jnp.dot(a, v, preferred_element_type=F32))
            updates.append(lax.dot_general(v, kd, TN_DIMS, preferred_element_type=F32))
            tots.append(tot)
        for it, r, b, q, o_in, upd, tot in zip(items, rows, bs, qs, intra, updates, tots):
            s_ref = it[1]
            st = s_ref[...]
            o = o_in + lax.dot_general((q * jnp.exp(b)).astype(BF16), st.astype(BF16), NT_DIMS,
                                       preferred_element_type=F32)
            s_ref[...] = st * jnp.exp(tot) + upd
            if second_half:
                finish(r, acc_ref[r, :] + o)
            else:
                acc_ref[r, :] = o

    half_trips = n // 2 // un

    def first_half(i, carry):
        trip(i, False)
        return carry

    def second_half(i, carry):
        trip(i, True)
        return carry

    lax.fori_loop(0, half_trips, first_half, 0)
    lax.fori_loop(half_trips, 2 * half_trips, second_half, 0)


def _gla(gq, gk, gv, sg, laf, lab, gain):
    b, t, _ = gq.shape
    kspec = pl.BlockSpec((1, t, GLA_DK), lambda bi, h: (bi, 0, h))
    vspec = pl.BlockSpec((1, t, GLA_DV), lambda bi, h: (bi, 0, h))
    return pl.pallas_call(
        _gla_kernel,
        grid=(b, GLA_HEADS),
        in_specs=[kspec, kspec, vspec, vspec, kspec, kspec,
                  pl.BlockSpec((1, GLA_DV), lambda bi, h: (0, 0))],
        out_specs=vspec,
        out_shape=jax.ShapeDtypeStruct((b, t, GLA_WIDTH), BF16),
        scratch_shapes=[
            pltpu.VMEM((t, GLA_DV), F32),
            pltpu.VMEM((GLA_DV, GLA_DK), F32),
            pltpu.VMEM((GLA_DV, GLA_DK), F32),
        ],
        compiler_params=_params(("parallel", "parallel")),
        name="gla_bidir",
    )(gq, gk, gv, sg, laf, lab, gain)


def _out_proj_kernel(x_ref, oa_ref, og_ref, ag_ref, wa_ref, wg_ref, fg_ref, h_ref, hn_ref):
    oa = _rms(oa_ref[...].astype(F32), ag_ref[...]).astype(BF16)
    h = (x_ref[...]
         + jnp.dot(oa, wa_ref[...], preferred_element_type=F32)
         + jnp.dot(og_ref[...], wg_ref[...], preferred_element_type=F32))
    h_ref[...] = h
    hn_ref[...] = _rms(h, fg_ref[...]).astype(BF16)


def _out_proj(x2, oa, og, ag, wa, wg, fg):
    n, d = x2.shape
    tm = PROJ_TM
    full = lambda i: (0, 0)
    row = lambda i: (i, 0)
    return pl.pallas_call(
        _out_proj_kernel,
        grid=(n // tm,),
        in_specs=[
            pl.BlockSpec((tm, d), row),
            pl.BlockSpec((tm, ATTN_WIDTH), row),
            pl.BlockSpec((tm, GLA_WIDTH), row),
            pl.BlockSpec((1, ATTN_WIDTH), full),
            pl.BlockSpec(wa.shape, full),
            pl.BlockSpec(wg.shape, full),
            pl.BlockSpec((1, d), full),
        ],
        out_specs=[pl.BlockSpec((tm, d), row), pl.BlockSpec((tm, d), row)],
        out_shape=[jax.ShapeDtypeStruct((n, d), F32), jax.ShapeDtypeStruct((n, d), BF16)],
        compiler_params=_params(("parallel",)),
        name="out_proj",
    )(x2, oa, og, ag, wa, wg, fg)


def _cmp_exchange(rows, i, j):
    a, b = rows[i], rows[j]
    if b is None:
        return
    if a is None:
        rows[i], rows[j] = b, None
        return
    rows[i], rows[j] = jnp.maximum(a, b), jnp.minimum(a, b)


def _bitonic_merge_desc(rows):
    n = len(rows)
    j = n // 2
    while j >= 1:
        for i in range(n):
            l = i ^ j
            if l > i:
                _cmp_exchange(rows, i, l)
        j //= 2


def _bitonic_sort_desc(rows):
    n = len(rows)
    k = 2
    while k <= n:
        j = k // 2
        while j >= 1:
            for i in range(n):
                l = i ^ j
                if l > i:
                    if (i & k) == 0:
                        _cmp_exchange(rows, i, l)
                    else:
                        _cmp_exchange(rows, l, i)
            j //= 2
        k *= 2


def _top_of_union(a, b):
    n = len(a)
    out = []
    for i in range(n):
        x, y = a[i], b[n - 1 - i]
        out.append(x if y is None else (y if x is None else jnp.maximum(x, y)))
    _bitonic_merge_desc(out)
    return out


def _top16_rows(s):
    l = s.shape[1]
    s3 = s.reshape(N_KEYS // SUBLANES, SUBLANES, l)
    rows = [s3[i] for i in range(N_KEYS // SUBLANES)]
    _bitonic_sort_desc(rows)
    for shift in (4, 2, 1):
        other = [pltpu.roll(r, shift, axis=0) for r in rows]
        rows = _top_of_union(rows, other)
    return rows


_CANDIDATES = [(i, j) for i in range(PEER_TOPK) for j in range(PEER_TOPK)
               if (i + 1) * (j + 1) <= PEER_TOPK]


def _bf16_bits(x):
    bits = lax.bitcast_convert_type(x, jnp.uint32)
    return (bits + jnp.uint32(0x7FFF) + ((bits >> 16) & jnp.uint32(1))) >> 16


def _bf16_pair_words(x):
    hi = _bf16_bits(x)
    return lax.bitcast_convert_type((hi << 16) | hi, jnp.int32)


def _bf16_packed_rows(x):
    hi = _bf16_bits(x)
    half = x.shape[0] // 2
    return lax.bitcast_convert_type(hi[:half] | (hi[half:] << 16), jnp.int32)


def _route_kernel(hn_ref, wqt_ref, keys_ref, sel_ref, row_ref, qt_ref):
    step = pl.program_id(1)

    @pl.when(step == 0)
    def _():
        qt_ref[...] = lax.dot_general(wqt_ref[...], hn_ref[...], NT_DIMS,
                                      preferred_element_type=F32).astype(BF16)

    for k in range(ROUTE_HEADS):
        _route_head(step * ROUTE_HEADS + k, k, keys_ref, sel_ref, row_ref, qt_ref)


def _route_head(hh, k, keys_ref, sel_ref, row_ref, qt_ref):
    base = pl.multiple_of(hh * PEER_QDIM, PEER_QDIM)
    q1 = qt_ref[pl.ds(base, PEER_HALF), :]
    q2 = qt_ref[pl.ds(base + PEER_HALF, PEER_HALF), :]
    s1 = jnp.dot(keys_ref[k, 0], q1, preferred_element_type=F32)
    s2 = jnp.dot(keys_ref[k, 1], q2, preferred_element_type=F32)
    tb = s1.shape[1]
    lgs = tb // LANES
    a = _top16_rows(s1)
    b = _top16_rows(s2)

    sub = lax.broadcasted_iota(jnp.int32, (SUBLANES, LANES), 0)

    def compact(r):
        out = r[:, :LANES]
        for g in range(1, lgs):
            out = jnp.where(sub == g, r[:, g * LANES:(g + 1) * LANES], out)
        return out

    def expand(c):
        return jnp.concatenate(
            [jnp.broadcast_to(c[g:g + 1, :], (SUBLANES, LANES)) for g in range(lgs)], axis=1)

    ac = [compact(r) for r in a]
    bc = [compact(r) for r in b]
    cand = [ac[i] + bc[j] for (i, j) in _CANDIDATES]
    cand += [None] * (64 - len(cand))
    groups = []
    for gi in range(4):
        grp = cand[gi * 16:(gi + 1) * 16]
        _bitonic_sort_desc(grp)
        groups.append(grp)
    top = _top_of_union(_top_of_union(groups[0], groups[1]),
                        _top_of_union(groups[2], groups[3]))
    z = jnp.zeros_like(top[0])
    for tk in top:
        z = z + jnp.exp(tk - top[0])
    tau = expand(top[PEER_TOPK - 1])
    inv_z = expand(1.0 / z)
    full = lambda r: jnp.broadcast_to(r[0:1, :], (N_KEYS, tb))
    tau_f = full(tau)
    cnt_bits = jnp.zeros((N_KEYS, tb), jnp.int32)
    rank_bits = jnp.zeros((N_KEYS, tb), jnp.int32)
    for j in range(PEER_TOPK):
        bj = full(b[j])
        bits = int(np.asarray(j + 1, dtype=np.float32).view(np.uint32)) >> 16
        cnt_bits = jnp.where(s1 + bj >= tau_f, bits, cnt_bits)
        rank_bits = jnp.where(bj > s2, bits, rank_bits)
    e2 = jnp.exp(s2 - full(b[0]))
    w1 = jnp.exp(s1 - full(a[0])) * full(inv_z)
    half = N_KEYS // 2
    rank_w = rank_bits[:half] | (rank_bits[half:] << 16)
    e2_w = _bf16_packed_rows(e2)
    cnt_w = cnt_bits | (cnt_bits << 16)
    w1_w = _bf16_pair_words(w1)
    for lg in range(tb // LANES):
        lanes = slice(lg * LANES, (lg + 1) * LANES)
        sel_ref[lg, k, 0] = rank_w[:, lanes]
        sel_ref[lg, k, 1] = e2_w[:, lanes]
        row_ref[lg, k, 0] = cnt_w[:, lanes]
        row_ref[lg, k, 1] = w1_w[:, lanes]


def _peer_route(hn, wqt, keys):
    n, d = hn.shape
    tb = min(ROUTE_TB, n)
    lgs = tb // LANES
    rows = jax.ShapeDtypeStruct((n // LANES, PEER_HEADS, 2, N_KEYS, LANES), jnp.int32)
    sels = jax.ShapeDtypeStruct((n // LANES, PEER_HEADS, 2, N_KEYS // 2, LANES), jnp.int32)
    rspec = pl.BlockSpec((lgs, ROUTE_HEADS, 2, N_KEYS, LANES), lambda i, h: (i, h, 0, 0, 0))
    sspec = pl.BlockSpec((lgs, ROUTE_HEADS, 2, N_KEYS // 2, LANES), lambda i, h: (i, h, 0, 0, 0))
    return pl.pallas_call(
        _route_kernel,
        grid=(n // tb, PEER_HEADS // ROUTE_HEADS),
        in_specs=[
            pl.BlockSpec((tb, d), lambda i, h: (i, 0)),
            pl.BlockSpec(wqt.shape, lambda i, h: (0, 0), pipeline_mode=pl.Buffered(1)),
            pl.BlockSpec((ROUTE_HEADS, 2, N_KEYS, PEER_HALF), lambda i, h: (h, 0, 0, 0)),
        ],
        out_specs=[sspec, rspec],
        out_shape=[sels, rows],
        scratch_shapes=[pltpu.VMEM((PEER_HEADS * PEER_QDIM, tb), BF16)],
        compiler_params=_params(("parallel", "arbitrary")),
        name="peer_route",
    )(hn, wqt, keys)


def _gelu(x):
    return 0.5 * x * (1.0 + lax.erf(x * (2.0 ** -0.5)))


def _peer_kernel(hn_ref, h_ref, u0_ref, u_ref, vt_ref, sel_ref, row_ref, fg_ref,
                 y_ref, acc_ref, hta_ref, htb_ref, w_ref):
    j = pl.program_id(1)
    nj = pl.num_programs(1)
    lane_groups, te, _ = hta_ref.shape
    rows_per_step = te // N_KEYS

    def score(u_tile_ref, dst_ref, hn=None):
        hn = hn_ref[...] if hn is None else hn
        ht = lax.dot_general(u_tile_ref[...], hn, NT_DIMS,
                             preferred_element_type=F32)
        for lg in range(lane_groups):
            dst_ref[lg] = ht[:, lg * LANES:(lg + 1) * LANES]

    def released_by(gates):
        hn = hn_ref[...]
        tb, d = hn.shape
        n_tiles, k_tiles = tb // MXU_DIM, d // MXU_DIM
        groups = MXU_DIM // BF16_ROWS
        assert len(gates) == n_tiles * k_tiles * groups
        rows = []
        for nt in range(n_tiles):
            for rg in range(groups):
                r0 = nt * MXU_DIM + rg * BF16_ROWS
                cols = []
                for kt in range(k_tiles):
                    g = gates[((nt * k_tiles + kt) * groups + rg) * RELEASE_NUM // RELEASE_DEN]
                    z = g * jnp.zeros_like(g)
                    z = jnp.concatenate([z] * (MXU_DIM // LANES), axis=1)
                    cols.append(hn[r0:r0 + BF16_ROWS, kt * MXU_DIM:(kt + 1) * MXU_DIM] + z)
                rows.append(jnp.concatenate(cols, axis=1))
        return jnp.concatenate(rows, axis=0)

    @pl.when(j == 0)
    def _():
        acc_ref[...] = jnp.zeros_like(acc_ref)

    @pl.when((pl.program_id(0) == 0) & (j == 0))
    def _():
        score(u0_ref, hta_ref)

    def step(cur_ref, nxt_ref):
        zero = jnp.zeros((BF16_ROWS, LANES), BF16)
        gates = []

        def row_tile(which, lg, hh, rr):
            words = jnp.broadcast_to(row_ref[lg, hh, which, rr:rr + 1, :], (SUBLANES, LANES))
            return pltpu.bitcast(words, BF16)

        for lg in range(lane_groups):
            lanes = slice(lg * LANES, (lg + 1) * LANES)
            for rr in range(rows_per_step):
                cnt = [row_tile(0, lg, hh, rr) for hh in range(PEER_HEADS)]
                w1 = [row_tile(1, lg, hh, rr) for hh in range(PEER_HEADS)]
                for kk in range(N_KEYS // BF16_ROWS):
                    words = slice(kk * SUBLANES, (kk + 1) * SUBLANES)
                    rows = slice(rr * N_KEYS + kk * BF16_ROWS, rr * N_KEYS + (kk + 1) * BF16_ROWS)
                    gate = zero
                    for hh in range(PEER_HEADS):
                        rank = pltpu.bitcast(sel_ref[lg, hh, 0, words, :], BF16)
                        e2 = pltpu.bitcast(sel_ref[lg, hh, 1, words, :], BF16)
                        gate = gate + jnp.where(rank < cnt[hh], e2, zero) * w1[hh]
                    gates.append(gate)
                    w_ref[rows, lanes] = gate * _gelu(cur_ref[lg, rows, :]).astype(BF16)
        score(u_ref, nxt_ref, released_by(gates))
        acc_ref[...] += jnp.dot(vt_ref[...], w_ref[...], preferred_element_type=F32)

    @pl.when(j % 2 == 0)
    def _():
        step(hta_ref, htb_ref)

    @pl.when(j % 2 == 1)
    def _():
        step(htb_ref, hta_ref)

    @pl.when(j == nj - 1)
    def _():
        h = h_ref[...] + acc_ref[...].T
        y_ref[...] = _rms(h, fg_ref[...])


def _peer_dense(hn, h, u, vt, sel, row, fg):
    n, d = hn.shape
    e = u.shape[0]
    tb, te = PEER_TB, PEER_TE
    nj = e // te
    ni = n // tb
    assert nj % 2 == 0, "score buffers alternate with the step parity across token blocks"
    lgs = tb // LANES
    rows = te // N_KEYS
    once = pl.Buffered(1)
    tspec = pl.BlockSpec((lgs, PEER_HEADS, 2, N_KEYS // 2, LANES), lambda i, j: (i, 0, 0, 0, 0))
    rspec = pl.BlockSpec((lgs, PEER_HEADS, 2, rows, LANES), lambda i, j: (i, 0, 0, j, 0))
    return pl.pallas_call(
        _peer_kernel,
        grid=(ni, nj),
        in_specs=[
            pl.BlockSpec((tb, d), lambda i, j: (jnp.minimum(i + (j + 1) // nj, ni - 1), 0)),
            pl.BlockSpec((tb, d), lambda i, j: (i, 0)),
            pl.BlockSpec((te, d), lambda i, j: (0, 0), pipeline_mode=once),
            pl.BlockSpec((te, d), lambda i, j: ((j + 1) % nj, 0)),
            pl.BlockSpec((None, d, te), lambda i, j: (j, 0, 0)),
            tspec, rspec,
            pl.BlockSpec((1, d), lambda i, j: (0, 0)),
        ],
        out_specs=pl.BlockSpec((tb, d), lambda i, j: (i, 0)),
        out_shape=jax.ShapeDtypeStruct((n, d), F32),
        scratch_shapes=[
            pltpu.VMEM((d, tb), F32),
            pltpu.VMEM((lgs, te, LANES), F32),
            pltpu.VMEM((lgs, te, LANES), F32),
            pltpu.VMEM((te, tb), BF16),
        ],
        compiler_params=_params(("arbitrary", "arbitrary")),
        name="peer_dense",
    )(hn, h, u, u, vt, sel, row, fg)


def _rope_tables(t):
    n_rows = t // GRID_W
    row = jnp.repeat(jnp.arange(n_rows, dtype=F32), GRID_W)
    col = jnp.tile(jnp.arange(GRID_W, dtype=F32), n_rows)
    freqs = ROPE_THETA ** (-jnp.arange(0, AXIS_DIM, 2, dtype=F32) / AXIS_DIM)
    ang = jnp.concatenate([row[:, None] * freqs, col[:, None] * freqs], axis=-1)
    cos, sin = jnp.cos(ang), jnp.sin(ang)
    return jnp.concatenate([cos, cos], axis=-1), jnp.concatenate([-sin, sin], axis=-1)


def _prepare(norm_mix_g, w_in, q_norm_g, k_norm_g, w_gate_f2, b_gate_f, w_gate_b2, b_gate_b,
             gla_norm_g, attn_out_norm_g, w_out, norm_ffn_g, peer_w_q, peer_sub_keys, peer_u,
             peer_v, final_norm_g):
    perm = np.concatenate([np.arange(0, HEAD_DIM, 2), np.arange(1, HEAD_DIM, 2)])
    kw = GLA_HEADS * GLA_DK
    o_k = ATTN_WIDTH
    o_v = o_k + KV_WIDTH
    o_g = o_v + KV_WIDTH
    o_lr = o_g + 2 * kw + 2 * GLA_WIDTH
    wq = w_in[:, :o_k].reshape(D_MODEL, N_HEADS, HEAD_DIM)[:, :, perm].reshape(D_MODEL, o_k)
    wk = w_in[:, o_k:o_v].reshape(D_MODEL, N_KV_HEADS, HEAD_DIM)[:, :, perm].reshape(D_MODEL, KV_WIDTH)
    w_qkv = jnp.concatenate([wq, wk, w_in[:, o_v:o_g]], axis=1).astype(BF16)
    w_lr = jnp.pad(w_in[:, o_lr:], ((0, 0), (0, LANES - 2 * GATE_RANK)))
    w_gla = jnp.concatenate([w_in[:, o_g:o_lr], w_lr], axis=1).astype(BF16)
    wg = jnp.zeros((LANES, 2 * kw), F32)
    wg = wg.at[:GATE_RANK, :kw].set(w_gate_f2).at[GATE_RANK:2 * GATE_RANK, kw:].set(w_gate_b2)
    bg = jnp.concatenate([b_gate_f, b_gate_b])[None, :]

    keys = jnp.stack([peer_sub_keys[:, 0], peer_sub_keys[:, 1][:, PEER_KEY2_ROWS]], axis=1)
    return dict(
        norm_mix_g=norm_mix_g[None, :], w_qkv=w_qkv, w_gla=w_gla, wg=wg, bg=bg,
        qg=q_norm_g[perm][None, :], kg=k_norm_g[perm][None, :],
        gla_gain=gla_norm_g[None, :], attn_gain=attn_out_norm_g[None, :],
        wo_a=w_out[:ATTN_WIDTH].astype(BF16), wo_g=w_out[ATTN_WIDTH:].astype(BF16),
        ffn_g=norm_ffn_g[None, :], wqt=peer_w_q.T.astype(BF16),
        keys=keys.astype(BF16), u=peer_u.astype(BF16),
        vt=peer_v.astype(BF16).reshape(-1, PEER_TE, D_MODEL).transpose(0, 2, 1),
        final_g=final_norm_g[None, :],
    )


def _layer(x, p):
    b, t, d = x.shape
    n = b * t
    x2 = x.reshape(n, d)
    cos, sin = _rope_tables(t)
    q, k, v = _qkv_proj(x2, p["norm_mix_g"], p["w_qkv"], p["qg"], p["kg"], cos, sin, t)
    gq, gk, gv, sg, laf, lab = _gla_proj(x2, p["norm_mix_g"], p["w_gla"], p["wg"], p["bg"])
    oa = _attention(q.reshape(b, t, -1), k.reshape(b, t, -1), v.reshape(b, t, -1))
    r3 = lambda a: a.reshape(b, t, -1)
    og = _gla(r3(gq), r3(gk), r3(gv), r3(sg), r3(laf), r3(lab), p["gla_gain"])
    h, hn = _out_proj(x2, oa.reshape(n, -1), og.reshape(n, -1), p["attn_gain"],
                      p["wo_a"], p["wo_g"], p["ffn_g"])
    sel, row = _peer_route(hn, p["wqt"], p["keys"])
    y = _peer_dense(hn, h, p["u"], p["vt"], sel, row, p["final_g"])
    return y.reshape(b, t, d)


def kernel(x_prompt, x_sample, norm_mix_g, w_in, q_norm_g, k_norm_g, w_gate_f2, b_gate_f,
           w_gate_b2, b_gate_b, gla_norm_g, attn_out_norm_g, w_out, norm_ffn_g, peer_w_q,
           peer_sub_keys, peer_u, peer_v, final_norm_g):
    p = _prepare(norm_mix_g, w_in, q_norm_g, k_norm_g, w_gate_f2, b_gate_f, w_gate_b2,
                 b_gate_b, gla_norm_g, attn_out_norm_g, w_out, norm_ffn_g, peer_w_q,
                 peer_sub_keys, peer_u, peer_v, final_norm_g)
    return (_layer(x_prompt, p), _layer(x_sample, p))
```

```python
import functools
import math

import jax
import jax.numpy as jnp
import numpy as np
from jax import lax
from jax.experimental import pallas as pl
from jax.experimental.pallas import tpu as pltpu

F32 = jnp.float32
BF16 = jnp.bfloat16

D_MODEL = 2048
GRID_W = 64
HEAD_DIM = 128
N_HEADS = 8
N_KV_HEADS = 2
GQA_GROUP = N_HEADS // N_KV_HEADS
ATTN_WIDTH = N_HEADS * HEAD_DIM
KV_WIDTH = N_KV_HEADS * HEAD_DIM
ROPE_THETA = 10000.0
AXIS_DIM = HEAD_DIM // 2
GLA_HEADS = 4
GLA_DK = 128
GLA_DV = 256
GLA_WIDTH = GLA_HEADS * GLA_DV
GATE_RANK = 16
GATE_NORMALIZER = 16.0
PEER_HEADS = 8
N_KEYS = 128
PEER_TOPK = 16
PEER_HALF = 128
PEER_QDIM = 2 * PEER_HALF
EPS = 1e-6
LOG2_E = math.log2(math.e)

LANES = 128
SUBLANES = 8
BF16_ROWS = 16
MXU_DIM = 256
VMEM_LIMIT = 58 * 1024 * 1024

PROJ_TM = 512
PROJ_COLS = 512
ATTN_TQ = 256
ATTN_AHEAD = 2
GLA_CHUNK = 64
GLA_UNROLL = 16
ROUTE_TB = 512
ROUTE_HEADS = 2
PEER_TB = 512
RELEASE_NUM, RELEASE_DEN = 1, 2
PEER_TE = 1024

PEER_KEY2_ROWS = np.concatenate([np.arange(0, N_KEYS, 2), np.arange(1, N_KEYS, 2)])

NT_DIMS = (((1,), (1,)), ((), ()))
TN_DIMS = (((0,), (0,)), ((), ()))


def _params(sem):
    return pltpu.CompilerParams(dimension_semantics=sem, vmem_limit_bytes=VMEM_LIMIT)


def _rms(x, g):
    return x * lax.rsqrt(jnp.mean(x * x, axis=-1, keepdims=True) + EPS) * g


def _qkv_kernel(x_ref, g_ref, w_ref, qg_ref, kg_ref, cos_ref, sin_ref, q_ref, k_ref, v_ref):
    xn = _rms(x_ref[...], g_ref[...]).astype(BF16)
    z = jnp.dot(xn, w_ref[...], preferred_element_type=F32)
    c = cos_ref[...]
    s = sin_ref[...]

    def norm_rope(hd, gain, scale):
        hn = _rms(hd, gain)
        rot = pltpu.roll(hn, HEAD_DIM // 2, axis=1)
        return ((hn * c + rot * s) * scale).astype(BF16)

    for hh in range(N_HEADS):
        sl = slice(hh * HEAD_DIM, (hh + 1) * HEAD_DIM)
        q_ref[:, sl] = norm_rope(z[:, sl], qg_ref[...], HEAD_DIM ** -0.5 * LOG2_E)
    for hh in range(N_KV_HEADS):
        sl = slice(hh * HEAD_DIM, (hh + 1) * HEAD_DIM)
        zsl = slice(ATTN_WIDTH + hh * HEAD_DIM, ATTN_WIDTH + (hh + 1) * HEAD_DIM)
        k_ref[:, sl] = norm_rope(z[:, zsl], kg_ref[...], 1.0)
    v_ref[...] = z[:, ATTN_WIDTH + KV_WIDTH:].astype(BF16)


def _qkv_proj(x2, g, w, qg, kg, cos, sin, seq):
    n, d = x2.shape
    tm = min(PROJ_TM, seq)
    nt = seq // tm
    wq = w.shape[1]
    full = lambda i: (0, 0)
    return pl.pallas_call(
        _qkv_kernel,
        grid=(n // tm,),
        in_specs=[
            pl.BlockSpec((tm, d), lambda i: (i, 0)),
            pl.BlockSpec((1, d), full),
            pl.BlockSpec((d, wq), full),
            pl.BlockSpec((1, HEAD_DIM), full),
            pl.BlockSpec((1, HEAD_DIM), full),
            pl.BlockSpec((tm, HEAD_DIM), lambda i: (i % nt, 0)),
            pl.BlockSpec((tm, HEAD_DIM), lambda i: (i % nt, 0)),
        ],
        out_specs=[
            pl.BlockSpec((tm, ATTN_WIDTH), lambda i: (i, 0)),
            pl.BlockSpec((tm, KV_WIDTH), lambda i: (i, 0)),
            pl.BlockSpec((tm, KV_WIDTH), lambda i: (i, 0)),
        ],
        out_shape=[
            jax.ShapeDtypeStruct((n, ATTN_WIDTH), BF16),
            jax.ShapeDtypeStruct((n, KV_WIDTH), BF16),
            jax.ShapeDtypeStruct((n, KV_WIDTH), BF16),
        ],
        compiler_params=_params(("parallel",)),
        name="qkv_proj",
    )(x2, g, w, qg, kg, cos, sin)


def _dot_bf16x3(a, b):
    a_hi = a.astype(BF16)
    b_hi = b.astype(BF16)
    a_lo = (a - a_hi.astype(F32)).astype(BF16)
    b_lo = (b - b_hi.astype(F32)).astype(BF16)
    dot = functools.partial(jnp.dot, preferred_element_type=F32)
    return dot(a_hi, b_hi) + dot(a_lo, b_hi) + dot(a_hi, b_lo)


def _log_sigmoid(x):
    return jnp.minimum(x, 0.0) - jnp.log(1.0 + jnp.exp(-jnp.abs(x)))


def _gla_proj_kernel(x_ref, g_ref, w_ref, wg_ref, bg_ref,
                     gq_ref, gk_ref, gv_ref, sg_ref, laf_ref, lab_ref):
    xn = _rms(x_ref[...], g_ref[...]).astype(BF16)
    kw = GLA_HEADS * GLA_DK
    assert kw == PROJ_COLS and GLA_WIDTH == 2 * PROJ_COLS

    def project(g):
        cols = slice(g * PROJ_COLS, (g + 1) * PROJ_COLS)
        return jnp.dot(xn, w_ref[:, cols], preferred_element_type=F32)

    def silu(x):
        return x * (1.0 / (1.0 + jnp.exp(-x)))

    outputs = [
        (gq_ref, 0, lambda z: z * (GLA_DK ** -0.5)),
        (gk_ref, 0, lambda z: z),
        (gv_ref, 0, lambda z: z),
        (gv_ref, PROJ_COLS, lambda z: z),
        (sg_ref, 0, silu),
        (sg_ref, PROJ_COLS, silu),
    ]
    lr = jnp.dot(xn, w_ref[:, len(outputs) * PROJ_COLS:], preferred_element_type=F32)
    pre = _dot_bf16x3(lr, wg_ref[...]) + bg_ref[...]
    z = project(0)
    la = _log_sigmoid(pre) * (1.0 / GATE_NORMALIZER)
    laf_ref[...] = la[:, :kw]
    lab_ref[...] = la[:, kw:]
    for g, (ref, col, fn) in enumerate(outputs):
        z_next = project(g + 1) if g + 1 < len(outputs) else None
        ref[:, col:col + PROJ_COLS] = fn(z).astype(BF16)
        z = z_next


def _gla_proj(x2, g, w, wg, bg):
    n, d = x2.shape
    tm = PROJ_TM
    kw = GLA_HEADS * GLA_DK
    full = lambda i: (0, 0)
    row = lambda i: (i, 0)
    return pl.pallas_call(
        _gla_proj_kernel,
        grid=(n // tm,),
        in_specs=[
            pl.BlockSpec((tm, d), row),
            pl.BlockSpec((1, d), full),
            pl.BlockSpec(w.shape, full),
            pl.BlockSpec(wg.shape, full),
            pl.BlockSpec(bg.shape, full),
        ],
        out_specs=[
            pl.BlockSpec((tm, kw), row),
            pl.BlockSpec((tm, kw), row),
            pl.BlockSpec((tm, GLA_WIDTH), row),
            pl.BlockSpec((tm, GLA_WIDTH), row),
            pl.BlockSpec((tm, kw), row),
            pl.BlockSpec((tm, kw), row),
        ],
        out_shape=[
            jax.ShapeDtypeStruct((n, kw), BF16),
            jax.ShapeDtypeStruct((n, kw), BF16),
            jax.ShapeDtypeStruct((n, GLA_WIDTH), BF16),
            jax.ShapeDtypeStruct((n, GLA_WIDTH), BF16),
            jax.ShapeDtypeStruct((n, kw), F32),
            jax.ShapeDtypeStruct((n, kw), F32),
        ],
        compiler_params=_params(("parallel",)),
        name="gla_proj",
    )(x2, g, w, wg, bg)


def _attn_kernel(q_ref, k_ref, v_ref, o_ref):
    k = k_ref[0]
    v = v_ref[0]

    def scores(hh):
        sl = slice(hh * HEAD_DIM, (hh + 1) * HEAD_DIM)
        return lax.dot_general(q_ref[0, :, sl], k, NT_DIMS, preferred_element_type=F32)

    ahead = [scores(hh) for hh in range(ATTN_AHEAD)]
    for hh in range(GQA_GROUP):
        if hh + ATTN_AHEAD < GQA_GROUP:
            ahead.append(scores(hh + ATTN_AHEAD))
        s = ahead[hh]
        m = jnp.max(s, axis=-1, keepdims=True)
        p = jnp.exp2(s - m)
        l = jnp.sum(p, axis=-1, keepdims=True)
        o = jnp.dot(p.astype(BF16), v, preferred_element_type=F32)
        o_ref[0, :, hh * HEAD_DIM:(hh + 1) * HEAD_DIM] = (o / l).astype(BF16)


def _attention(q, k, v):
    b, t, _ = q.shape
    tq = min(ATTN_TQ, t)
    gw = GQA_GROUP * HEAD_DIM
    return pl.pallas_call(
        _attn_kernel,
        grid=(b, N_KV_HEADS, t // tq),
        in_specs=[
            pl.BlockSpec((1, tq, gw), lambda bi, g, qi: (bi, qi, g)),
            pl.BlockSpec((1, t, HEAD_DIM), lambda bi, g, qi: (bi, 0, g)),
            pl.BlockSpec((1, t, HEAD_DIM), lambda bi, g, qi: (bi, 0, g)),
        ],
        out_specs=pl.BlockSpec((1, tq, gw), lambda bi, g, qi: (bi, qi, g)),
        out_shape=jax.ShapeDtypeStruct((b, t, ATTN_WIDTH), BF16),
        compiler_params=_params(("parallel", "parallel", "arbitrary")),
        name="gqa_attention",
    )(q, k, v)


def _split_dot(tri, g):
    hi = g.astype(BF16)
    lo = (g - hi.astype(F32)).astype(BF16)
    return (jnp.dot(tri, hi, preferred_element_type=F32)
            + jnp.dot(tri, lo, preferred_element_type=F32))


def _gla_kernel(q_ref, k_ref, v_ref, sg_ref, gf_ref, gb_ref, gain_ref, o_ref, acc_ref,
                sf_ref, sb_ref):
    t = q_ref.shape[1]
    c = GLA_CHUNK
    n = t // c
    mid = c // 2
    ri = lax.broadcasted_iota(jnp.int32, (c, c), 0)
    ci = lax.broadcasted_iota(jnp.int32, (c, c), 1)
    lower = ri >= ci
    tri_lo = lower.astype(BF16)
    tri_up = (ri <= ci).astype(BF16)
    sf_ref[...] = jnp.zeros_like(sf_ref)
    sb_ref[...] = jnp.zeros_like(sb_ref)

    upper = ri <= ci
    un = min(GLA_UNROLL, n // 2)

    def finish(rows, o):
        y = _rms(o, gain_ref[...]) * sg_ref[0, rows, :].astype(F32)
        o_ref[0, rows, :] = y.astype(BF16)

    def trip(i, second_half):
        items = []
        for u in range(un):
            idx = i * un + u
            items.append((idx, sf_ref, gf_ref, tri_lo, c - 1, lower))
            items.append((n - 1 - idx, sb_ref, gb_ref, tri_up, 0, upper))
        rows = [pl.ds(pl.multiple_of(it[0] * c, c), c) for it in items]
        bs = [_split_dot(it[3], it[2][0, r, :]) for it, r in zip(items, rows)]
        qs, ks, scores = [], [], []
        for it, r, b in zip(items, rows, bs):
            q = q_ref[0, r, :].astype(F32)
            k = k_ref[0, r, :].astype(F32)
            ref = b[mid:mid + 1, :]
            qt = (q * jnp.exp(b - ref)).astype(BF16)
            kt = (k * jnp.exp(ref - b)).astype(BF16)
            a = lax.dot_general(qt, kt, NT_DIMS, preferred_element_type=F32)
            qs.append(q)
            ks.append(k)
            scores.append(jnp.where(it[5], a, 0.0).astype(BF16))
        intra, updates, tots = [], [], []
        for it, r, b, k, a in zip(items, rows, bs, ks, scores):
            v = v_ref[0, r, :]
            tot = b[it[4]:it[4] + 1, :]
            kd = (k * jnp.exp(tot - b)).astype(BF16)
            intra.append(jnp.dot(a, v, preferred_element_type=F32))
            updates.append(lax.dot_general(v, kd, TN_DIMS, preferred_element_type=F32))
            tots.append(tot)
        for it, r, b, q, o_in, upd, tot in zip(items, rows, bs, qs, intra, updates, tots):
            s_ref = it[1]
            st = s_ref[...]
            o = o_in + lax.dot_general((q * jnp.exp(b)).astype(BF16), st.astype(BF16), NT_DIMS,
                                       preferred_element_type=F32)
            s_ref[...] = st * jnp.exp(tot) + upd
            if second_half:
                finish(r, acc_ref[r, :] + o)
            else:
                acc_ref[r, :] = o

    half_trips = n // 2 // un

    def first_half(i, carry):
        trip(i, False)
        return carry

    def second_half(i, carry):
        trip(i, True)
        return carry

    lax.fori_loop(0, half_trips, first_half, 0)
    lax.fori_loop(half_trips, 2 * half_trips, second_half, 0)


def _gla(gq, gk, gv, sg, laf, lab, gain):
    b, t, _ = gq.shape
    kspec = pl.BlockSpec((1, t, GLA_DK), lambda bi, h: (bi, 0, h))
    vspec = pl.BlockSpec((1, t, GLA_DV), lambda bi, h: (bi, 0, h))
    return pl.pallas_call(
        _gla_kernel,
        grid=(b, GLA_HEADS),
        in_specs=[kspec, kspec, vspec, vspec, kspec, kspec,
                  pl.BlockSpec((1, GLA_DV), lambda bi, h: (0, 0))],
        out_specs=vspec,
        out_shape=jax.ShapeDtypeStruct((b, t, GLA_WIDTH), BF16),
        scratch_shapes=[
            pltpu.VMEM((t, GLA_DV), F32),
            pltpu.VMEM((GLA_DV, GLA_DK), F32),
            pltpu.VMEM((GLA_DV, GLA_DK), F32),
        ],
        compiler_params=_params(("parallel", "parallel")),
        name="gla_bidir",
    )(gq, gk, gv, sg, laf, lab, gain)


def _out_proj_kernel(x_ref, oa_ref, og_ref, ag_ref, wa_ref, wg_ref, fg_ref, h_ref, hn_ref):
    oa = _rms(oa_ref[...].astype(F32), ag_ref[...]).astype(BF16)
    h = (x_ref[...]
         + jnp.dot(oa, wa_ref[...], preferred_element_type=F32)
         + jnp.dot(og_ref[...], wg_ref[...], preferred_element_type=F32))
    h_ref[...] = h
    hn_ref[...] = _rms(h, fg_ref[...]).astype(BF16)


def _out_proj(x2, oa, og, ag, wa, wg, fg):
    n, d = x2.shape
    tm = PROJ_TM
    full = lambda i: (0, 0)
    row = lambda i: (i, 0)
    return pl.pallas_call(
        _out_proj_kernel,
        grid=(n // tm,),
        in_specs=[
            pl.BlockSpec((tm, d), row),
            pl.BlockSpec((tm, ATTN_WIDTH), row),
            pl.BlockSpec((tm, GLA_WIDTH), row),
            pl.BlockSpec((1, ATTN_WIDTH), full),
            pl.BlockSpec(wa.shape, full),
            pl.BlockSpec(wg.shape, full),
            pl.BlockSpec((1, d), full),
        ],
        out_specs=[pl.BlockSpec((tm, d), row), pl.BlockSpec((tm, d), row)],
        out_shape=[jax.ShapeDtypeStruct((n, d), F32), jax.ShapeDtypeStruct((n, d), BF16)],
        compiler_params=_params(("parallel",)),
        name="out_proj",
    )(x2, oa, og, ag, wa, wg, fg)


def _cmp_exchange(rows, i, j):
    a, b = rows[i], rows[j]
    if b is None:
        return
    if a is None:
        rows[i], rows[j] = b, None
        return
    rows[i], rows[j] = jnp.maximum(a, b), jnp.minimum(a, b)


def _bitonic_merge_desc(rows):
    n = len(rows)
    j = n // 2
    while j >= 1:
        for i in range(n):
            l = i ^ j
            if l > i:
                _cmp_exchange(rows, i, l)
        j //= 2


def _bitonic_sort_desc(rows):
    n = len(rows)
    k = 2
    while k <= n:
        j = k // 2
        while j >= 1:
            for i in range(n):
                l = i ^ j
                if l > i:
                    if (i & k) == 0:
                        _cmp_exchange(rows, i, l)
                    else:
                        _cmp_exchange(rows, l, i)
            j //= 2
        k *= 2


def _top_of_union(a, b):
    n = len(a)
    out = []
    for i in range(n):
        x, y = a[i], b[n - 1 - i]
        out.append(x if y is None else (y if x is None else jnp.maximum(x, y)))
    _bitonic_merge_desc(out)
    return out


def _top16_rows(s):
    l = s.shape[1]
    s3 = s.reshape(N_KEYS // SUBLANES, SUBLANES, l)
    rows = [s3[i] for i in range(N_KEYS // SUBLANES)]
    _bitonic_sort_desc(rows)
    for shift in (4, 2, 1):
        other = [pltpu.roll(r, shift, axis=0) for r in rows]
        rows = _top_of_union(rows, other)
    return rows


_CANDIDATES = [(i, j) for i in range(PEER_TOPK) for j in range(PEER_TOPK)
               if (i + 1) * (j + 1) <= PEER_TOPK]


def _bf16_bits(x):
    return lax.bitcast_convert_type(x.astype(BF16).astype(F32), jnp.uint32) >> 16


def _bf16_pair_words(x):
    hi = _bf16_bits(x)
    return lax.bitcast_convert_type((hi << 16) | hi, jnp.int32)


def _bf16_packed_rows(x):
    hi = _bf16_bits(x)
    half = x.shape[0] // 2
    return lax.bitcast_convert_type(hi[:half] | (hi[half:] << 16), jnp.int32)


def _route_kernel(hn_ref, wqt_ref, keys_ref, sel_ref, row_ref, qt_ref):
    step = pl.program_id(1)

    @pl.when(step == 0)
    def _():
        qt_ref[...] = lax.dot_general(wqt_ref[...], hn_ref[...], NT_DIMS,
                                      preferred_element_type=F32).astype(BF16)

    for k in range(ROUTE_HEADS):
        _route_head(step * ROUTE_HEADS + k, k, keys_ref, sel_ref, row_ref, qt_ref)


def _route_head(hh, k, keys_ref, sel_ref, row_ref, qt_ref):
    base = pl.multiple_of(hh * PEER_QDIM, PEER_QDIM)
    q1 = qt_ref[pl.ds(base, PEER_HALF), :]
    q2 = qt_ref[pl.ds(base + PEER_HALF, PEER_HALF), :]
    s1 = jnp.dot(keys_ref[k, 0], q1, preferred_element_type=F32)
    s2 = jnp.dot(keys_ref[k, 1], q2, preferred_element_type=F32)
    tb = s1.shape[1]
    lgs = tb // LANES
    a = _top16_rows(s1)
    b = _top16_rows(s2)

    sub = lax.broadcasted_iota(jnp.int32, (SUBLANES, LANES), 0)

    def compact(r):
        out = r[:, :LANES]
        for g in range(1, lgs):
            out = jnp.where(sub == g, r[:, g * LANES:(g + 1) * LANES], out)
        return out

    def expand(c):
        return jnp.concatenate(
            [jnp.broadcast_to(c[g:g + 1, :], (SUBLANES, LANES)) for g in range(lgs)], axis=1)

    ac = [compact(r) for r in a]
    bc = [compact(r) for r in b]
    cand = [ac[i] + bc[j] for (i, j) in _CANDIDATES]
    cand += [None] * (64 - len(cand))
    groups = []
    for gi in range(4):
        grp = cand[gi * 16:(gi + 1) * 16]
        _bitonic_sort_desc(grp)
        groups.append(grp)
    top = _top_of_union(_top_of_union(groups[0], groups[1]),
                        _top_of_union(groups[2], groups[3]))
    z = jnp.zeros_like(top[0])
    for tk in top:
        z = z + jnp.exp(tk - top[0])
    tau = expand(top[PEER_TOPK - 1])
    inv_z = expand(1.0 / z)
    full = lambda r: jnp.broadcast_to(r[0:1, :], (N_KEYS, tb))
    tau_f = full(tau)
    cnt_bits = jnp.zeros((N_KEYS, tb), jnp.int32)
    rank_bits = jnp.zeros((N_KEYS, tb), jnp.int32)
    for j in range(PEER_TOPK):
        bj = full(b[j])
        bits = int(np.asarray(j + 1, dtype=np.float32).view(np.uint32)) >> 16
        cnt_bits = jnp.where(s1 + bj >= tau_f, bits, cnt_bits)
        rank_bits = jnp.where(bj > s2, bits, rank_bits)
    e2 = jnp.exp(s2 - full(b[0]))
    w1 = jnp.exp(s1 - full(a[0])) * full(inv_z)
    half = N_KEYS // 2
    rank_w = rank_bits[:half] | (rank_bits[half:] << 16)
    e2_w = _bf16_packed_rows(e2)
    cnt_w = cnt_bits | (cnt_bits << 16)
    w1_w = _bf16_pair_words(w1)
    for lg in range(tb // LANES):
        lanes = slice(lg * LANES, (lg + 1) * LANES)
        sel_ref[lg, k, 0] = rank_w[:, lanes]
        sel_ref[lg, k, 1] = e2_w[:, lanes]
        row_ref[lg, k, 0] = cnt_w[:, lanes]
        row_ref[lg, k, 1] = w1_w[:, lanes]


def _peer_route(hn, wqt, keys):
    n, d = hn.shape
    tb = min(ROUTE_TB, n)
    lgs = tb // LANES
    rows = jax.ShapeDtypeStruct((n // LANES, PEER_HEADS, 2, N_KEYS, LANES), jnp.int32)
    sels = jax.ShapeDtypeStruct((n // LANES, PEER_HEADS, 2, N_KEYS // 2, LANES), jnp.int32)
    rspec = pl.BlockSpec((lgs, ROUTE_HEADS, 2, N_KEYS, LANES), lambda i, h: (i, h, 0, 0, 0))
    sspec = pl.BlockSpec((lgs, ROUTE_HEADS, 2, N_KEYS // 2, LANES), lambda i, h: (i, h, 0, 0, 0))
    return pl.pallas_call(
        _route_kernel,
        grid=(n // tb, PEER_HEADS // ROUTE_HEADS),
        in_specs=[
            pl.BlockSpec((tb, d), lambda i, h: (i, 0)),
            pl.BlockSpec(wqt.shape, lambda i, h: (0, 0), pipeline_mode=pl.Buffered(1)),
            pl.BlockSpec((ROUTE_HEADS, 2, N_KEYS, PEER_HALF), lambda i, h: (h, 0, 0, 0)),
        ],
        out_specs=[sspec, rspec],
        out_shape=[sels, rows],
        scratch_shapes=[pltpu.VMEM((PEER_HEADS * PEER_QDIM, tb), BF16)],
        compiler_params=_params(("parallel", "arbitrary")),
        name="peer_route",
    )(hn, wqt, keys)


def _gelu(x):
    return 0.5 * x * (1.0 + lax.erf(x * (2.0 ** -0.5)))


def _peer_kernel(hn_ref, h_ref, u0_ref, u_ref, vt_ref, sel_ref, row_ref, fg_ref,
                 y_ref, acc_ref, hta_ref, htb_ref, w_ref):
    j = pl.program_id(1)
    nj = pl.num_programs(1)
    lane_groups, te, _ = hta_ref.shape
    rows_per_step = te // N_KEYS

    def score(u_tile_ref, dst_ref, hn=None):
        hn = hn_ref[...] if hn is None else hn
        ht = lax.dot_general(u_tile_ref[...], hn, NT_DIMS,
                             preferred_element_type=F32)
        for lg in range(lane_groups):
            dst_ref[lg] = ht[:, lg * LANES:(lg + 1) * LANES]

    def released_by(gates):
        hn = hn_ref[...]
        tb, d = hn.shape
        n_tiles, k_tiles = tb // MXU_DIM, d // MXU_DIM
        groups = MXU_DIM // BF16_ROWS
        assert len(gates) == n_tiles * k_tiles * groups
        rows = []
        for nt in range(n_tiles):
            for rg in range(groups):
                r0 = nt * MXU_DIM + rg * BF16_ROWS
                cols = []
                for kt in range(k_tiles):
                    g = gates[((nt * k_tiles + kt) * groups + rg) * RELEASE_NUM // RELEASE_DEN]
                    z = g * jnp.zeros_like(g)
                    z = jnp.concatenate([z] * (MXU_DIM // LANES), axis=1)
                    cols.append(hn[r0:r0 + BF16_ROWS, kt * MXU_DIM:(kt + 1) * MXU_DIM] + z)
                rows.append(jnp.concatenate(cols, axis=1))
        return jnp.concatenate(rows, axis=0)

    @pl.when(j == 0)
    def _():
        acc_ref[...] = jnp.zeros_like(acc_ref)

    @pl.when((pl.program_id(0) == 0) & (j == 0))
    def _():
        score(u0_ref, hta_ref)

    def step(cur_ref, nxt_ref):
        zero = jnp.zeros((BF16_ROWS, LANES), BF16)
        gates = []

        def row_tile(which, lg, hh, rr):
            words = jnp.broadcast_to(row_ref[lg, hh, which, rr:rr + 1, :], (SUBLANES, LANES))
            return pltpu.bitcast(words, BF16)

        for lg in range(lane_groups):
            lanes = slice(lg * LANES, (lg + 1) * LANES)
            for rr in range(rows_per_step):
                cnt = [row_tile(0, lg, hh, rr) for hh in range(PEER_HEADS)]
                w1 = [row_tile(1, lg, hh, rr) for hh in range(PEER_HEADS)]
                for kk in range(N_KEYS // BF16_ROWS):
                    words = slice(kk * SUBLANES, (kk + 1) * SUBLANES)
                    rows = slice(rr * N_KEYS + kk * BF16_ROWS, rr * N_KEYS + (kk + 1) * BF16_ROWS)
                    gate = zero
                    for hh in range(PEER_HEADS):
                        rank = pltpu.bitcast(sel_ref[lg, hh, 0, words, :], BF16)
                        e2 = pltpu.bitcast(sel_ref[lg, hh, 1, words, :], BF16)
                        gate = gate + jnp.where(rank < cnt[hh], e2, zero) * w1[hh]
                    gates.append(gate)
                    w_ref[rows, lanes] = gate * _gelu(cur_ref[lg, rows, :]).astype(BF16)
        score(u_ref, nxt_ref, released_by(gates))
        acc_ref[...] += jnp.dot(vt_ref[...], w_ref[...], preferred_element_type=F32)

    @pl.when(j % 2 == 0)
    def _():
        step(hta_ref, htb_ref)

    @pl.when(j % 2 == 1)
    def _():
        step(htb_ref, hta_ref)

    @pl.when(j == nj - 1)
    def _():
        h = h_ref[...] + acc_ref[...].T
        y_ref[...] = _rms(h, fg_ref[...])


def _peer_dense(hn, h, u, vt, sel, row, fg):
    n, d = hn.shape
    e = u.shape[0]
    tb, te = PEER_TB, PEER_TE
    nj = e // te
    ni = n // tb
    assert nj % 2 == 0, "score buffers alternate with the step parity across token blocks"
    lgs = tb // LANES
    rows = te // N_KEYS
    once = pl.Buffered(1)
    tspec = pl.BlockSpec((lgs, PEER_HEADS, 2, N_KEYS // 2, LANES), lambda i, j: (i, 0, 0, 0, 0))
    rspec = pl.BlockSpec((lgs, PEER_HEADS, 2, rows, LANES), lambda i, j: (i, 0, 0, j, 0))
    return pl.pallas_call(
        _peer_kernel,
        grid=(ni, nj),
        in_specs=[
            pl.BlockSpec((tb, d), lambda i, j: (jnp.minimum(i + (j + 1) // nj, ni - 1), 0)),
            pl.BlockSpec((tb, d), lambda i, j: (i, 0)),
            pl.BlockSpec((te, d), lambda i, j: (0, 0), pipeline_mode=once),
            pl.BlockSpec((te, d), lambda i, j: ((j + 1) % nj, 0)),
            pl.BlockSpec((None, d, te), lambda i, j: (j, 0, 0)),
            tspec, rspec,
            pl.BlockSpec((1, d), lambda i, j: (0, 0)),
        ],
        out_specs=pl.BlockSpec((tb, d), lambda i, j: (i, 0)),
        out_shape=jax.ShapeDtypeStruct((n, d), F32),
        scratch_shapes=[
            pltpu.VMEM((d, tb), F32),
            pltpu.VMEM((lgs, te, LANES), F32),
            pltpu.VMEM((lgs, te, LANES), F32),
            pltpu.VMEM((te, tb), BF16),
        ],
        compiler_params=_params(("arbitrary", "arbitrary")),
        name="peer_dense",
    )(hn, h, u, u, vt, sel, row, fg)


def _rope_tables(t):
    n_rows = t // GRID_W
    row = jnp.repeat(jnp.arange(n_rows, dtype=F32), GRID_W)
    col = jnp.tile(jnp.arange(GRID_W, dtype=F32), n_rows)
    freqs = ROPE_THETA ** (-jnp.arange(0, AXIS_DIM, 2, dtype=F32) / AXIS_DIM)
    ang = jnp.concatenate([row[:, None] * freqs, col[:, None] * freqs], axis=-1)
    cos, sin = jnp.cos(ang), jnp.sin(ang)
    return jnp.concatenate([cos, cos], axis=-1), jnp.concatenate([-sin, sin], axis=-1)


def _prepare(norm_mix_g, w_in, q_norm_g, k_norm_g, w_gate_f2, b_gate_f, w_gate_b2, b_gate_b,
             gla_norm_g, attn_out_norm_g, w_out, norm_ffn_g, peer_w_q, peer_sub_keys, peer_u,
             peer_v, final_norm_g):
    perm = np.concatenate([np.arange(0, HEAD_DIM, 2), np.arange(1, HEAD_DIM, 2)])
    kw = GLA_HEADS * GLA_DK
    o_k = ATTN_WIDTH
    o_v = o_k + KV_WIDTH
    o_g = o_v + KV_WIDTH
    o_lr = o_g + 2 * kw + 2 * GLA_WIDTH
    wq = w_in[:, :o_k].reshape(D_MODEL, N_HEADS, HEAD_DIM)[:, :, perm].reshape(D_MODEL, o_k)
    wk = w_in[:, o_k:o_v].reshape(D_MODEL, N_KV_HEADS, HEAD_DIM)[:, :, perm].reshape(D_MODEL, KV_WIDTH)
    w_qkv = jnp.concatenate([wq, wk, w_in[:, o_v:o_g]], axis=1).astype(BF16)
    w_lr = jnp.pad(w_in[:, o_lr:], ((0, 0), (0, LANES - 2 * GATE_RANK)))
    w_gla = jnp.concatenate([w_in[:, o_g:o_lr], w_lr], axis=1).astype(BF16)
    wg = jnp.zeros((LANES, 2 * kw), F32)
    wg = wg.at[:GATE_RANK, :kw].set(w_gate_f2).at[GATE_RANK:2 * GATE_RANK, kw:].set(w_gate_b2)
    bg = jnp.concatenate([b_gate_f, b_gate_b])[None, :]

    keys = jnp.stack([peer_sub_keys[:, 0], peer_sub_keys[:, 1][:, PEER_KEY2_ROWS]], axis=1)
    return dict(
        norm_mix_g=norm_mix_g[None, :], w_qkv=w_qkv, w_gla=w_gla, wg=wg, bg=bg,
        qg=q_norm_g[perm][None, :], kg=k_norm_g[perm][None, :],
        gla_gain=gla_norm_g[None, :], attn_gain=attn_out_norm_g[None, :],
        wo_a=w_out[:ATTN_WIDTH].astype(BF16), wo_g=w_out[ATTN_WIDTH:].astype(BF16),
        ffn_g=norm_ffn_g[None, :], wqt=peer_w_q.T.astype(BF16),
        keys=keys.astype(BF16), u=peer_u.astype(BF16),
        vt=peer_v.astype(BF16).reshape(-1, PEER_TE, D_MODEL).transpose(0, 2, 1),
        final_g=final_norm_g[None, :],
    )


def _layer(x, p):
    b, t, d = x.shape
    n = b * t
    x2 = x.reshape(n, d)
    cos, sin = _rope_tables(t)
    q, k, v = _qkv_proj(x2, p["norm_mix_g"], p["w_qkv"], p["qg"], p["kg"], cos, sin, t)
    gq, gk, gv, sg, laf, lab = _gla_proj(x2, p["norm_mix_g"], p["w_gla"], p["wg"], p["bg"])
    oa = _attention(q.reshape(b, t, -1), k.reshape(b, t, -1), v.reshape(b, t, -1))
    r3 = lambda a: a.reshape(b, t, -1)
    og = _gla(r3(gq), r3(gk), r3(gv), r3(sg), r3(laf), r3(lab), p["gla_gain"])
    h, hn = _out_proj(x2, oa.reshape(n, -1), og.reshape(n, -1), p["attn_gain"],
                      p["wo_a"], p["wo_g"], p["ffn_g"])
    sel, row = _peer_route(hn, p["wqt"], p["keys"])
    y = _peer_dense(hn, h, p["u"], p["vt"], sel, row, p["final_g"])
    return y.reshape(b, t, d)


def kernel(x_prompt, x_sample, norm_mix_g, w_in, q_norm_g, k_norm_g, w_gate_f2, b_gate_f,
           w_gate_b2, b_gate_b, gla_norm_g, attn_out_norm_g, w_out, norm_ffn_g, peer_w_q,
           peer_sub_keys, peer_u, peer_v, final_norm_g):
    p = _prepare(norm_mix_g, w_in, q_norm_g, k_norm_g, w_gate_f2, b_gate_f, w_gate_b2,
                 b_gate_b, gla_norm_g, attn_out_norm_g, w_out, norm_ffn_g, peer_w_q,
                 peer_sub_keys, peer_u, peer_v, final_norm_g)
    return (_layer(x_prompt, p), _layer(x_sample, p))
```

```python
import functools
import math

import jax
import jax.numpy as jnp
import numpy as np
from jax import lax
from jax.experimental import pallas as pl
from jax.experimental.pallas import tpu as pltpu

F32 = jnp.float32
BF16 = jnp.bfloat16

D_MODEL = 2048
GRID_W = 64
HEAD_DIM = 128
N_HEADS = 8
N_KV_HEADS = 2
GQA_GROUP = N_HEADS // N_KV_HEADS
ATTN_WIDTH = N_HEADS * HEAD_DIM
KV_WIDTH = N_KV_HEADS * HEAD_DIM
ROPE_THETA = 10000.0
AXIS_DIM = HEAD_DIM // 2
GLA_HEADS = 4
GLA_DK = 128
GLA_DV = 256
GLA_WIDTH = GLA_HEADS * GLA_DV
GATE_RANK = 16
GATE_NORMALIZER = 16.0
PEER_HEADS = 8
N_KEYS = 128
PEER_TOPK = 16
PEER_HALF = 128
PEER_QDIM = 2 * PEER_HALF
EPS = 1e-6
LOG2_E = math.log2(math.e)

LANES = 128
SUBLANES = 8
BF16_ROWS = 16
MXU_DIM = 256
VMEM_LIMIT = 58 * 1024 * 1024

PROJ_TM = 512
PROJ_COLS = 512
ATTN_TQ = 256
ATTN_AHEAD = 2
GLA_CHUNK = 64
GLA_UNROLL = 16
ROUTE_TB = 512
ROUTE_HEADS = 2
PEER_TB = 512
RELEASE_NUM, RELEASE_DEN = 1, 2
PEER_TE = 1024

PEER_KEY2_ROWS = np.concatenate([np.arange(0, N_KEYS, 2), np.arange(1, N_KEYS, 2)])

NT_DIMS = (((1,), (1,)), ((), ()))
TN_DIMS = (((0,), (0,)), ((), ()))


def _params(sem):
    return pltpu.CompilerParams(dimension_semantics=sem, vmem_limit_bytes=VMEM_LIMIT)


def _scaled_rows(x, g):
    r = lax.rsqrt(jnp.mean(x * x, axis=-1, keepdims=True) + EPS)
    return (x * g).astype(BF16), r


def _rms(x, g):
    return x * lax.rsqrt(jnp.mean(x * x, axis=-1, keepdims=True) + EPS) * g


def _qkv_kernel(x_ref, g_ref, w_ref, qg_ref, kg_ref, cos_ref, sin_ref, q_ref, k_ref, v_ref):
    xg, r = _scaled_rows(x_ref[...], g_ref[...])
    z = jnp.dot(xg, w_ref[...], preferred_element_type=F32)
    c = cos_ref[...]
    s = sin_ref[...]

    def norm_rope(hd, gain, scale):
        hn = _rms(hd * r, gain)
        rot = pltpu.roll(hn, HEAD_DIM // 2, axis=1)
        return ((hn * c + rot * s) * scale).astype(BF16)

    for hh in range(N_HEADS):
        sl = slice(hh * HEAD_DIM, (hh + 1) * HEAD_DIM)
        q_ref[:, sl] = norm_rope(z[:, sl], qg_ref[...], HEAD_DIM ** -0.5 * LOG2_E)
    for hh in range(N_KV_HEADS):
        sl = slice(hh * HEAD_DIM, (hh + 1) * HEAD_DIM)
        zsl = slice(ATTN_WIDTH + hh * HEAD_DIM, ATTN_WIDTH + (hh + 1) * HEAD_DIM)
        k_ref[:, sl] = norm_rope(z[:, zsl], kg_ref[...], 1.0)
    v_ref[...] = (z[:, ATTN_WIDTH + KV_WIDTH:] * r).astype(BF16)


def _qkv_proj(x2, g, w, qg, kg, cos, sin, seq):
    n, d = x2.shape
    tm = min(PROJ_TM, seq)
    nt = seq // tm
    wq = w.shape[1]
    full = lambda i: (0, 0)
    return pl.pallas_call(
        _qkv_kernel,
        grid=(n // tm,),
        in_specs=[
            pl.BlockSpec((tm, d), lambda i: (i, 0)),
            pl.BlockSpec((1, d), full),
            pl.BlockSpec((d, wq), full),
            pl.BlockSpec((1, HEAD_DIM), full),
            pl.BlockSpec((1, HEAD_DIM), full),
            pl.BlockSpec((tm, HEAD_DIM), lambda i: (i % nt, 0)),
            pl.BlockSpec((tm, HEAD_DIM), lambda i: (i % nt, 0)),
        ],
        out_specs=[
            pl.BlockSpec((tm, ATTN_WIDTH), lambda i: (i, 0)),
            pl.BlockSpec((tm, KV_WIDTH), lambda i: (i, 0)),
            pl.BlockSpec((tm, KV_WIDTH), lambda i: (i, 0)),
        ],
        out_shape=[
            jax.ShapeDtypeStruct((n, ATTN_WIDTH), BF16),
            jax.ShapeDtypeStruct((n, KV_WIDTH), BF16),
            jax.ShapeDtypeStruct((n, KV_WIDTH), BF16),
        ],
        compiler_params=_params(("parallel",)),
        name="qkv_proj",
    )(x2, g, w, qg, kg, cos, sin)


def _dot_bf16x3(a, b):
    a_hi = a.astype(BF16)
    b_hi = b.astype(BF16)
    a_lo = (a - a_hi.astype(F32)).astype(BF16)
    b_lo = (b - b_hi.astype(F32)).astype(BF16)
    dot = functools.partial(jnp.dot, preferred_element_type=F32)
    return dot(a_hi, b_hi) + dot(a_lo, b_hi) + dot(a_hi, b_lo)


def _log_sigmoid(x):
    return jnp.minimum(x, 0.0) - jnp.log(1.0 + jnp.exp(-jnp.abs(x)))


def _gla_proj_kernel(x_ref, g_ref, w_ref, wg_ref, bg_ref,
                     gq_ref, gk_ref, gv_ref, sg_ref, laf_ref, lab_ref):
    xn, r = _scaled_rows(x_ref[...], g_ref[...])
    kw = GLA_HEADS * GLA_DK
    assert kw == PROJ_COLS and GLA_WIDTH == 2 * PROJ_COLS

    def project(g):
        cols = slice(g * PROJ_COLS, (g + 1) * PROJ_COLS)
        return jnp.dot(xn, w_ref[:, cols], preferred_element_type=F32)

    def silu(x):
        return x * (1.0 / (1.0 + jnp.exp(-x)))

    outputs = [
        (gq_ref, 0, lambda z: z * (GLA_DK ** -0.5)),
        (gk_ref, 0, lambda z: z),
        (gv_ref, 0, lambda z: z),
        (gv_ref, PROJ_COLS, lambda z: z),
        (sg_ref, 0, silu),
        (sg_ref, PROJ_COLS, silu),
    ]
    lr = jnp.dot(xn, w_ref[:, len(outputs) * PROJ_COLS:], preferred_element_type=F32)
    pre = _dot_bf16x3(lr * r, wg_ref[...]) + bg_ref[...]
    z = project(0)
    la = _log_sigmoid(pre) * (1.0 / GATE_NORMALIZER)
    laf_ref[...] = la[:, :kw]
    lab_ref[...] = la[:, kw:]
    for g, (ref, col, fn) in enumerate(outputs):
        z_next = project(g + 1) if g + 1 < len(outputs) else None
        ref[:, col:col + PROJ_COLS] = fn(z * r).astype(BF16)
        z = z_next


def _gla_proj(x2, g, w, wg, bg):
    n, d = x2.shape
    tm = PROJ_TM
    kw = GLA_HEADS * GLA_DK
    full = lambda i: (0, 0)
    row = lambda i: (i, 0)
    return pl.pallas_call(
        _gla_proj_kernel,
        grid=(n // tm,),
        in_specs=[
            pl.BlockSpec((tm, d), row),
            pl.BlockSpec((1, d), full),
            pl.BlockSpec(w.shape, full),
            pl.BlockSpec(wg.shape, full),
            pl.BlockSpec(bg.shape, full),
        ],
        out_specs=[
            pl.BlockSpec((tm, kw), row),
            pl.BlockSpec((tm, kw), row),
            pl.BlockSpec((tm, GLA_WIDTH), row),
            pl.BlockSpec((tm, GLA_WIDTH), row),
            pl.BlockSpec((tm, kw), row),
            pl.BlockSpec((tm, kw), row),
        ],
        out_shape=[
            jax.ShapeDtypeStruct((n, kw), BF16),
            jax.ShapeDtypeStruct((n, kw), BF16),
            jax.ShapeDtypeStruct((n, GLA_WIDTH), BF16),
            jax.ShapeDtypeStruct((n, GLA_WIDTH), BF16),
            jax.ShapeDtypeStruct((n, kw), F32),
            jax.ShapeDtypeStruct((n, kw), F32),
        ],
        compiler_params=_params(("parallel",)),
        name="gla_proj",
    )(x2, g, w, wg, bg)


def _attn_kernel(q_ref, k_ref, v_ref, o_ref):
    k = k_ref[0]
    v = v_ref[0]

    def scores(hh):
        sl = slice(hh * HEAD_DIM, (hh + 1) * HEAD_DIM)
        return lax.dot_general(q_ref[0, :, sl], k, NT_DIMS, preferred_element_type=F32)

    ahead = [scores(hh) for hh in range(ATTN_AHEAD)]
    for hh in range(GQA_GROUP):
        if hh + ATTN_AHEAD < GQA_GROUP:
            ahead.append(scores(hh + ATTN_AHEAD))
        s = ahead[hh]
        m = jnp.max(s, axis=-1, keepdims=True)
        p = jnp.exp2(s - m)
        l = jnp.sum(p, axis=-1, keepdims=True)
        o = jnp.dot(p.astype(BF16), v, preferred_element_type=F32)
        o_ref[0, :, hh * HEAD_DIM:(hh + 1) * HEAD_DIM] = (o / l).astype(BF16)


def _attention(q, k, v):
    b, t, _ = q.shape
    tq = min(ATTN_TQ, t)
    gw = GQA_GROUP * HEAD_DIM
    return pl.pallas_call(
        _attn_kernel,
        grid=(b, N_KV_HEADS, t // tq),
        in_specs=[
            pl.BlockSpec((1, tq, gw), lambda bi, g, qi: (bi, qi, g)),
            pl.BlockSpec((1, t, HEAD_DIM), lambda bi, g, qi: (bi, 0, g)),
            pl.BlockSpec((1, t, HEAD_DIM), lambda bi, g, qi: (bi, 0, g)),
        ],
        out_specs=pl.BlockSpec((1, tq, gw), lambda bi, g, qi: (bi, qi, g)),
        out_shape=jax.ShapeDtypeStruct((b, t, ATTN_WIDTH), BF16),
        compiler_params=_params(("parallel", "parallel", "arbitrary")),
        name="gqa_attention",
    )(q, k, v)


def _split_dot(tri, g):
    hi = g.astype(BF16)
    lo = (g - hi.astype(F32)).astype(BF16)
    return (jnp.dot(tri, hi, preferred_element_type=F32)
            + jnp.dot(tri, lo, preferred_element_type=F32))


def _gla_kernel(q_ref, k_ref, v_ref, sg_ref, gf_ref, gb_ref, gain_ref, o_ref, acc_ref,
                sf_ref, sb_ref):
    t = q_ref.shape[1]
    c = GLA_CHUNK
    n = t // c
    mid = c // 2
    ri = lax.broadcasted_iota(jnp.int32, (c, c), 0)
    ci = lax.broadcasted_iota(jnp.int32, (c, c), 1)
    lower = ri >= ci
    tri_lo = lower.astype(BF16)
    tri_up = (ri <= ci).astype(BF16)
    sf_ref[...] = jnp.zeros_like(sf_ref)
    sb_ref[...] = jnp.zeros_like(sb_ref)

    upper = ri <= ci
    un = min(GLA_UNROLL, n // 2)

    def finish(rows, o):
        y = _rms(o, gain_ref[...]) * sg_ref[0, rows, :].astype(F32)
        o_ref[0, rows, :] = y.astype(BF16)

    def trip(i, second_half):
        items = []
        for u in range(un):
            idx = i * un + u
            items.append((idx, sf_ref, gf_ref, tri_lo, c - 1, lower))
            items.append((n - 1 - idx, sb_ref, gb_ref, tri_up, 0, upper))
        rows = [pl.ds(pl.multiple_of(it[0] * c, c), c) for it in items]
        bs = [_split_dot(it[3], it[2][0, r, :]) for it, r in zip(items, rows)]
        qs, ks, scores = [], [], []
        for it, r, b in zip(items, rows, bs):
            q = q_ref[0, r, :].astype(F32)
            k = k_ref[0, r, :].astype(F32)
            ref = b[mid:mid + 1, :]
            qt = (q * jnp.exp(b - ref)).astype(BF16)
            kt = (k * jnp.exp(ref - b)).astype(BF16)
            a = lax.dot_general(qt, kt, NT_DIMS, preferred_element_type=F32)
            qs.append(q)
            ks.append(k)
            scores.append(jnp.where(it[5], a, 0.0).astype(BF16))
        intra, updates, tots = [], [], []
        for it, r, b, k, a in zip(items, rows, bs, ks, scores):
            v = v_ref[0, r, :]
            tot = b[it[4]:it[4] + 1, :]
            kd = (k * jnp.exp(tot - b)).astype(BF16)
            intra.append(jnp.dot(a, v, preferred_element_type=F32))
            updates.append(lax.dot_general(v, kd, TN_DIMS, preferred_element_type=F32))
            tots.append(tot)
        for it, r, b, q, o_in, upd, tot in zip(items, rows, bs, qs, intra, updates, tots):
            s_ref = it[1]
            st = s_ref[...]
            o = o_in + lax.dot_general((q * jnp.exp(b)).astype(BF16), st.astype(BF16), NT_DIMS,
                                       preferred_element_type=F32)
            s_ref[...] = st * jnp.exp(tot) + upd
            if second_half:
                finish(r, acc_ref[r, :] + o)
            else:
                acc_ref[r, :] = o

    half_trips = n // 2 // un

    def first_half(i, carry):
        trip(i, False)
        return carry

    def second_half(i, carry):
        trip(i, True)
        return carry

    lax.fori_loop(0, half_trips, first_half, 0)
    lax.fori_loop(half_trips, 2 * half_trips, second_half, 0)


def _gla(gq, gk, gv, sg, laf, lab, gain):
    b, t, _ = gq.shape
    kspec = pl.BlockSpec((1, t, GLA_DK), lambda bi, h: (bi, 0, h))
    vspec = pl.BlockSpec((1, t, GLA_DV), lambda bi, h: (bi, 0, h))
    return pl.pallas_call(
        _gla_kernel,
        grid=(b, GLA_HEADS),
        in_specs=[kspec, kspec, vspec, vspec, kspec, kspec,
                  pl.BlockSpec((1, GLA_DV), lambda bi, h: (0, 0))],
        out_specs=vspec,
        out_shape=jax.ShapeDtypeStruct((b, t, GLA_WIDTH), BF16),
        scratch_shapes=[
            pltpu.VMEM((t, GLA_DV), F32),
            pltpu.VMEM((GLA_DV, GLA_DK), F32),
            pltpu.VMEM((GLA_DV, GLA_DK), F32),
        ],
        compiler_params=_params(("parallel", "parallel")),
        name="gla_bidir",
    )(gq, gk, gv, sg, laf, lab, gain)


def _out_proj_kernel(x_ref, oa_ref, og_ref, ag_ref, wa_ref, wg_ref, fg_ref, h_ref, hn_ref):
    hg = jnp.dot(og_ref[...], wg_ref[...], preferred_element_type=F32)
    oa = _rms(oa_ref[...].astype(F32), ag_ref[...]).astype(BF16)
    h = x_ref[...] + jnp.dot(oa, wa_ref[...], preferred_element_type=F32) + hg
    h_ref[...] = h
    hn_ref[...] = _rms(h, fg_ref[...]).astype(BF16)


def _out_proj(x2, oa, og, ag, wa, wg, fg):
    n, d = x2.shape
    tm = PROJ_TM
    full = lambda i: (0, 0)
    row = lambda i: (i, 0)
    return pl.pallas_call(
        _out_proj_kernel,
        grid=(n // tm,),
        in_specs=[
            pl.BlockSpec((tm, d), row),
            pl.BlockSpec((tm, ATTN_WIDTH), row),
            pl.BlockSpec((tm, GLA_WIDTH), row),
            pl.BlockSpec((1, ATTN_WIDTH), full),
            pl.BlockSpec(wa.shape, full),
            pl.BlockSpec(wg.shape, full),
            pl.BlockSpec((1, d), full),
        ],
        out_specs=[pl.BlockSpec((tm, d), row), pl.BlockSpec((tm, d), row)],
        out_shape=[jax.ShapeDtypeStruct((n, d), F32), jax.ShapeDtypeStruct((n, d), BF16)],
        compiler_params=_params(("parallel",)),
        name="out_proj",
    )(x2, oa, og, ag, wa, wg, fg)


def _cmp_exchange(rows, i, j):
    a, b = rows[i], rows[j]
    if b is None:
        return
    if a is None:
        rows[i], rows[j] = b, None
        return
    rows[i], rows[j] = jnp.maximum(a, b), jnp.minimum(a, b)


def _bitonic_merge_desc(rows):
    n = len(rows)
    j = n // 2
    while j >= 1:
        for i in range(n):
            l = i ^ j
            if l > i:
                _cmp_exchange(rows, i, l)
        j //= 2


def _bitonic_sort_desc(rows):
    n = len(rows)
    k = 2
    while k <= n:
        j = k // 2
        while j >= 1:
            for i in range(n):
                l = i ^ j
                if l > i:
                    if (i & k) == 0:
                        _cmp_exchange(rows, i, l)
                    else:
                        _cmp_exchange(rows, l, i)
            j //= 2
        k *= 2


def _top_of_union(a, b):
    n = len(a)
    out = []
    for i in range(n):
        x, y = a[i], b[n - 1 - i]
        out.append(x if y is None else (y if x is None else jnp.maximum(x, y)))
    _bitonic_merge_desc(out)
    return out


def _top16_rows(s):
    l = s.shape[1]
    s3 = s.reshape(N_KEYS // SUBLANES, SUBLANES, l)
    rows = [s3[i] for i in range(N_KEYS // SUBLANES)]
    _bitonic_sort_desc(rows)
    for shift in (4, 2, 1):
        other = [pltpu.roll(r, shift, axis=0) for r in rows]
        rows = _top_of_union(rows, other)
    return rows


_CANDIDATES = [(i, j) for i in range(PEER_TOPK) for j in range(PEER_TOPK)
               if (i + 1) * (j + 1) <= PEER_TOPK]


def _bf16_bits(x):
    return lax.bitcast_convert_type(x.astype(BF16).astype(F32), jnp.uint32) >> 16


def _bf16_pair_words(x):
    hi = _bf16_bits(x)
    return lax.bitcast_convert_type((hi << 16) | hi, jnp.int32)


def _bf16_packed_rows(x):
    hi = _bf16_bits(x)
    half = x.shape[0] // 2
    return lax.bitcast_convert_type(hi[:half] | (hi[half:] << 16), jnp.int32)


def _route_kernel(hn_ref, wqt_ref, keys_ref, sel_ref, row_ref, qt_ref):
    step = pl.program_id(1)

    @pl.when(step == 0)
    def _():
        qt_ref[...] = lax.dot_general(wqt_ref[...], hn_ref[...], NT_DIMS,
                                      preferred_element_type=F32).astype(BF16)

    for k in range(ROUTE_HEADS):
        _route_head(step * ROUTE_HEADS + k, k, keys_ref, sel_ref, row_ref, qt_ref)


def _route_head(hh, k, keys_ref, sel_ref, row_ref, qt_ref):
    base = pl.multiple_of(hh * PEER_QDIM, PEER_QDIM)
    q1 = qt_ref[pl.ds(base, PEER_HALF), :]
    q2 = qt_ref[pl.ds(base + PEER_HALF, PEER_HALF), :]
    s1 = jnp.dot(keys_ref[k, 0], q1, preferred_element_type=F32)
    s2 = jnp.dot(keys_ref[k, 1], q2, preferred_element_type=F32)
    tb = s1.shape[1]
    lgs = tb // LANES
    a = _top16_rows(s1)
    b = _top16_rows(s2)

    sub = lax.broadcasted_iota(jnp.int32, (SUBLANES, LANES), 0)

    def compact(r):
        out = r[:, :LANES]
        for g in range(1, lgs):
            out = jnp.where(sub == g, r[:, g * LANES:(g + 1) * LANES], out)
        return out

    def expand(c):
        return jnp.concatenate(
            [jnp.broadcast_to(c[g:g + 1, :], (SUBLANES, LANES)) for g in range(lgs)], axis=1)

    ac = [compact(r) for r in a]
    bc = [compact(r) for r in b]
    cand = [ac[i] + bc[j] for (i, j) in _CANDIDATES]
    cand += [None] * (64 - len(cand))
    groups = []
    for gi in range(4):
        grp = cand[gi * 16:(gi + 1) * 16]
        _bitonic_sort_desc(grp)
        groups.append(grp)
    top = _top_of_union(_top_of_union(groups[0], groups[1]),
                        _top_of_union(groups[2], groups[3]))
    z = jnp.zeros_like(top[0])
    for tk in top:
        z = z + jnp.exp(tk - top[0])
    tau = expand(top[PEER_TOPK - 1])
    inv_z = expand(1.0 / z)
    full = lambda r: jnp.broadcast_to(r[0:1, :], (N_KEYS, tb))
    tau_f = full(tau)
    cnt_bits = jnp.zeros((N_KEYS, tb), jnp.int32)
    rank_bits = jnp.zeros((N_KEYS, tb), jnp.int32)
    for j in range(PEER_TOPK):
        bj = full(b[j])
        bits = int(np.asarray(j + 1, dtype=np.float32).view(np.uint32)) >> 16
        cnt_bits = jnp.where(s1 + bj >= tau_f, bits, cnt_bits)
        rank_bits = jnp.where(bj > s2, bits, rank_bits)
    e2 = jnp.exp(s2 - full(b[0]))
    w1 = jnp.exp(s1 - full(a[0])) * full(inv_z)
    half = N_KEYS // 2
    rank_w = rank_bits[:half] | (rank_bits[half:] << 16)
    e2_w = _bf16_packed_rows(e2)
    cnt_w = cnt_bits | (cnt_bits << 16)
    w1_w = _bf16_pair_words(w1)
    for lg in range(tb // LANES):
        lanes = slice(lg * LANES, (lg + 1) * LANES)
        sel_ref[lg, k, 0] = rank_w[:, lanes]
        sel_ref[lg, k, 1] = e2_w[:, lanes]
        row_ref[lg, k, 0] = cnt_w[:, lanes]
        row_ref[lg, k, 1] = w1_w[:, lanes]


def _peer_route(hn, wqt, keys):
    n, d = hn.shape
    tb = min(ROUTE_TB, n)
    lgs = tb // LANES
    rows = jax.ShapeDtypeStruct((n // LANES, PEER_HEADS, 2, N_KEYS, LANES), jnp.int32)
    sels = jax.ShapeDtypeStruct((n // LANES, PEER_HEADS, 2, N_KEYS // 2, LANES), jnp.int32)
    rspec = pl.BlockSpec((lgs, ROUTE_HEADS, 2, N_KEYS, LANES), lambda i, h: (i, h, 0, 0, 0))
    sspec = pl.BlockSpec((lgs, ROUTE_HEADS, 2, N_KEYS // 2, LANES), lambda i, h: (i, h, 0, 0, 0))
    return pl.pallas_call(
        _route_kernel,
        grid=(n // tb, PEER_HEADS // ROUTE_HEADS),
        in_specs=[
            pl.BlockSpec((tb, d), lambda i, h: (i, 0)),
            pl.BlockSpec(wqt.shape, lambda i, h: (0, 0), pipeline_mode=pl.Buffered(1)),
            pl.BlockSpec((ROUTE_HEADS, 2, N_KEYS, PEER_HALF), lambda i, h: (h, 0, 0, 0)),
        ],
        out_specs=[sspec, rspec],
        out_shape=[sels, rows],
        scratch_shapes=[pltpu.VMEM((PEER_HEADS * PEER_QDIM, tb), BF16)],
        compiler_params=_params(("parallel", "arbitrary")),
        name="peer_route",
    )(hn, wqt, keys)


def _gelu(x):
    return 0.5 * x * (1.0 + lax.erf(x * (2.0 ** -0.5)))


def _peer_kernel(hn_ref, h_ref, u0_ref, u_ref, vt_ref, sel_ref, row_ref, fg_ref,
                 y_ref, acc_ref, hta_ref, htb_ref, w_ref):
    j = pl.program_id(1)
    nj = pl.num_programs(1)
    lane_groups, te, _ = hta_ref.shape
    rows_per_step = te // N_KEYS

    def score(u_tile_ref, dst_ref, hn=None):
        hn = hn_ref[...] if hn is None else hn
        ht = lax.dot_general(u_tile_ref[...], hn, NT_DIMS,
                             preferred_element_type=F32)
        for lg in range(lane_groups):
            dst_ref[lg] = ht[:, lg * LANES:(lg + 1) * LANES]

    def released_by(gates):
        hn = hn_ref[...]
        tb, d = hn.shape
        n_tiles, k_tiles = tb // MXU_DIM, d // MXU_DIM
        groups = MXU_DIM // BF16_ROWS
        assert len(gates) == n_tiles * k_tiles * groups
        rows = []
        for nt in range(n_tiles):
            for rg in range(groups):
                r0 = nt * MXU_DIM + rg * BF16_ROWS
                cols = []
                for kt in range(k_tiles):
                    g = gates[((nt * k_tiles + kt) * groups + rg) * RELEASE_NUM // RELEASE_DEN]
                    z = g * jnp.zeros_like(g)
                    z = jnp.concatenate([z] * (MXU_DIM // LANES), axis=1)
                    cols.append(hn[r0:r0 + BF16_ROWS, kt * MXU_DIM:(kt + 1) * MXU_DIM] + z)
                rows.append(jnp.concatenate(cols, axis=1))
        return jnp.concatenate(rows, axis=0)

    @pl.when(j == 0)
    def _():
        acc_ref[...] = jnp.zeros_like(acc_ref)

    @pl.when((pl.program_id(0) == 0) & (j == 0))
    def _():
        score(u0_ref, hta_ref)

    def step(cur_ref, nxt_ref):
        zero = jnp.zeros((BF16_ROWS, LANES), BF16)
        gates = []

        def row_tile(which, lg, hh, rr):
            words = jnp.broadcast_to(row_ref[lg, hh, which, rr:rr + 1, :], (SUBLANES, LANES))
            return pltpu.bitcast(words, BF16)

        for lg in range(lane_groups):
            lanes = slice(lg * LANES, (lg + 1) * LANES)
            for rr in range(rows_per_step):
                cnt = [row_tile(0, lg, hh, rr) for hh in range(PEER_HEADS)]
                w1 = [row_tile(1, lg, hh, rr) for hh in range(PEER_HEADS)]
                for kk in range(N_KEYS // BF16_ROWS):
                    words = slice(kk * SUBLANES, (kk + 1) * SUBLANES)
                    rows = slice(rr * N_KEYS + kk * BF16_ROWS, rr * N_KEYS + (kk + 1) * BF16_ROWS)
                    gate = zero
                    for hh in range(PEER_HEADS):
                        rank = pltpu.bitcast(sel_ref[lg, hh, 0, words, :], BF16)
                        e2 = pltpu.bitcast(sel_ref[lg, hh, 1, words, :], BF16)
                        gate = gate + jnp.where(rank < cnt[hh], e2, zero) * w1[hh]
                    gates.append(gate)
                    w_ref[rows, lanes] = gate * _gelu(cur_ref[lg, rows, :]).astype(BF16)
        score(u_ref, nxt_ref, released_by(gates))
        acc_ref[...] += jnp.dot(vt_ref[...], w_ref[...], preferred_element_type=F32)

    @pl.when(j % 2 == 0)
    def _():
        step(hta_ref, htb_ref)

    @pl.when(j % 2 == 1)
    def _():
        step(htb_ref, hta_ref)

    @pl.when(j == nj - 1)
    def _():
        h = h_ref[...] + acc_ref[...].T
        y_ref[...] = _rms(h, fg_ref[...])


def _peer_dense(hn, h, u, vt, sel, row, fg):
    n, d = hn.shape
    e = u.shape[0]
    tb, te = PEER_TB, PEER_TE
    nj = e // te
    ni = n // tb
    assert nj % 2 == 0, "score buffers alternate with the step parity across token blocks"
    lgs = tb // LANES
    rows = te // N_KEYS
    once = pl.Buffered(1)
    tspec = pl.BlockSpec((lgs, PEER_HEADS, 2, N_KEYS // 2, LANES), lambda i, j: (i, 0, 0, 0, 0))
    rspec = pl.BlockSpec((lgs, PEER_HEADS, 2, rows, LANES), lambda i, j: (i, 0, 0, j, 0))
    return pl.pallas_call(
        _peer_kernel,
        grid=(ni, nj),
        in_specs=[
            pl.BlockSpec((tb, d), lambda i, j: (jnp.minimum(i + (j + 1) // nj, ni - 1), 0)),
            pl.BlockSpec((tb, d), lambda i, j: (i, 0)),
            pl.BlockSpec((te, d), lambda i, j: (0, 0), pipeline_mode=once),
            pl.BlockSpec((te, d), lambda i, j: ((j + 1) % nj, 0)),
            pl.BlockSpec((None, d, te), lambda i, j: (j, 0, 0)),
            tspec, rspec,
            pl.BlockSpec((1, d), lambda i, j: (0, 0)),
        ],
        out_specs=pl.BlockSpec((tb, d), lambda i, j: (i, 0)),
        out_shape=jax.ShapeDtypeStruct((n, d), F32),
        scratch_shapes=[
            pltpu.VMEM((d, tb), F32),
            pltpu.VMEM((lgs, te, LANES), F32),
            pltpu.VMEM((lgs, te, LANES), F32),
            pltpu.VMEM((te, tb), BF16),
        ],
        compiler_params=_params(("arbitrary", "arbitrary")),
        name="peer_dense",
    )(hn, h, u, u, vt, sel, row, fg)


def _rope_tables(t):
    n_rows = t // GRID_W
    row = jnp.repeat(jnp.arange(n_rows, dtype=F32), GRID_W)
    col = jnp.tile(jnp.arange(GRID_W, dtype=F32), n_rows)
    freqs = ROPE_THETA ** (-jnp.arange(0, AXIS_DIM, 2, dtype=F32) / AXIS_DIM)
    ang = jnp.concatenate([row[:, None] * freqs, col[:, None] * freqs], axis=-1)
    cos, sin = jnp.cos(ang), jnp.sin(ang)
    return jnp.concatenate([cos, cos], axis=-1), jnp.concatenate([-sin, sin], axis=-1)


def _prepare(norm_mix_g, w_in, q_norm_g, k_norm_g, w_gate_f2, b_gate_f, w_gate_b2, b_gate_b,
             gla_norm_g, attn_out_norm_g, w_out, norm_ffn_g, peer_w_q, peer_sub_keys, peer_u,
             peer_v, final_norm_g):
    perm = np.concatenate([np.arange(0, HEAD_DIM, 2), np.arange(1, HEAD_DIM, 2)])
    kw = GLA_HEADS * GLA_DK
    o_k = ATTN_WIDTH
    o_v = o_k + KV_WIDTH
    o_g = o_v + KV_WIDTH
    o_lr = o_g + 2 * kw + 2 * GLA_WIDTH
    wq = w_in[:, :o_k].reshape(D_MODEL, N_HEADS, HEAD_DIM)[:, :, perm].reshape(D_MODEL, o_k)
    wk = w_in[:, o_k:o_v].reshape(D_MODEL, N_KV_HEADS, HEAD_DIM)[:, :, perm].reshape(D_MODEL, KV_WIDTH)
    w_qkv = jnp.concatenate([wq, wk, w_in[:, o_v:o_g]], axis=1).astype(BF16)
    w_lr = jnp.pad(w_in[:, o_lr:], ((0, 0), (0, LANES - 2 * GATE_RANK)))
    w_gla = jnp.concatenate([w_in[:, o_g:o_lr], w_lr], axis=1).astype(BF16)
    wg = jnp.zeros((LANES, 2 * kw), F32)
    wg = wg.at[:GATE_RANK, :kw].set(w_gate_f2).at[GATE_RANK:2 * GATE_RANK, kw:].set(w_gate_b2)
    bg = jnp.concatenate([b_gate_f, b_gate_b])[None, :]

    keys = jnp.stack([peer_sub_keys[:, 0], peer_sub_keys[:, 1][:, PEER_KEY2_ROWS]], axis=1)
    return dict(
        norm_mix_g=norm_mix_g[None, :], w_qkv=w_qkv, w_gla=w_gla, wg=wg, bg=bg,
        qg=q_norm_g[perm][None, :], kg=k_norm_g[perm][None, :],
        gla_gain=gla_norm_g[None, :], attn_gain=attn_out_norm_g[None, :],
        wo_a=w_out[:ATTN_WIDTH].astype(BF16), wo_g=w_out[ATTN_WIDTH:].astype(BF16),
        ffn_g=norm_ffn_g[None, :], wqt=peer_w_q.T.astype(BF16),
        keys=keys.astype(BF16), u=peer_u.astype(BF16),
        vt=peer_v.astype(BF16).reshape(-1, PEER_TE, D_MODEL).transpose(0, 2, 1),
        final_g=final_norm_g[None, :],
    )


def _layer(x, p):
    b, t, d = x.shape
    n = b * t
    x2 = x.reshape(n, d)
    cos, sin = _rope_tables(t)
    q, k, v = _qkv_proj(x2, p["norm_mix_g"], p["w_qkv"], p["qg"], p["kg"], cos, sin, t)
    gq, gk, gv, sg, laf, lab = _gla_proj(x2, p["norm_mix_g"], p["w_gla"], p["wg"], p["bg"])
    oa = _attention(q.reshape(b, t, -1), k.reshape(b, t, -1), v.reshape(b, t, -1))
    r3 = lambda a: a.reshape(b, t, -1)
    og = _gla(r3(gq), r3(gk), r3(gv), r3(sg), r3(laf), r3(lab), p["gla_gain"])
    h, hn = _out_proj(x2, oa.reshape(n, -1), og.reshape(n, -1), p["attn_gain"],
                      p["wo_a"], p["wo_g"], p["ffn_g"])
    sel, row = _peer_route(hn, p["wqt"], p["keys"])
    y = _peer_dense(hn, h, p["u"], p["vt"], sel, row, p["final_g"])
    return y.reshape(b, t, d)


def kernel(x_prompt, x_sample, norm_mix_g, w_in, q_norm_g, k_norm_g, w_gate_f2, b_gate_f,
           w_gate_b2, b_gate_b, gla_norm_g, attn_out_norm_g, w_out, norm_ffn_g, peer_w_q,
           peer_sub_keys, peer_u, peer_v, final_norm_g):
    p = _prepare(norm_mix_g, w_in, q_norm_g, k_norm_g, w_gate_f2, b_gate_f, w_gate_b2,
                 b_gate_b, gla_norm_g, attn_out_norm_g, w_out, norm_ffn_g, peer_w_q,
                 peer_sub_keys, peer_u, peer_v, final_norm_g)
    return (_layer(x_prompt, p), _layer(x_sample, p))
```
